```python
import math
import jax, jax.numpy as jnp
from jax import lax
import numpy as np

D_MODEL = 1024
BATCH = 8
SEQ = 4096
DEPTH = 4

PLE_DIM = 256
HEAD_DIM = 64
LRU_WIDTH = D_MODEL // 2
LRU_BLOCKS = 8
LRU_BLOCK = LRU_WIDTH // LRU_BLOCKS
CONV_WIDTH = 4
LRU_C = 8.0
SB_HEADS = 8
SB_WIDTH = SB_HEADS * HEAD_DIM
DIFF_HEADS = 8
DIFF_QK = DIFF_HEADS * 2 * HEAD_DIM
DIFF_V = DIFF_HEADS * 2 * HEAD_DIM
EVEN_IN = 2 * LRU_WIDTH + 4 * SB_WIDTH
EVEN_MIX = LRU_WIDTH + SB_WIDTH
ODD_IN = 2 * DIFF_QK + 2 * DIFF_V
ODD_MIX = DIFF_V
Q_BLOCK = 128
ROPE_THETA = 10000.0
EPS = 1e-6

kernel_name = 'hybrid_rglru_stickbreak_diffattn_trunk'


def rmsnorm(x, g):
    xf = x.astype(jnp.float32)
    ms = jnp.mean(xf * xf, axis=-1, keepdims=True)
    return (xf * lax.rsqrt(ms + EPS) * g.astype(jnp.float32)).astype(x.dtype)


def rope(x, positions):
    dh = x.shape[-1]
    inv_freq = ROPE_THETA ** (-jnp.arange(0, dh, 2, dtype=jnp.float32) / dh)
    ang = positions.astype(jnp.float32)[..., None] * inv_freq
    cos = jnp.cos(ang)[:, :, None, :]
    sin = jnp.sin(ang)[:, :, None, :]
    xf = x.astype(jnp.float32)
    x1, x2 = xf[..., : dh // 2], xf[..., dh // 2:]
    out = jnp.concatenate([x1 * cos - x2 * sin, x2 * cos + x1 * sin], axis=-1)
    return out.astype(x.dtype)


def causal_depthwise_conv(x, w, b):
    S = x.shape[1]
    xp = jnp.pad(x, ((0, 0), (CONV_WIDTH - 1, 0), (0, 0)))
    out = b
    for k in range(CONV_WIDTH):
        out = out + w[k] * xp[:, k:k + S]
    return out


def block_diag_linear(x, w, b):
    B, S, C = x.shape
    xb = x.reshape(B, S, LRU_BLOCKS, LRU_BLOCK)
    return jnp.einsum('bsnc,ncd->bsnd', xb, w).reshape(B, S, C) + b


def rg_lru(x, w_a, b_a, w_x, b_x, lam):
    r = jax.nn.sigmoid(block_diag_linear(x, w_a, b_a).astype(jnp.float32))
    i = jax.nn.sigmoid(block_diag_linear(x, w_x, b_x).astype(jnp.float32))
    log_a = LRU_C * r * jax.nn.log_sigmoid(lam.astype(jnp.float32))
    a = jnp.exp(log_a)
    u = jnp.sqrt(-jnp.expm1(2.0 * log_a)) * (i * x.astype(jnp.float32))

    def combine(left, right):
        a_l, b_l = left
        a_r, b_r = right
        return a_l * a_r, a_r * b_l + b_r

    _, h = lax.associative_scan(combine, (a, u), axis=1)
    return h.astype(x.dtype)


def stick_breaking_attention(q, k, v):
    B, S, H, Dh = q.shape
    scale = Dh ** -0.5
    outs = []
    for start in range(0, S, Q_BLOCK):
        end = start + Q_BLOCK
        z = jnp.einsum('bqhd,bkhd->bhqk', q[:, start:end].astype(jnp.float32),
                       k[:, :end].astype(jnp.float32)) * scale
        mask = jnp.arange(end)[None, :] < (start + jnp.arange(Q_BLOCK))[:, None]
        log_1m = jnp.where(mask, jax.nn.log_sigmoid(-z), 0.0)
        cs = jnp.cumsum(log_1m, axis=-1)
        log_w = jax.nn.log_sigmoid(z) + (cs[..., -1:] - cs)
        w = jnp.where(mask, jnp.exp(log_w), 0.0)
        outs.append(jnp.einsum('bhqk,bkhd->bqhd', w.astype(v.dtype), v[:, :end]))
    return jnp.concatenate(outs, axis=1)


def _causal_softmax_block(q_blk, k_pre, mask, scale):
    s = jnp.einsum('bqhd,bkhd->bhqk', q_blk.astype(jnp.float32), k_pre.astype(jnp.float32)) * scale
    s = jnp.where(mask, s, -jnp.inf)
    return jax.nn.softmax(s, axis=-1)


def differential_attention(q1, q2, k1, k2, v, lam):
    B, S, H, Dh = q1.shape
    scale = Dh ** -0.5
    outs = []
    for start in range(0, S, Q_BLOCK):
        end = start + Q_BLOCK
        mask = jnp.arange(end)[None, :] <= (start + jnp.arange(Q_BLOCK))[:, None]
        p1 = _causal_softmax_block(q1[:, start:end], k1[:, :end], mask, scale)
        p2 = _causal_softmax_block(q2[:, start:end], k2[:, :end], mask, scale)
        w = p1 - lam.astype(jnp.float32) * p2
        outs.append(jnp.einsum('bhqk,bkhd->bqhd', w.astype(v.dtype), v[:, :end]))
    return jnp.concatenate(outs, axis=1)


def even_mixer(h, w_in, conv_w, conv_b, w_a, b_a, w_x, b_x, lam, w_out):
    B, S, _ = h.shape
    y = h @ w_in
    xa, ga, q, k, v, gb = jnp.split(
        y, [LRU_WIDTH, 2 * LRU_WIDTH, 2 * LRU_WIDTH + SB_WIDTH,
            2 * LRU_WIDTH + 2 * SB_WIDTH, 2 * LRU_WIDTH + 3 * SB_WIDTH], axis=-1)
    xa = causal_depthwise_conv(xa, conv_w, conv_b)
    oa = rg_lru(xa, w_a, b_a, w_x, b_x, lam) * jax.nn.silu(ga)
    q = q.reshape(B, S, SB_HEADS, HEAD_DIM)
    k = k.reshape(B, S, SB_HEADS, HEAD_DIM)
    v = v.reshape(B, S, SB_HEADS, HEAD_DIM)
    ob = stick_breaking_attention(q, k, v).reshape(B, S, SB_WIDTH) * jax.nn.silu(gb)
    return jnp.concatenate([oa, ob], axis=-1) @ w_out


def odd_mixer(h, positions, w_in, lq1, lk1, lq2, lk2, subln_g, w_out, lambda_init):
    B, S, _ = h.shape
    y = h @ w_in
    q, k, v, g = jnp.split(y, [DIFF_QK, 2 * DIFF_QK, 2 * DIFF_QK + DIFF_V], axis=-1)
    q = rope(q.reshape(B, S, 2 * DIFF_HEADS, HEAD_DIM), positions).reshape(B, S, DIFF_HEADS, 2, HEAD_DIM)
    k = rope(k.reshape(B, S, 2 * DIFF_HEADS, HEAD_DIM), positions).reshape(B, S, DIFF_HEADS, 2, HEAD_DIM)
    v = v.reshape(B, S, DIFF_HEADS, 2 * HEAD_DIM)
    lam = (jnp.exp(jnp.sum(lq1.astype(jnp.float32) * lk1.astype(jnp.float32)))
           - jnp.exp(jnp.sum(lq2.astype(jnp.float32) * lk2.astype(jnp.float32))) + lambda_init)
    o = differential_attention(q[:, :, :, 0], q[:, :, :, 1], k[:, :, :, 0], k[:, :, :, 1], v, lam)
    o = rmsnorm(o, subln_g) * (1.0 - lambda_init)
    o = o.reshape(B, S, ODD_MIX) * jax.nn.silu(g)
    return o @ w_out


def setup_inputs(seed: int = 0) -> dict:
    key = jax.random.key(seed)
    n_even = (DEPTH + 1) // 2
    n_odd = DEPTH // 2
    ks = jax.random.split(key, 24)

    def nrm(k, shape, scale):
        return scale * jax.random.normal(k, shape, jnp.float32)

    x = nrm(ks[0], (BATCH, SEQ, D_MODEL), 1.0)
    p = nrm(ks[1], (DEPTH, BATCH, SEQ, PLE_DIM), 1.0)
    positions = jnp.broadcast_to(jnp.arange(SEQ, dtype=jnp.int32), (BATCH, SEQ))
    norm_mix = 1.0 + nrm(ks[2], (DEPTH, D_MODEL), 0.02)
    norm_ple = 1.0 + nrm(ks[3], (DEPTH, D_MODEL), 0.02)
    w_ple_gate = nrm(ks[4], (DEPTH, D_MODEL, D_MODEL), D_MODEL ** -0.5)
    w_ple_proj = nrm(ks[5], (DEPTH, PLE_DIM, D_MODEL), 0.5 * PLE_DIM ** -0.5)
    w_in_e = nrm(ks[6], (n_even, D_MODEL, EVEN_IN), D_MODEL ** -0.5)
    conv_w = nrm(ks[7], (n_even, CONV_WIDTH, LRU_WIDTH), CONV_WIDTH ** -0.5)
    conv_b = nrm(ks[8], (n_even, LRU_WIDTH), 0.02)
    lru_wa = nrm(ks[9], (n_even, LRU_BLOCKS, LRU_BLOCK, LRU_BLOCK), LRU_BLOCK ** -0.5)
    lru_ba = nrm(ks[10], (n_even, LRU_WIDTH), 0.1)
    lru_wx = nrm(ks[11], (n_even, LRU_BLOCKS, LRU_BLOCK, LRU_BLOCK), LRU_BLOCK ** -0.5)
    lru_bx = nrm(ks[12], (n_even, LRU_WIDTH), 0.1)
    u = jax.random.uniform(ks[13], (n_even, LRU_WIDTH), jnp.float32, minval=0.9, maxval=0.999)
    a0 = u ** (1.0 / LRU_C)
    lru_lambda = jnp.log(a0) - jnp.log1p(-a0)
    w_out_e = nrm(ks[14], (n_even, EVEN_MIX, D_MODEL), EVEN_MIX ** -0.5)
    w_in_o = nrm(ks[15], (n_odd, D_MODEL, ODD_IN), D_MODEL ** -0.5)
    lam_q1 = nrm(ks[16], (n_odd, HEAD_DIM), 0.1)
    lam_k1 = nrm(ks[17], (n_odd, HEAD_DIM), 0.1)
    lam_q2 = nrm(ks[18], (n_odd, HEAD_DIM), 0.1)
    lam_k2 = nrm(ks[19], (n_odd, HEAD_DIM), 0.1)
    subln_g = 1.0 + nrm(ks[20], (n_odd, 2 * HEAD_DIM), 0.02)
    w_out_o = nrm(ks[21], (n_odd, ODD_MIX, D_MODEL), ODD_MIX ** -0.5)
    final_norm = 1.0 + nrm(ks[22], (D_MODEL,), 0.02)
    return {'x': x, 'p': p, 'positions': positions, 'norm_mix': norm_mix, 'norm_ple': norm_ple,
            'w_ple_gate': w_ple_gate, 'w_ple_proj': w_ple_proj, 'w_in_e': w_in_e,
            'conv_w': conv_w, 'conv_b': conv_b, 'lru_wa': lru_wa, 'lru_ba': lru_ba,
            'lru_wx': lru_wx, 'lru_bx': lru_bx, 'lru_lambda': lru_lambda, 'w_out_e': w_out_e,
            'w_in_o': w_in_o, 'lam_q1': lam_q1, 'lam_k1': lam_k1, 'lam_q2': lam_q2,
            'lam_k2': lam_k2, 'subln_g': subln_g, 'w_out_o': w_out_o, 'final_norm': final_norm}


def reference(x, p, positions, norm_mix, norm_ple, w_ple_gate, w_ple_proj, w_in_e,
              conv_w, conv_b, lru_wa, lru_ba, lru_wx, lru_bx, lru_lambda, w_out_e,
              w_in_o, lam_q1, lam_k1, lam_q2, lam_k2, subln_g, w_out_o, final_norm):
    h = x
    for i in range(DEPTH):
        j = i // 2
        hn = rmsnorm(h, norm_mix[i])
        if i % 2 == 0:
            mix = even_mixer(hn, w_in_e[j], conv_w[j], conv_b[j], lru_wa[j], lru_ba[j],
                             lru_wx[j], lru_bx[j], lru_lambda[j], w_out_e[j])
        else:
            lambda_init = 0.8 - 0.6 * math.exp(-0.3 * i)
            mix = odd_mixer(hn, positions, w_in_o[j], lam_q1[j], lam_k1[j], lam_q2[j],
                            lam_k2[j], subln_g[j], w_out_o[j], lambda_init)
        h = h + mix
        gate = jax.nn.sigmoid(rmsnorm(h, norm_ple[i]) @ w_ple_gate[i])
        h = h + gate * (p[i] @ w_ple_proj[i])
    return rmsnorm(h, final_norm)
```

```python
import functools
import math

import jax
import jax.numpy as jnp
from jax import lax
from jax.experimental import pallas as pl
from jax.experimental.pallas import tpu as pltpu

F32 = jnp.float32
BF16 = jnp.bfloat16

EPS = 1e-6
HEAD_DIM = 64
LRU_C = 8.0
CONV_WIDTH = 4
ROPE_THETA = 10000.0

LANES = 128
SUBLANES = 8
KV_TILE = 256
Q_TILE = 256
SEGMENT = KV_TILE // SUBLANES
VMEM_LIMIT = 48 * 1024 * 1024


def _params(n_axes):
    return pltpu.CompilerParams(
        dimension_semantics=("arbitrary",) * n_axes, vmem_limit_bytes=VMEM_LIMIT)


def _sigmoid(x):
    return 1.0 / (1.0 + jnp.exp(-x))


def _silu(x):
    return x * _sigmoid(x)


def _in_proj_kernel(*refs, n_out, chunk, rope_cols, scale_range):
    if rope_cols:
        x_ref, g_ref, w_ref, cos_ref, sin_ref, o_ref = refs
    else:
        x_ref, g_ref, w_ref, o_ref = refs
    x = x_ref[...]
    ms = jnp.mean(x * x, axis=-1, keepdims=True)
    xn = (x * lax.rsqrt(ms + EPS) * g_ref[...]).astype(BF16)
    if rope_cols:
        reps = chunk // LANES
        cos = jnp.concatenate([cos_ref[...]] * reps, axis=1)
        sin = jnp.concatenate([sin_ref[...]] * reps, axis=1)
        lane = lax.broadcasted_iota(jnp.int32, (x.shape[0], chunk), 1)
        first_half = (lane % HEAD_DIM) < (HEAD_DIM // 2)
    for c in range(n_out // chunk):
        lo = c * chunk
        y = jnp.dot(xn, w_ref[:, lo:lo + chunk], preferred_element_type=F32)
        if lo < rope_cols:
            partner = jnp.where(first_half,
                                pltpu.roll(y, chunk - HEAD_DIM // 2, 1),
                                pltpu.roll(y, HEAD_DIM // 2, 1))
            y = y * cos + partner * sin
        if scale_range[0] <= lo < scale_range[1]:
            y = y * (HEAD_DIM ** -0.5)
        o_ref[:, lo:lo + chunk] = y.astype(BF16)


def _in_proj(h, g, w, cos=None, sin=None, *, rope_cols=0, scale_range, tm=512, chunk=512):
    t, d = h.shape
    n_out = w.shape[1]
    in_specs = [pl.BlockSpec((tm, d), lambda i: (i, 0)),
                pl.BlockSpec((1, d), lambda i: (0, 0)),
                pl.BlockSpec((d, n_out), lambda i: (0, 0))]
    args = [h, g.reshape(1, d), w]
    if rope_cols:
        in_specs += [pl.BlockSpec((tm, LANES), lambda i: (i, 0))] * 2
        args += [cos, sin]
    return pl.pallas_call(
        functools.partial(_in_proj_kernel, n_out=n_out, chunk=chunk,
                          rope_cols=rope_cols, scale_range=scale_range),
        grid=(t // tm,),
        in_specs=in_specs,
        out_specs=pl.BlockSpec((tm, n_out), lambda i: (i, 0)),
        out_shape=jax.ShapeDtypeStruct((t, n_out), BF16),
        compiler_params=_params(1),
        name="in_proj_rope" if rope_cols else "in_proj",
    )(*args)


def _lru_kernel(xa_ref, ga_ref, cw_ref, cb_ref, wg_ref, bg_ref, lam_ref, o_ref,
                ext_ref, h_ref, *, ts, width):
    @pl.when(pl.program_id(1) == 0)
    def _():
        ext_ref[0:SUBLANES, :] = jnp.zeros((SUBLANES, width), F32)
        h_ref[...] = jnp.zeros((1, width), F32)

    xa = xa_ref[...].astype(F32)
    ext_ref[SUBLANES:SUBLANES + ts, :] = xa
    xc = cb_ref[...] + cw_ref[CONV_WIDTH - 1:CONV_WIDTH, :] * xa
    for j in range(1, CONV_WIDTH):
        k = CONV_WIDTH - 1 - j
        xc = xc + cw_ref[k:k + 1, :] * ext_ref[SUBLANES - j:SUBLANES - j + ts, :]
    ext_ref[0:SUBLANES, :] = ext_ref[ts:ts + SUBLANES, :]

    gates = jnp.dot(xc.astype(BF16), wg_ref[...], preferred_element_type=F32) + bg_ref[...]
    r = _sigmoid(gates[:, :width])
    i = _sigmoid(gates[:, width:])
    lam = lam_ref[...]
    log_sig_lam = jnp.minimum(lam, 0.0) - jnp.log(1.0 + jnp.exp(-jnp.abs(lam)))
    log_a = LRU_C * r * log_sig_lam
    a = jnp.exp(log_a)
    u = jnp.sqrt(1.0 - jnp.exp(2.0 * log_a)) * (i * xc)

    row = lax.broadcasted_iota(jnp.int32, (ts, width), 0)
    step = 1
    while step < ts:
        valid = row >= step
        a_prev = pltpu.roll(a, step, 0)
        u_prev = pltpu.roll(u, step, 0)
        u = jnp.where(valid, a * u_prev + u, u)
        a = jnp.where(valid, a * a_prev, a)
        step *= 2
    h = u + a * h_ref[...]
    h_ref[...] = h[ts - 1:ts, :]
    o_ref[...] = (h * _silu(ga_ref[...].astype(F32))).astype(BF16)


def _lru(y, conv_w, conv_b, w_gates, b_gates, lam, *, batch, seq, width, ts=512):
    ns = seq // ts
    return pl.pallas_call(
        functools.partial(_lru_kernel, ts=ts, width=width),
        grid=(batch, ns),
        in_specs=[pl.BlockSpec((ts, width), lambda b, s: (b * ns + s, 0)),
                  pl.BlockSpec((ts, width), lambda b, s: (b * ns + s, 1)),
                  pl.BlockSpec((CONV_WIDTH, width), lambda b, s: (0, 0)),
                  pl.BlockSpec((1, width), lambda b, s: (0, 0)),
                  pl.BlockSpec((width, 2 * width), lambda b, s: (0, 0)),
                  pl.BlockSpec((1, 2 * width), lambda b, s: (0, 0)),
                  pl.BlockSpec((1, width), lambda b, s: (0, 0))],
        out_specs=pl.BlockSpec((ts, width), lambda b, s: (b * ns + s, 0)),
        out_shape=jax.ShapeDtypeStruct((batch * seq, width), BF16),
        scratch_shapes=[pltpu.VMEM((ts + SUBLANES, width), F32),
                        pltpu.VMEM((1, width), F32)],
        compiler_params=_params(2),
        name="rg_lru",
    )(y, y, conv_w, conv_b.reshape(1, width), w_gates, b_gates.reshape(1, 2 * width),
      lam.reshape(1, width))


def _head_masked(q):
    lane = lax.broadcasted_iota(jnp.int32, q.shape, 1)
    zero = jnp.zeros_like(q)
    return [jnp.where(lane < HEAD_DIM, q, zero), jnp.where(lane >= HEAD_DIM, q, zero)]


def _scores_t(k_tile, q_masked):
    return lax.dot_general(k_tile, q_masked, (((1,), (1,)), ((), ())),
                           preferred_element_type=F32)


def _sb_kernel(q_ref, k_ref, vt_ref, g_ref, o_ref, *, tq):
    qi = pl.program_id(2)
    qm = _head_masked(q_ref[...])

    r = lax.broadcasted_iota(jnp.int32, (KV_TILE, tq), 0)
    c = lax.broadcasted_iota(jnp.int32, (KV_TILE, tq), 1)
    causal = ((r % SUBLANES) * SEGMENT + r // SUBLANES) < c
    srow = lax.broadcasted_iota(jnp.int32, (SUBLANES, tq), 0)

    def tile(kb, carry, masked):
        k_tile = k_ref[0, kb]
        vt_tile = vt_ref[0, kb]
        new = []
        for hd in range(2):
            later, acc = carry[hd]
            z = _scores_t(k_tile, qm[hd])
            sp = jnp.maximum(z, 0.0) + jnp.log(1.0 + jnp.exp(-jnp.abs(z)))
            if masked:
                sp = jnp.where(causal, sp, 0.0)
            run = jnp.zeros((SUBLANES, tq), F32)
            part = [None] * SEGMENT
            for v in reversed(range(SEGMENT)):
                rows = slice(v * SUBLANES, (v + 1) * SUBLANES)
                run = run + sp[rows]
                part[v] = z[rows] - run
            off = jnp.broadcast_to(later, (SUBLANES, tq))
            for s2 in range(1, SUBLANES):
                off = off + jnp.where(srow < s2, run[s2:s2 + 1, :], 0.0)
            w = jnp.concatenate([jnp.exp(p - off) for p in part], axis=0)
            if masked:
                w = jnp.where(causal, w, 0.0)
            acc = acc + jnp.dot(vt_tile[hd * HEAD_DIM:(hd + 1) * HEAD_DIM, :], w.astype(BF16),
                                preferred_element_type=F32)
            later = later + jnp.sum(run, axis=0, keepdims=True)
            new.append((later, acc))
        return tuple(new)

    init = tuple((jnp.zeros((1, tq), F32), jnp.zeros((HEAD_DIM, tq), F32)) for _ in range(2))
    carry = tile(qi, init, True)
    carry = lax.fori_loop(0, qi, lambda j, cr: tile(qi - 1 - j, cr, False), carry)
    out_t = jnp.concatenate([carry[0][1], carry[1][1]], axis=0)
    o_ref[...] = (out_t.T * _silu(g_ref[...].astype(F32))).astype(BF16)


def _sb_attention(y, k_tiles, vt_tiles, *, batch, seq, q_col, g_col, n_pairs):
    nq = seq // Q_TILE
    nkb = seq // KV_TILE
    return pl.pallas_call(
        functools.partial(_sb_kernel, tq=Q_TILE),
        grid=(batch, n_pairs, nq),
        in_specs=[pl.BlockSpec((Q_TILE, LANES), lambda b, p, i: (b * nq + i, q_col + p)),
                  pl.BlockSpec((1, nkb, KV_TILE, LANES), lambda b, p, i: (b, 0, 0, p)),
                  pl.BlockSpec((1, nkb, LANES, KV_TILE), lambda b, p, i: (b, 0, p, 0)),
                  pl.BlockSpec((Q_TILE, LANES), lambda b, p, i: (b * nq + i, g_col + p))],
        out_specs=pl.BlockSpec((Q_TILE, LANES), lambda b, p, i: (b * nq + i, p)),
        out_shape=jax.ShapeDtypeStruct((batch * seq, n_pairs * LANES), BF16),
        compiler_params=_params(3),
        name="stick_breaking_attention",
    )(y, k_tiles, vt_tiles, y)


def _diff_kernel(q_ref, k_ref, vt_ref, g_ref, lq1_ref, lk1_ref, lq2_ref, lk2_ref, sg_ref,
                 o_ref, *, tq, lambda_init):
    qi = pl.program_id(2)
    qm = _head_masked(q_ref[...])
    r = lax.broadcasted_iota(jnp.int32, (KV_TILE, tq), 0)
    c = lax.broadcasted_iota(jnp.int32, (KV_TILE, tq), 1)
    causal = r <= c

    def tile(kb, carry, masked):
        k_tile = k_ref[pl.ds(pl.multiple_of(kb * KV_TILE, KV_TILE), KV_TILE), :]
        vt_tile = vt_ref[0, kb]
        new = []
        for mp in range(2):
            m, l, acc = carry[mp]
            s = _scores_t(k_tile, qm[mp])
            if masked:
                s = jnp.where(causal, s, -jnp.inf)
            m_new = jnp.maximum(m, jnp.max(s, axis=0, keepdims=True))
            p = jnp.exp(s - m_new)
            alpha = jnp.exp(m - m_new)
            l = alpha * l + jnp.sum(p, axis=0, keepdims=True)
            acc = alpha * acc + jnp.dot(vt_tile, p.astype(BF16), preferred_element_type=F32)
            new.append((m_new, l, acc))
        return tuple(new)

    init = tuple((jnp.full((1, tq), -jnp.inf, F32), jnp.zeros((1, tq), F32),
                  jnp.zeros((2 * HEAD_DIM, tq), F32)) for _ in range(2))
    carry = tile(qi, init, True)
    carry = lax.fori_loop(0, qi, lambda j, cr: tile(qi - 1 - j, cr, False), carry)

    lam = (jnp.exp(jnp.sum(lq1_ref[...] * lk1_ref[...], axis=-1, keepdims=True))
           - jnp.exp(jnp.sum(lq2_ref[...] * lk2_ref[...], axis=-1, keepdims=True))
           + lambda_init)
    (_, l1, acc1), (_, l2, acc2) = carry
    out_t = acc1 / l1 - lam * (acc2 / l2)
    ms = jnp.mean(out_t * out_t, axis=0, keepdims=True)
    out = (out_t * lax.rsqrt(ms + EPS)).T
    out = out * sg_ref[...] * (1.0 - lambda_init)
    o_ref[...] = (out * _silu(g_ref[...].astype(F32))).astype(BF16)


def _diff_attention(y, vt_tiles, lq1, lk1, lq2, lk2, subln_g, *, batch, seq, n_heads,
                    k_col, g_col, lambda_init):
    nq = seq // Q_TILE
    nkb = seq // KV_TILE
    vec = lambda a: a.reshape(1, -1)
    small = lambda n: pl.BlockSpec((1, n), lambda b, h, i: (0, 0))
    return pl.pallas_call(
        functools.partial(_diff_kernel, tq=Q_TILE, lambda_init=lambda_init),
        grid=(batch, n_heads, nq),
        in_specs=[pl.BlockSpec((Q_TILE, LANES), lambda b, h, i: (b * nq + i, h)),
                  pl.BlockSpec((seq, LANES), lambda b, h, i: (b, k_col + h)),
                  pl.BlockSpec((1, nkb, LANES, KV_TILE), lambda b, h, i: (b, 0, h, 0)),
                  pl.BlockSpec((Q_TILE, LANES), lambda b, h, i: (b * nq + i, g_col + h)),
                  small(HEAD_DIM), small(HEAD_DIM), small(HEAD_DIM), small(HEAD_DIM),
                  small(2 * HEAD_DIM)],
        out_specs=pl.BlockSpec((Q_TILE, LANES), lambda b, h, i: (b * nq + i, h)),
        out_shape=jax.ShapeDtypeStruct((batch * seq, n_heads * LANES), BF16),
        compiler_params=_params(3),
        name="differential_attention",
    )(y, y, vt_tiles, y, vec(lq1), vec(lk1), vec(lq2), vec(lk2), vec(subln_g))


def _post_kernel(*refs, final):
    if final:
        (h_ref, ma_ref, mb_ref, p_ref, woa_ref, wob_ref, gp_ref, wg_ref, wp_ref,
         fn_ref, o_ref) = refs
    else:
        h_ref, ma_ref, mb_ref, p_ref, woa_ref, wob_ref, gp_ref, wg_ref, wp_ref, o_ref = refs
    h = (h_ref[...]
         + jnp.dot(ma_ref[...], woa_ref[...], preferred_element_type=F32)
         + jnp.dot(mb_ref[...], wob_ref[...], preferred_element_type=F32))
    ms = jnp.mean(h * h, axis=-1, keepdims=True)
    hn = (h * lax.rsqrt(ms + EPS) * gp_ref[...]).astype(BF16)
    gate = _sigmoid(jnp.dot(hn, wg_ref[...], preferred_element_type=F32))
    ple = jnp.dot(p_ref[...].astype(BF16), wp_ref[...], preferred_element_type=F32)
    h = h + gate * ple
    if final:
        ms = jnp.mean(h * h, axis=-1, keepdims=True)
        h = h * lax.rsqrt(ms + EPS) * fn_ref[...]
    o_ref[...] = h


def _post(h, mix_a, mix_b, b_col, p, w_out, g_ple, w_gate, w_proj, final_norm=None, *, tm=256):
    t, d = h.shape
    half = w_out.shape[0] // 2
    pd = p.shape[1]
    final = final_norm is not None
    const = lambda shape: pl.BlockSpec(shape, lambda i: (0, 0))
    in_specs = [pl.BlockSpec((tm, d), lambda i: (i, 0)),
                pl.BlockSpec((tm, half), lambda i: (i, 0)),
                pl.BlockSpec((tm, half), lambda i: (i, b_col)),
                pl.BlockSpec((tm, pd), lambda i: (i, 0)),
                pl.BlockSpec((half, d), lambda i: (0, 0)),
                pl.BlockSpec((half, d), lambda i: (1, 0)),
                const((1, d)), const((d, d)), const((pd, d))]
    args = [h, mix_a, mix_b, p, w_out, w_out, g_ple.reshape(1, d), w_gate, w_proj]
    if final:
        in_specs.append(const((1, d)))
        args.append(final_norm.reshape(1, d))
    return pl.pallas_call(
        functools.partial(_post_kernel, final=final),
        grid=(t // tm,),
        in_specs=in_specs,
        out_specs=pl.BlockSpec((tm, d), lambda i: (i, 0)),
        out_shape=jax.ShapeDtypeStruct((t, d), F32),
        compiler_params=_params(1),
        name="out_proj_ple_final" if final else "out_proj_ple",
    )(*args)


def _key_tiles(x, batch, seq, permute):
    cols = x.shape[1]
    nkb = seq // KV_TILE
    if permute:
        x = x.reshape(batch, nkb, SUBLANES, SEGMENT, cols)
        x = x.transpose(0, 1, 3, 2, 4)
    return x.reshape(batch, nkb, KV_TILE, cols)


def _block_diag(w):
    nb, bs, _ = w.shape
    eye = jnp.eye(nb, dtype=w.dtype)
    return jnp.einsum('ncd,nm->ncmd', w, eye).reshape(nb * bs, nb * bs)


def kernel(x, p, positions, norm_mix, norm_ple, w_ple_gate, w_ple_proj, w_in_e, conv_w, conv_b,
           lru_wa, lru_ba, lru_wx, lru_bx, lru_lambda, w_out_e, w_in_o, lam_q1, lam_k1, lam_q2,
           lam_k2, subln_g, w_out_o, final_norm):
    batch, seq, d = x.shape
    depth = norm_mix.shape[0]
    t = batch * seq
    lru_w = conv_w.shape[-1]
    sb_w = (w_in_e.shape[-1] - 2 * lru_w) // 4
    diff_w = w_in_o.shape[-1] // 4
    n_pairs = sb_w // LANES
    n_heads = diff_w // LANES

    half = HEAD_DIM // 2
    inv_freq = ROPE_THETA ** (-jnp.arange(0, HEAD_DIM, 2, dtype=F32) / HEAD_DIM)
    ang = positions.astype(F32).reshape(t, 1) * inv_freq
    cos = jnp.tile(jnp.cos(ang), (1, LANES // half))
    sin = jnp.tile(jnp.concatenate([-jnp.sin(ang), jnp.sin(ang)], axis=1), (1, LANES // HEAD_DIM))

    h = x.reshape(t, d)
    for i in range(depth):
        j = i // 2
        last = i == depth - 1
        if i % 2 == 0:
            q0 = 2 * lru_w
            y = _in_proj(h, norm_mix[i], w_in_e[j].astype(BF16), scale_range=(q0, q0 + sb_w))
            w_gates = jnp.concatenate([_block_diag(lru_wa[j]), _block_diag(lru_wx[j])], axis=1)
            b_gates = jnp.concatenate([lru_ba[j], lru_bx[j]])
            oa = _lru(y, conv_w[j], conv_b[j], w_gates.astype(BF16), b_gates, lru_lambda[j],
                      batch=batch, seq=seq, width=lru_w)
            k_tiles = _key_tiles(y[:, q0 + sb_w:q0 + 2 * sb_w], batch, seq, True)
            v_tiles = _key_tiles(y[:, q0 + 2 * sb_w:q0 + 3 * sb_w], batch, seq, True)
            vt_tiles = v_tiles.transpose(0, 1, 3, 2)
            ob = _sb_attention(y, k_tiles, vt_tiles, batch=batch, seq=seq,
                               q_col=q0 // LANES, g_col=(q0 + 3 * sb_w) // LANES,
                               n_pairs=n_pairs)
            mix_a, mix_b, b_col, w_out = oa, ob, 0, w_out_e[j]
        else:
            lambda_init = 0.8 - 0.6 * math.exp(-0.3 * i)
            y = _in_proj(h, norm_mix[i], w_in_o[j].astype(BF16), cos, sin,
                         rope_cols=2 * diff_w, scale_range=(0, diff_w))
            vt_tiles = _key_tiles(y[:, 2 * diff_w:3 * diff_w], batch, seq,
                                  False).transpose(0, 1, 3, 2)
            o = _diff_attention(y, vt_tiles, lam_q1[j], lam_k1[j], lam_q2[j], lam_k2[j],
                                subln_g[j], batch=batch, seq=seq, n_heads=n_heads,
                                k_col=diff_w // LANES, g_col=3 * diff_w // LANES,
                                lambda_init=lambda_init)
            mix_a, mix_b, b_col, w_out = o, o, 1, w_out_o[j]
        h = _post(h, mix_a, mix_b, b_col, p[i].reshape(t, -1), w_out.astype(BF16), norm_ple[i],
                  w_ple_gate[i].astype(BF16), w_ple_proj[i].astype(BF16),
                  final_norm if last else None)
    return h.reshape(batch, seq, d)
```

```python
import functools
import math

import jax
import jax.numpy as jnp
from jax import lax
from jax.experimental import pallas as pl
from jax.experimental.pallas import tpu as pltpu

F32 = jnp.float32
BF16 = jnp.bfloat16

EPS = 1e-6
HEAD_DIM = 64
LRU_C = 8.0
CONV_WIDTH = 4
ROPE_THETA = 10000.0

LANES = 128
SUBLANES = 8
KV_TILE = 256
Q_TILE = 256
SEGMENT = KV_TILE // SUBLANES
STREAMS = 4
ATTN_COLS = STREAMS * HEAD_DIM
VMEM_LIMIT = 48 * 1024 * 1024


def _params(n_axes):
    return pltpu.CompilerParams(
        dimension_semantics=("arbitrary",) * n_axes, vmem_limit_bytes=VMEM_LIMIT)


def _sigmoid(x):
    return 1.0 / (1.0 + jnp.exp(-x))


def _silu(x):
    return x * _sigmoid(x)


def _in_proj_kernel(*refs, n_out, chunk, rope_cols, scale_range, scale):
    if rope_cols:
        x_ref, g_ref, w_ref, cos_ref, sin_ref, o_ref = refs
    else:
        x_ref, g_ref, w_ref, o_ref = refs
    x = x_ref[...]
    ms = jnp.mean(x * x, axis=-1, keepdims=True)
    xn = (x * lax.rsqrt(ms + EPS) * g_ref[...]).astype(BF16)
    if rope_cols:
        reps = chunk // LANES
        cos = jnp.concatenate([cos_ref[...]] * reps, axis=1)
        sin = jnp.concatenate([sin_ref[...]] * reps, axis=1)
        lane = lax.broadcasted_iota(jnp.int32, (x.shape[0], chunk), 1)
        first_half = (lane % HEAD_DIM) < (HEAD_DIM // 2)
    for c in range(n_out // chunk):
        lo = c * chunk
        y = jnp.dot(xn, w_ref[:, lo:lo + chunk], preferred_element_type=F32)
        if lo < rope_cols:
            partner = jnp.where(first_half,
                                pltpu.roll(y, chunk - HEAD_DIM // 2, 1),
                                pltpu.roll(y, HEAD_DIM // 2, 1))
            y = y * cos + partner * sin
        if scale_range[0] <= lo < scale_range[1]:
            y = y * scale
        o_ref[:, lo:lo + chunk] = y.astype(BF16)


def _in_proj(h, g, w, cos=None, sin=None, *, rope_cols=0, scale_range, scale, tm=512,
             chunk=512):
    t, d = h.shape
    n_out = w.shape[1]
    in_specs = [pl.BlockSpec((tm, d), lambda i: (i, 0)),
                pl.BlockSpec((1, d), lambda i: (0, 0)),
                pl.BlockSpec((d, n_out), lambda i: (0, 0))]
    args = [h, g.reshape(1, d), w]
    if rope_cols:
        in_specs += [pl.BlockSpec((tm, LANES), lambda i: (i, 0))] * 2
        args += [cos, sin]
    return pl.pallas_call(
        functools.partial(_in_proj_kernel, n_out=n_out, chunk=chunk,
                          rope_cols=rope_cols, scale_range=scale_range, scale=scale),
        grid=(t // tm,),
        in_specs=in_specs,
        out_specs=pl.BlockSpec((tm, n_out), lambda i: (i, 0)),
        out_shape=jax.ShapeDtypeStruct((t, n_out), BF16),
        compiler_params=_params(1),
        name="in_proj_rope" if rope_cols else "in_proj",
    )(*args)


def _lru_kernel(xa_ref, ga_ref, cw_ref, cb_ref, wg_ref, bg_ref, lam_ref, o_ref,
                ext_ref, h_ref, *, ts, width):
    @pl.when(pl.program_id(1) == 0)
    def _():
        ext_ref[0:SUBLANES, :] = jnp.zeros((SUBLANES, width), F32)
        h_ref[...] = jnp.zeros((1, width), F32)

    xa = xa_ref[...].astype(F32)
    ext_ref[SUBLANES:SUBLANES + ts, :] = xa
    xc = cb_ref[...] + cw_ref[CONV_WIDTH - 1:CONV_WIDTH, :] * xa
    for j in range(1, CONV_WIDTH):
        k = CONV_WIDTH - 1 - j
        xc = xc + cw_ref[k:k + 1, :] * ext_ref[SUBLANES - j:SUBLANES - j + ts, :]
    ext_ref[0:SUBLANES, :] = ext_ref[ts:ts + SUBLANES, :]

    gates = jnp.dot(xc.astype(BF16), wg_ref[...], preferred_element_type=F32) + bg_ref[...]
    r = _sigmoid(gates[:, :width])
    i = _sigmoid(gates[:, width:])
    lam = lam_ref[...]
    log_sig_lam = jnp.minimum(lam, 0.0) - jnp.log(1.0 + jnp.exp(-jnp.abs(lam)))
    log_a = LRU_C * r * log_sig_lam
    a = jnp.exp(log_a)
    u = jnp.sqrt(1.0 - jnp.exp(2.0 * log_a)) * (i * xc)

    row = lax.broadcasted_iota(jnp.int32, (ts, width), 0)
    step = 1
    while step < ts:
        valid = row >= step
        a_prev = pltpu.roll(a, step, 0)
        u_prev = pltpu.roll(u, step, 0)
        u = jnp.where(valid, a * u_prev + u, u)
        a = jnp.where(valid, a * a_prev, a)
        step *= 2
    h = u + a * h_ref[...]
    h_ref[...] = h[ts - 1:ts, :]
    o_ref[...] = (h * _silu(ga_ref[...].astype(F32))).astype(BF16)


def _lru(y, conv_w, conv_b, w_gates, b_gates, lam, *, batch, seq, width, ts=512):
    ns = seq // ts
    return pl.pallas_call(
        functools.partial(_lru_kernel, ts=ts, width=width),
        grid=(batch, ns),
        in_specs=[pl.BlockSpec((ts, width), lambda b, s: (b * ns + s, 0)),
                  pl.BlockSpec((ts, width), lambda b, s: (b * ns + s, 1)),
                  pl.BlockSpec((CONV_WIDTH, width), lambda b, s: (0, 0)),
                  pl.BlockSpec((1, width), lambda b, s: (0, 0)),
                  pl.BlockSpec((width, 2 * width), lambda b, s: (0, 0)),
                  pl.BlockSpec((1, 2 * width), lambda b, s: (0, 0)),
                  pl.BlockSpec((1, width), lambda b, s: (0, 0))],
        out_specs=pl.BlockSpec((ts, width), lambda b, s: (b * ns + s, 0)),
        out_shape=jax.ShapeDtypeStruct((batch * seq, width), BF16),
        scratch_shapes=[pltpu.VMEM((ts + SUBLANES, width), F32),
                        pltpu.VMEM((1, width), F32)],
        compiler_params=_params(2),
        name="rg_lru",
    )(y, y, conv_w, conv_b.reshape(1, width), w_gates, b_gates.reshape(1, 2 * width),
      lam.reshape(1, width))


def _store_stream_queries(q_ref, qt_scr):
    q_t = q_ref[...].astype(F32).T
    row = lax.broadcasted_iota(jnp.int32, q_t.shape, 0)
    for g in range(STREAMS):
        qt_scr[g] = jnp.where(row // HEAD_DIM == g, q_t, 0.0).astype(BF16)


def _score_stage(k_tile, qt_scr, s_scr):
    for g in range(STREAMS):
        s_scr[g] = jnp.dot(k_tile, qt_scr[g], preferred_element_type=F32)


def _attention_scratch(tq, value_rows):
    return [pltpu.VMEM((STREAMS, ATTN_COLS, tq), BF16),
            pltpu.VMEM((STREAMS, KV_TILE, tq), F32),
            pltpu.VMEM((STREAMS, KV_TILE, tq), BF16),
            pltpu.VMEM((STREAMS, value_rows, tq), F32)]


def _sb_kernel(q_ref, k_ref, vt_ref, g_ref, o_ref, qt_scr, s_scr, w_scr, acc_scr, *, tq):
    qi = pl.program_id(2)
    _store_stream_queries(q_ref, qt_scr)
    acc_scr[...] = jnp.zeros(acc_scr.shape, F32)

    r = lax.broadcasted_iota(jnp.int32, (KV_TILE, tq), 0)
    c = lax.broadcasted_iota(jnp.int32, (KV_TILE, tq), 1)
    causal = ((r % SUBLANES) * SEGMENT + r // SUBLANES) < c
    srow = lax.broadcasted_iota(jnp.int32, (SUBLANES, tq), 0)

    def tile_index(t):
        return jnp.maximum(qi - t, 0)

    def weight_stage(later, masked):
        new_later = []
        for g in range(STREAMS):
            half_tanh = 0.5 * jnp.tanh(s_scr[g])
            beta = 0.5 + half_tanh
            rest = 0.5 - half_tanh
            if masked:
                beta = jnp.where(causal, beta, 0.0)
                rest = jnp.where(causal, rest, 1.0)
            run = jnp.ones((SUBLANES, tq), F32)
            part = [None] * SEGMENT
            for v in reversed(range(SEGMENT)):
                rows = slice(v * SUBLANES, (v + 1) * SUBLANES)
                part[v] = beta[rows] * run
                run = run * rest[rows]
            off = jnp.broadcast_to(later[g], (SUBLANES, tq))
            for s2 in range(1, SUBLANES):
                off = off * jnp.where(srow < s2, run[s2:s2 + 1, :], 1.0)
            w_scr[g] = jnp.concatenate([p * off for p in part], axis=0).astype(BF16)
            new_later.append((off * run)[0:1, :])
        return tuple(new_later)

    def value_stage(t):
        vt_tile = vt_ref[0, tile_index(t)]
        for g in range(STREAMS):
            acc_scr[g] += jnp.dot(vt_tile[g * HEAD_DIM:(g + 1) * HEAD_DIM, :], w_scr[g],
                                  preferred_element_type=F32)

    _score_stage(k_ref[0, qi], qt_scr, s_scr)
    later = weight_stage(tuple(jnp.ones((1, tq), F32) for _ in range(STREAMS)), True)
    _score_stage(k_ref[0, tile_index(1)], qt_scr, s_scr)

    def step(j, later):
        value_stage(j - 1)
        later = weight_stage(later, False)
        _score_stage(k_ref[0, tile_index(j + 1)], qt_scr, s_scr)
        return later

    lax.fori_loop(1, qi + 1, step, later)
    value_stage(qi)
    out_t = acc_scr[...].reshape(STREAMS * HEAD_DIM, tq)
    o_ref[...] = (out_t.T * _silu(g_ref[...].astype(F32))).astype(BF16)


def _sb_attention(y, k_tiles, vt_tiles, *, batch, seq, q_col, g_col, n_groups):
    nq = seq // Q_TILE
    nkb = seq // KV_TILE
    return pl.pallas_call(
        functools.partial(_sb_kernel, tq=Q_TILE),
        grid=(batch, n_groups, nq),
        in_specs=[pl.BlockSpec((Q_TILE, ATTN_COLS), lambda b, p, i: (b * nq + i, q_col + p)),
                  pl.BlockSpec((1, nkb, KV_TILE, ATTN_COLS), lambda b, p, i: (b, 0, 0, p)),
                  pl.BlockSpec((1, nkb, ATTN_COLS, KV_TILE), lambda b, p, i: (b, 0, p, 0)),
                  pl.BlockSpec((Q_TILE, ATTN_COLS), lambda b, p, i: (b * nq + i, g_col + p))],
        out_specs=pl.BlockSpec((Q_TILE, ATTN_COLS), lambda b, p, i: (b * nq + i, p)),
        out_shape=jax.ShapeDtypeStruct((batch * seq, n_groups * ATTN_COLS), BF16),
        scratch_shapes=_attention_scratch(Q_TILE, HEAD_DIM),
        compiler_params=_params(3),
        name="stick_breaking_attention",
    )(y, k_tiles, vt_tiles, y)


def _diff_kernel(q_ref, k_ref, vt_ref, g_ref, lq1_ref, lk1_ref, lq2_ref, lk2_ref, sg_ref,
                 o_ref, qt_scr, s_scr, p_scr, acc_scr, *, tq, lambda_init):
    qi = pl.program_id(2)
    _store_stream_queries(q_ref, qt_scr)
    acc_scr[...] = jnp.zeros(acc_scr.shape, F32)
    r = lax.broadcasted_iota(jnp.int32, (KV_TILE, tq), 0)
    c = lax.broadcasted_iota(jnp.int32, (KV_TILE, tq), 1)
    causal = r <= c
    vdim = 2 * HEAD_DIM

    def tile_index(t):
        return jnp.maximum(qi - t, 0)

    def key_tile(t):
        start = pl.multiple_of(tile_index(t) * KV_TILE, KV_TILE)
        return k_ref[pl.ds(start, KV_TILE), :]

    def softmax_stage(stats, masked):
        new = []
        for g in range(STREAMS):
            m, l, _ = stats[g]
            s = s_scr[g]
            if masked:
                s = jnp.where(causal, s, -jnp.inf)
            m_new = jnp.maximum(m, jnp.max(s, axis=0, keepdims=True))
            p = jnp.exp(s - m_new)
            alpha = jnp.exp(m - m_new)
            l = alpha * l + jnp.sum(p, axis=0, keepdims=True)
            p_scr[g] = p.astype(BF16)
            new.append((m_new, l, alpha))
        return tuple(new)

    def value_stage(t, stats):
        vt_tile = vt_ref[0, tile_index(t)]
        for g in range(STREAMS):
            hd = g // 2
            acc_scr[g] = stats[g][2] * acc_scr[g] + jnp.dot(
                vt_tile[hd * vdim:(hd + 1) * vdim, :], p_scr[g], preferred_element_type=F32)

    init = tuple((jnp.full((1, tq), -jnp.inf, F32), jnp.zeros((1, tq), F32),
                  jnp.zeros((1, tq), F32)) for _ in range(STREAMS))
    _score_stage(key_tile(0), qt_scr, s_scr)
    stats = softmax_stage(init, True)
    _score_stage(key_tile(1), qt_scr, s_scr)

    def step(j, stats):
        value_stage(j - 1, stats)
        stats = softmax_stage(stats, False)
        _score_stage(key_tile(j + 1), qt_scr, s_scr)
        return stats

    stats = lax.fori_loop(1, qi + 1, step, stats)
    value_stage(qi, stats)

    lam = (jnp.exp(jnp.sum(lq1_ref[...] * lk1_ref[...], axis=-1, keepdims=True))
           - jnp.exp(jnp.sum(lq2_ref[...] * lk2_ref[...], axis=-1, keepdims=True))
           + lambda_init)
    for hd in range(STREAMS // 2):
        l1, l2 = stats[2 * hd][1], stats[2 * hd + 1][1]
        out_t = acc_scr[2 * hd] / l1 - lam * (acc_scr[2 * hd + 1] / l2)
        ms = jnp.mean(out_t * out_t, axis=0, keepdims=True)
        out = (out_t * lax.rsqrt(ms + EPS)).T
        out = out * sg_ref[...] * (1.0 - lambda_init)
        cols = slice(hd * vdim, (hd + 1) * vdim)
        o_ref[:, cols] = (out * _silu(g_ref[:, cols].astype(F32))).astype(BF16)


def _diff_attention(y, vt_tiles, lq1, lk1, lq2, lk2, subln_g, *, batch, seq, n_groups,
                    k_col, g_col, lambda_init):
    nq = seq // Q_TILE
    nkb = seq // KV_TILE
    vec = lambda a: a.reshape(1, -1)
    small = lambda n: pl.BlockSpec((1, n), lambda b, h, i: (0, 0))
    return pl.pallas_call(
        functools.partial(_diff_kernel, tq=Q_TILE, lambda_init=lambda_init),
        grid=(batch, n_groups, nq),
        in_specs=[pl.BlockSpec((Q_TILE, ATTN_COLS), lambda b, h, i: (b * nq + i, h)),
                  pl.BlockSpec((seq, ATTN_COLS), lambda b, h, i: (b, k_col + h)),
                  pl.BlockSpec((1, nkb, ATTN_COLS, KV_TILE), lambda b, h, i: (b, 0, h, 0)),
                  pl.BlockSpec((Q_TILE, ATTN_COLS), lambda b, h, i: (b * nq + i, g_col + h)),
                  small(HEAD_DIM), small(HEAD_DIM), small(HEAD_DIM), small(HEAD_DIM),
                  small(2 * HEAD_DIM)],
        out_specs=pl.BlockSpec((Q_TILE, ATTN_COLS), lambda b, h, i: (b * nq + i, h)),
        out_shape=jax.ShapeDtypeStruct((batch * seq, n_groups * ATTN_COLS), BF16),
        scratch_shapes=_attention_scratch(Q_TILE, 2 * HEAD_DIM),
        compiler_params=_params(3),
        name="differential_attention",
    )(y, y, vt_tiles, y, vec(lq1), vec(lk1), vec(lq2), vec(lk2), vec(subln_g))


def _post_kernel(*refs, final):
    if final:
        (h_ref, ma_ref, mb_ref, p_ref, woa_ref, wob_ref, gp_ref, wg_ref, wp_ref,
         fn_ref, o_ref) = refs
    else:
        h_ref, ma_ref, mb_ref, p_ref, woa_ref, wob_ref, gp_ref, wg_ref, wp_ref, o_ref = refs
    h = (h_ref[...]
         + jnp.dot(ma_ref[...], woa_ref[...], preferred_element_type=F32)
         + jnp.dot(mb_ref[...], wob_ref[...], preferred_element_type=F32))
    ms = jnp.mean(h * h, axis=-1, keepdims=True)
    hn = (h * lax.rsqrt(ms + EPS) * gp_ref[...]).astype(BF16)
    gate = _sigmoid(jnp.dot(hn, wg_ref[...], preferred_element_type=F32))
    ple = jnp.dot(p_ref[...].astype(BF16), wp_ref[...], preferred_element_type=F32)
    h = h + gate * ple
    if final:
        ms = jnp.mean(h * h, axis=-1, keepdims=True)
        h = h * lax.rsqrt(ms + EPS) * fn_ref[...]
    o_ref[...] = h


def _post(h, mix_a, mix_b, b_col, p, w_out, g_ple, w_gate, w_proj, final_norm=None, *, tm=256):
    t, d = h.shape
    half = w_out.shape[0] // 2
    pd = p.shape[1]
    final = final_norm is not None
    const = lambda shape: pl.BlockSpec(shape, lambda i: (0, 0))
    in_specs = [pl.BlockSpec((tm, d), lambda i: (i, 0)),
                pl.BlockSpec((tm, half), lambda i: (i, 0)),
                pl.BlockSpec((tm, half), lambda i: (i, b_col)),
                pl.BlockSpec((tm, pd), lambda i: (i, 0)),
                pl.BlockSpec((half, d), lambda i: (0, 0)),
                pl.BlockSpec((half, d), lambda i: (1, 0)),
                const((1, d)), const((d, d)), const((pd, d))]
    args = [h, mix_a, mix_b, p, w_out, w_out, g_ple.reshape(1, d), w_gate, w_proj]
    if final:
        in_specs.append(const((1, d)))
        args.append(final_norm.reshape(1, d))
    return pl.pallas_call(
        functools.partial(_post_kernel, final=final),
        grid=(t // tm,),
        in_specs=in_specs,
        out_specs=pl.BlockSpec((tm, d), lambda i: (i, 0)),
        out_shape=jax.ShapeDtypeStruct((t, d), F32),
        compiler_params=_params(1),
        name="out_proj_ple_final" if final else "out_proj_ple",
    )(*args)


def _key_tiles(x, batch, seq, permute):
    cols = x.shape[1]
    nkb = seq // KV_TILE
    if permute:
        x = x.reshape(batch, nkb, SUBLANES, SEGMENT, cols)
        x = x.transpose(0, 1, 3, 2, 4)
    return x.reshape(batch, nkb, KV_TILE, cols)


def _block_diag(w):
    nb, bs, _ = w.shape
    eye = jnp.eye(nb, dtype=w.dtype)
    return jnp.einsum('ncd,nm->ncmd', w, eye).reshape(nb * bs, nb * bs)


def kernel(x, p, positions, norm_mix, norm_ple, w_ple_gate, w_ple_proj, w_in_e, conv_w, conv_b,
           lru_wa, lru_ba, lru_wx, lru_bx, lru_lambda, w_out_e, w_in_o, lam_q1, lam_k1, lam_q2,
           lam_k2, subln_g, w_out_o, final_norm):
    batch, seq, d = x.shape
    depth = norm_mix.shape[0]
    t = batch * seq
    lru_w = conv_w.shape[-1]
    sb_w = (w_in_e.shape[-1] - 2 * lru_w) // 4
    diff_w = w_in_o.shape[-1] // 4

    half = HEAD_DIM // 2
    inv_freq = ROPE_THETA ** (-jnp.arange(0, HEAD_DIM, 2, dtype=F32) / HEAD_DIM)
    ang = positions.astype(F32).reshape(t, 1) * inv_freq
    cos = jnp.tile(jnp.cos(ang), (1, LANES // half))
    sin = jnp.tile(jnp.concatenate([-jnp.sin(ang), jnp.sin(ang)], axis=1), (1, LANES // HEAD_DIM))

    h = x.reshape(t, d)
    for i in range(depth):
        j = i // 2
        last = i == depth - 1
        if i % 2 == 0:
            q0 = 2 * lru_w
            y = _in_proj(h, norm_mix[i], w_in_e[j].astype(BF16), scale_range=(q0, q0 + sb_w),
                         scale=0.5 * HEAD_DIM ** -0.5)
            w_gates = jnp.concatenate([_block_diag(lru_wa[j]), _block_diag(lru_wx[j])], axis=1)
            b_gates = jnp.concatenate([lru_ba[j], lru_bx[j]])
            oa = _lru(y, conv_w[j], conv_b[j], w_gates.astype(BF16), b_gates, lru_lambda[j],
                      batch=batch, seq=seq, width=lru_w)
            k_tiles = _key_tiles(y[:, q0 + sb_w:q0 + 2 * sb_w], batch, seq, True)
            v_tiles = _key_tiles(y[:, q0 + 2 * sb_w:q0 + 3 * sb_w], batch, seq, True)
            vt_tiles = v_tiles.transpose(0, 1, 3, 2)
            ob = _sb_attention(y, k_tiles, vt_tiles, batch=batch, seq=seq,
                               q_col=q0 // ATTN_COLS, g_col=(q0 + 3 * sb_w) // ATTN_COLS,
                               n_groups=sb_w // ATTN_COLS)
            mix_a, mix_b, b_col, w_out = oa, ob, 0, w_out_e[j]
        else:
            lambda_init = 0.8 - 0.6 * math.exp(-0.3 * i)
            y = _in_proj(h, norm_mix[i], w_in_o[j].astype(BF16), cos, sin,
                         rope_cols=2 * diff_w, scale_range=(0, diff_w), scale=HEAD_DIM ** -0.5)
            vt_tiles = _key_tiles(y[:, 2 * diff_w:3 * diff_w], batch, seq,
                                  False).transpose(0, 1, 3, 2)
            o = _diff_attention(y, vt_tiles, lam_q1[j], lam_k1[j], lam_q2[j], lam_k2[j],
                                subln_g[j], batch=batch, seq=seq,
                                n_groups=diff_w // ATTN_COLS, k_col=diff_w // ATTN_COLS,
                                g_col=3 * diff_w // ATTN_COLS, lambda_init=lambda_init)
            mix_a, mix_b, b_col, w_out = o, o, 1, w_out_o[j]
        h = _post(h, mix_a, mix_b, b_col, p[i].reshape(t, -1), w_out.astype(BF16), norm_ple[i],
                  w_ple_gate[i].astype(BF16), w_ple_proj[i].astype(BF16),
                  final_norm if last else None)
    return h.reshape(batch, seq, d)
```

```python
import functools
import math

import jax
import jax.numpy as jnp
from jax import lax
from jax.experimental import pallas as pl
from jax.experimental.pallas import tpu as pltpu

F32 = jnp.float32
BF16 = jnp.bfloat16

EPS = 1e-6
HEAD_DIM = 64
LRU_C = 8.0
CONV_WIDTH = 4
ROPE_THETA = 10000.0

LANES = 128
SUBLANES = 8
KV_TILE = 256
Q_TILE = 256
SEGMENT = KV_TILE // SUBLANES
STREAMS = 4
ATTN_COLS = STREAMS * HEAD_DIM
VMEM_LIMIT = 48 * 1024 * 1024


def _params(n_axes):
    return pltpu.CompilerParams(
        dimension_semantics=("arbitrary",) * n_axes, vmem_limit_bytes=VMEM_LIMIT)


def _sigmoid(x):
    return 1.0 / (1.0 + jnp.exp(-x))


def _silu(x):
    return x * _sigmoid(x)


def _in_proj_kernel(*refs, n_out, chunk, rope_cols, scale_range, scale):
    if rope_cols:
        x_ref, g_ref, w_ref, cos_ref, sin_ref, o_ref = refs
    else:
        x_ref, g_ref, w_ref, o_ref = refs
    x = x_ref[...]
    ms = jnp.mean(x * x, axis=-1, keepdims=True)
    xn = (x * lax.rsqrt(ms + EPS) * g_ref[...]).astype(BF16)
    if rope_cols:
        reps = chunk // LANES
        cos = jnp.concatenate([cos_ref[...]] * reps, axis=1)
        sin = jnp.concatenate([sin_ref[...]] * reps, axis=1)
        lane = lax.broadcasted_iota(jnp.int32, (x.shape[0], chunk), 1)
        first_half = (lane % HEAD_DIM) < (HEAD_DIM // 2)
    for c in range(n_out // chunk):
        lo = c * chunk
        y = jnp.dot(xn, w_ref[:, lo:lo + chunk], preferred_element_type=F32)
        if lo < rope_cols:
            partner = jnp.where(first_half,
                                pltpu.roll(y, chunk - HEAD_DIM // 2, 1),
                                pltpu.roll(y, HEAD_DIM // 2, 1))
            y = y * cos + partner * sin
        if scale_range[0] <= lo < scale_range[1]:
            y = y * scale
        o_ref[:, lo:lo + chunk] = y.astype(BF16)


def _in_proj(h, g, w, cos=None, sin=None, *, rope_cols=0, scale_range, scale, tm=512,
             chunk=512):
    t, d = h.shape
    n_out = w.shape[1]
    in_specs = [pl.BlockSpec((tm, d), lambda i: (i, 0)),
                pl.BlockSpec((1, d), lambda i: (0, 0)),
                pl.BlockSpec((d, n_out), lambda i: (0, 0))]
    args = [h, g.reshape(1, d), w]
    if rope_cols:
        in_specs += [pl.BlockSpec((tm, LANES), lambda i: (i, 0))] * 2
        args += [cos, sin]
    return pl.pallas_call(
        functools.partial(_in_proj_kernel, n_out=n_out, chunk=chunk,
                          rope_cols=rope_cols, scale_range=scale_range, scale=scale),
        grid=(t // tm,),
        in_specs=in_specs,
        out_specs=pl.BlockSpec((tm, n_out), lambda i: (i, 0)),
        out_shape=jax.ShapeDtypeStruct((t, n_out), BF16),
        compiler_params=_params(1),
        name="in_proj_rope" if rope_cols else "in_proj",
    )(*args)


def _lru_kernel(xa_ref, ga_ref, cw_ref, cb_ref, wg_ref, bg_ref, lam_ref, o_ref,
                ext_ref, h_ref, *, ts, width):
    @pl.when(pl.program_id(1) == 0)
    def _():
        ext_ref[0:SUBLANES, :] = jnp.zeros((SUBLANES, width), F32)
        h_ref[...] = jnp.zeros((1, width), F32)

    xa = xa_ref[...].astype(F32)
    ext_ref[SUBLANES:SUBLANES + ts, :] = xa
    xc = cb_ref[...] + cw_ref[CONV_WIDTH - 1:CONV_WIDTH, :] * xa
    for j in range(1, CONV_WIDTH):
        k = CONV_WIDTH - 1 - j
        xc = xc + cw_ref[k:k + 1, :] * ext_ref[SUBLANES - j:SUBLANES - j + ts, :]
    ext_ref[0:SUBLANES, :] = ext_ref[ts:ts + SUBLANES, :]

    gates = jnp.dot(xc.astype(BF16), wg_ref[...], preferred_element_type=F32) + bg_ref[...]
    r = _sigmoid(gates[:, :width])
    i = _sigmoid(gates[:, width:])
    lam = lam_ref[...]
    log_sig_lam = jnp.minimum(lam, 0.0) - jnp.log(1.0 + jnp.exp(-jnp.abs(lam)))
    log_a = LRU_C * r * log_sig_lam
    a = jnp.exp(log_a)
    u = jnp.sqrt(1.0 - jnp.exp(2.0 * log_a)) * (i * xc)

    row = lax.broadcasted_iota(jnp.int32, (ts, width), 0)
    step = 1
    while step < ts:
        valid = row >= step
        a_prev = pltpu.roll(a, step, 0)
        u_prev = pltpu.roll(u, step, 0)
        u = jnp.where(valid, a * u_prev + u, u)
        a = jnp.where(valid, a * a_prev, a)
        step *= 2
    h = u + a * h_ref[...]
    h_ref[...] = h[ts - 1:ts, :]
    o_ref[...] = (h * _silu(ga_ref[...].astype(F32))).astype(BF16)


def _lru(y, conv_w, conv_b, w_gates, b_gates, lam, *, batch, seq, width, ts=512):
    ns = seq // ts
    return pl.pallas_call(
        functools.partial(_lru_kernel, ts=ts, width=width),
        grid=(batch, ns),
        in_specs=[pl.BlockSpec((ts, width), lambda b, s: (b * ns + s, 0)),
                  pl.BlockSpec((ts, width), lambda b, s: (b * ns + s, 1)),
                  pl.BlockSpec((CONV_WIDTH, width), lambda b, s: (0, 0)),
                  pl.BlockSpec((1, width), lambda b, s: (0, 0)),
                  pl.BlockSpec((width, 2 * width), lambda b, s: (0, 0)),
                  pl.BlockSpec((1, 2 * width), lambda b, s: (0, 0)),
                  pl.BlockSpec((1, width), lambda b, s: (0, 0))],
        out_specs=pl.BlockSpec((ts, width), lambda b, s: (b * ns + s, 0)),
        out_shape=jax.ShapeDtypeStruct((batch * seq, width), BF16),
        scratch_shapes=[pltpu.VMEM((ts + SUBLANES, width), F32),
                        pltpu.VMEM((1, width), F32)],
        compiler_params=_params(2),
        name="rg_lru",
    )(y, y, conv_w, conv_b.reshape(1, width), w_gates, b_gates.reshape(1, 2 * width),
      lam.reshape(1, width))


def _store_stream_queries(q_ref, qt_scr):
    q_t = q_ref[...].astype(F32).T
    row = lax.broadcasted_iota(jnp.int32, (LANES, q_t.shape[1]), 0)
    for g in range(STREAMS):
        blk = q_t[(g // 2) * LANES:(g // 2 + 1) * LANES, :]
        qt_scr[g] = jnp.where(row // HEAD_DIM == g % 2, blk, 0.0).astype(BF16)


def _score_stage(k_tile, qt_scr, s_scr):
    for g in range(STREAMS):
        s_scr[g] = jnp.dot(k_tile[:, (g // 2) * LANES:(g // 2 + 1) * LANES], qt_scr[g],
                           preferred_element_type=F32)


def _attention_scratch(tq, value_rows):
    return [pltpu.VMEM((STREAMS, LANES, tq), BF16),
            pltpu.VMEM((STREAMS, KV_TILE, tq), F32),
            pltpu.VMEM((STREAMS, KV_TILE, tq), BF16),
            pltpu.VMEM((STREAMS, value_rows, tq), F32)]


def _sb_kernel(q_ref, k_ref, vt_ref, g_ref, o_ref, qt_scr, s_scr, w_scr, acc_scr, *, tq):
    qi = pl.program_id(2)
    _store_stream_queries(q_ref, qt_scr)
    acc_scr[...] = jnp.zeros(acc_scr.shape, F32)

    r = lax.broadcasted_iota(jnp.int32, (KV_TILE, tq), 0)
    c = lax.broadcasted_iota(jnp.int32, (KV_TILE, tq), 1)
    causal = ((r % SUBLANES) * SEGMENT + r // SUBLANES) < c
    srow = lax.broadcasted_iota(jnp.int32, (SUBLANES, tq), 0)

    def tile_index(t):
        return jnp.maximum(qi - t, 0)

    def weight_stage(later, masked):
        new_later = []
        for g in range(STREAMS):
            half_tanh = 0.5 * jnp.tanh(s_scr[g])
            beta = 0.5 + half_tanh
            rest = 0.5 - half_tanh
            if masked:
                beta = jnp.where(causal, beta, 0.0)
                rest = jnp.where(causal, rest, 1.0)
            run = jnp.ones((SUBLANES, tq), F32)
            part = [None] * SEGMENT
            for v in reversed(range(SEGMENT)):
                rows = slice(v * SUBLANES, (v + 1) * SUBLANES)
                part[v] = beta[rows] * run
                run = run * rest[rows]
            off = jnp.broadcast_to(later[g], (SUBLANES, tq))
            for s2 in range(1, SUBLANES):
                off = off * jnp.where(srow < s2, run[s2:s2 + 1, :], 1.0)
            w_scr[g] = jnp.concatenate([p * off for p in part], axis=0).astype(BF16)
            new_later.append((off * run)[0:1, :])
        return tuple(new_later)

    def value_stage(t):
        vt_tile = vt_ref[0, tile_index(t)]
        for g in range(STREAMS):
            acc_scr[g] += jnp.dot(vt_tile[g * HEAD_DIM:(g + 1) * HEAD_DIM, :], w_scr[g],
                                  preferred_element_type=F32)

    _score_stage(k_ref[0, qi], qt_scr, s_scr)
    later = weight_stage(tuple(jnp.ones((1, tq), F32) for _ in range(STREAMS)), True)
    _score_stage(k_ref[0, tile_index(1)], qt_scr, s_scr)

    def step(j, later):
        value_stage(j - 1)
        later = weight_stage(later, False)
        _score_stage(k_ref[0, tile_index(j + 1)], qt_scr, s_scr)
        return later

    lax.fori_loop(1, qi + 1, step, later)
    value_stage(qi)
    out_t = acc_scr[...].reshape(STREAMS * HEAD_DIM, tq)
    o_ref[...] = (out_t.T * _silu(g_ref[...].astype(F32))).astype(BF16)


def _sb_attention(y, k_tiles, vt_tiles, *, batch, seq, q_col, g_col, n_groups):
    nq = seq // Q_TILE
    nkb = seq // KV_TILE
    return pl.pallas_call(
        functools.partial(_sb_kernel, tq=Q_TILE),
        grid=(batch, n_groups, nq),
        in_specs=[pl.BlockSpec((Q_TILE, ATTN_COLS), lambda b, p, i: (b * nq + i, q_col + p)),
                  pl.BlockSpec((1, nkb, KV_TILE, ATTN_COLS), lambda b, p, i: (b, 0, 0, p)),
                  pl.BlockSpec((1, nkb, ATTN_COLS, KV_TILE), lambda b, p, i: (b, 0, p, 0)),
                  pl.BlockSpec((Q_TILE, ATTN_COLS), lambda b, p, i: (b * nq + i, g_col + p))],
        out_specs=pl.BlockSpec((Q_TILE, ATTN_COLS), lambda b, p, i: (b * nq + i, p)),
        out_shape=jax.ShapeDtypeStruct((batch * seq, n_groups * ATTN_COLS), BF16),
        scratch_shapes=_attention_scratch(Q_TILE, HEAD_DIM),
        compiler_params=_params(3),
        name="stick_breaking_attention",
    )(y, k_tiles, vt_tiles, y)


def _diff_kernel(q_ref, k_ref, vt_ref, g_ref, lq1_ref, lk1_ref, lq2_ref, lk2_ref, sg_ref,
                 o_ref, qt_scr, s_scr, p_scr, acc_scr, *, tq, lambda_init):
    qi = pl.program_id(2)
    _store_stream_queries(q_ref, qt_scr)
    acc_scr[...] = jnp.zeros(acc_scr.shape, F32)
    r = lax.broadcasted_iota(jnp.int32, (KV_TILE, tq), 0)
    c = lax.broadcasted_iota(jnp.int32, (KV_TILE, tq), 1)
    causal = r <= c
    vdim = 2 * HEAD_DIM

    def tile_index(t):
        return jnp.maximum(qi - t, 0)

    def key_tile(t):
        start = pl.multiple_of(tile_index(t) * KV_TILE, KV_TILE)
        return k_ref[pl.ds(start, KV_TILE), :]

    def softmax_stage(stats, masked):
        new = []
        for g in range(STREAMS):
            m, l, _ = stats[g]
            s = s_scr[g]
            if masked:
                s = jnp.where(causal, s, -jnp.inf)
            m_new = jnp.maximum(m, jnp.max(s, axis=0, keepdims=True))
            p = jnp.exp(s - m_new)
            alpha = jnp.exp(m - m_new)
            l = alpha * l + jnp.sum(p, axis=0, keepdims=True)
            p_scr[g] = p.astype(BF16)
            new.append((m_new, l, alpha))
        return tuple(new)

    def value_stage(t, stats):
        vt_tile = vt_ref[0, tile_index(t)]
        for g in range(STREAMS):
            hd = g // 2
            acc_scr[g] = stats[g][2] * acc_scr[g] + jnp.dot(
                vt_tile[hd * vdim:(hd + 1) * vdim, :], p_scr[g], preferred_element_type=F32)

    init = tuple((jnp.full((1, tq), -jnp.inf, F32), jnp.zeros((1, tq), F32),
                  jnp.zeros((1, tq), F32)) for _ in range(STREAMS))
    _score_stage(key_tile(0), qt_scr, s_scr)
    stats = softmax_stage(init, True)
    _score_stage(key_tile(1), qt_scr, s_scr)

    def step(j, stats):
        value_stage(j - 1, stats)
        stats = softmax_stage(stats, False)
        _score_stage(key_tile(j + 1), qt_scr, s_scr)
        return stats

    stats = lax.fori_loop(1, qi + 1, step, stats)
    value_stage(qi, stats)

    lam = (jnp.exp(jnp.sum(lq1_ref[...] * lk1_ref[...], axis=-1, keepdims=True))
           - jnp.exp(jnp.sum(lq2_ref[...] * lk2_ref[...], axis=-1, keepdims=True))
           + lambda_init)
    for hd in range(STREAMS // 2):
        l1, l2 = stats[2 * hd][1], stats[2 * hd + 1][1]
        out_t = acc_scr[2 * hd] / l1 - lam * (acc_scr[2 * hd + 1] / l2)
        ms = jnp.mean(out_t * out_t, axis=0, keepdims=True)
        out = (out_t * lax.rsqrt(ms + EPS)).T
        out = out * sg_ref[...] * (1.0 - lambda_init)
        cols = slice(hd * vdim, (hd + 1) * vdim)
        o_ref[:, cols] = (out * _silu(g_ref[:, cols].astype(F32))).astype(BF16)


def _diff_attention(y, vt_tiles, lq1, lk1, lq2, lk2, subln_g, *, batch, seq, n_groups,
                    k_col, g_col, lambda_init):
    nq = seq // Q_TILE
    nkb = seq // KV_TILE
    vec = lambda a: a.reshape(1, -1)
    small = lambda n: pl.BlockSpec((1, n), lambda b, h, i: (0, 0))
    return pl.pallas_call(
        functools.partial(_diff_kernel, tq=Q_TILE, lambda_init=lambda_init),
        grid=(batch, n_groups, nq),
        in_specs=[pl.BlockSpec((Q_TILE, ATTN_COLS), lambda b, h, i: (b * nq + i, h)),
                  pl.BlockSpec((seq, ATTN_COLS), lambda b, h, i: (b, k_col + h)),
                  pl.BlockSpec((1, nkb, ATTN_COLS, KV_TILE), lambda b, h, i: (b, 0, h, 0)),
                  pl.BlockSpec((Q_TILE, ATTN_COLS), lambda b, h, i: (b * nq + i, g_col + h)),
                  small(HEAD_DIM), small(HEAD_DIM), small(HEAD_DIM), small(HEAD_DIM),
                  small(2 * HEAD_DIM)],
        out_specs=pl.BlockSpec((Q_TILE, ATTN_COLS), lambda b, h, i: (b * nq + i, h)),
        out_shape=jax.ShapeDtypeStruct((batch * seq, n_groups * ATTN_COLS), BF16),
        scratch_shapes=_attention_scratch(Q_TILE, 2 * HEAD_DIM),
        compiler_params=_params(3),
        name="differential_attention",
    )(y, y, vt_tiles, y, vec(lq1), vec(lk1), vec(lq2), vec(lk2), vec(subln_g))


def _post_kernel(*refs, final):
    if final:
        (h_ref, ma_ref, mb_ref, p_ref, woa_ref, wob_ref, gp_ref, wg_ref, wp_ref,
         fn_ref, o_ref) = refs
    else:
        h_ref, ma_ref, mb_ref, p_ref, woa_ref, wob_ref, gp_ref, wg_ref, wp_ref, o_ref = refs
    h = (h_ref[...]
         + jnp.dot(ma_ref[...], woa_ref[...], preferred_element_type=F32)
         + jnp.dot(mb_ref[...], wob_ref[...], preferred_element_type=F32))
    ms = jnp.mean(h * h, axis=-1, keepdims=True)
    hn = (h * lax.rsqrt(ms + EPS) * gp_ref[...]).astype(BF16)
    gate = _sigmoid(jnp.dot(hn, wg_ref[...], preferred_element_type=F32))
    ple = jnp.dot(p_ref[...].astype(BF16), wp_ref[...], preferred_element_type=F32)
    h = h + gate * ple
    if final:
        ms = jnp.mean(h * h, axis=-1, keepdims=True)
        h = h * lax.rsqrt(ms + EPS) * fn_ref[...]
    o_ref[...] = h


def _post(h, mix_a, mix_b, b_col, p, w_out, g_ple, w_gate, w_proj, final_norm=None, *, tm=256):
    t, d = h.shape
    half = w_out.shape[0] // 2
    pd = p.shape[1]
    final = final_norm is not None
    const = lambda shape: pl.BlockSpec(shape, lambda i: (0, 0))
    in_specs = [pl.BlockSpec((tm, d), lambda i: (i, 0)),
                pl.BlockSpec((tm, half), lambda i: (i, 0)),
                pl.BlockSpec((tm, half), lambda i: (i, b_col)),
                pl.BlockSpec((tm, pd), lambda i: (i, 0)),
                pl.BlockSpec((half, d), lambda i: (0, 0)),
                pl.BlockSpec((half, d), lambda i: (1, 0)),
                const((1, d)), const((d, d)), const((pd, d))]
    args = [h, mix_a, mix_b, p, w_out, w_out, g_ple.reshape(1, d), w_gate, w_proj]
    if final:
        in_specs.append(const((1, d)))
        args.append(final_norm.reshape(1, d))
    return pl.pallas_call(
        functools.partial(_post_kernel, final=final),
        grid=(t // tm,),
        in_specs=in_specs,
        out_specs=pl.BlockSpec((tm, d), lambda i: (i, 0)),
        out_shape=jax.ShapeDtypeStruct((t, d), F32),
        compiler_params=_params(1),
        name="out_proj_ple_final" if final else "out_proj_ple",
    )(*args)


def _key_tiles(x, batch, seq, permute):
    cols = x.shape[1]
    nkb = seq // KV_TILE
    if permute:
        x = x.reshape(batch, nkb, SUBLANES, SEGMENT, cols)
        x = x.transpose(0, 1, 3, 2, 4)
    return x.reshape(batch, nkb, KV_TILE, cols)


def _block_diag(w):
    nb, bs, _ = w.shape
    eye = jnp.eye(nb, dtype=w.dtype)
    return jnp.einsum('ncd,nm->ncmd', w, eye).reshape(nb * bs, nb * bs)


def kernel(x, p, positions, norm_mix, norm_ple, w_ple_gate, w_ple_proj, w_in_e, conv_w, conv_b,
           lru_wa, lru_ba, lru_wx, lru_bx, lru_lambda, w_out_e, w_in_o, lam_q1, lam_k1, lam_q2,
           lam_k2, subln_g, w_out_o, final_norm):
    batch, seq, d = x.shape
    depth = norm_mix.shape[0]
    t = batch * seq
    lru_w = conv_w.shape[-1]
    sb_w = (w_in_e.shape[-1] - 2 * lru_w) // 4
    diff_w = w_in_o.shape[-1] // 4

    half = HEAD_DIM // 2
    inv_freq = ROPE_THETA ** (-jnp.arange(0, HEAD_DIM, 2, dtype=F32) / HEAD_DIM)
    ang = positions.astype(F32).reshape(t, 1) * inv_freq
    cos = jnp.tile(jnp.cos(ang), (1, LANES // half))
    sin = jnp.tile(jnp.concatenate([-jnp.sin(ang), jnp.sin(ang)], axis=1), (1, LANES // HEAD_DIM))

    h = x.reshape(t, d)
    for i in range(depth):
        j = i // 2
        last = i == depth - 1
        if i % 2 == 0:
            q0 = 2 * lru_w
            y = _in_proj(h, norm_mix[i], w_in_e[j].astype(BF16), scale_range=(q0, q0 + sb_w),
                         scale=0.5 * HEAD_DIM ** -0.5)
            w_gates = jnp.concatenate([_block_diag(lru_wa[j]), _block_diag(lru_wx[j])], axis=1)
            b_gates = jnp.concatenate([lru_ba[j], lru_bx[j]])
            oa = _lru(y, conv_w[j], conv_b[j], w_gates.astype(BF16), b_gates, lru_lambda[j],
                      batch=batch, seq=seq, width=lru_w)
            k_tiles = _key_tiles(y[:, q0 + sb_w:q0 + 2 * sb_w], batch, seq, True)
            v_tiles = _key_tiles(y[:, q0 + 2 * sb_w:q0 + 3 * sb_w], batch, seq, True)
            vt_tiles = v_tiles.transpose(0, 1, 3, 2)
            ob = _sb_attention(y, k_tiles, vt_tiles, batch=batch, seq=seq,
                               q_col=q0 // ATTN_COLS, g_col=(q0 + 3 * sb_w) // ATTN_COLS,
                               n_groups=sb_w // ATTN_COLS)
            mix_a, mix_b, b_col, w_out = oa, ob, 0, w_out_e[j]
        else:
            lambda_init = 0.8 - 0.6 * math.exp(-0.3 * i)
            y = _in_proj(h, norm_mix[i], w_in_o[j].astype(BF16), cos, sin,
                         rope_cols=2 * diff_w, scale_range=(0, diff_w), scale=HEAD_DIM ** -0.5)
            vt_tiles = _key_tiles(y[:, 2 * diff_w:3 * diff_w], batch, seq,
                                  False).transpose(0, 1, 3, 2)
            o = _diff_attention(y, vt_tiles, lam_q1[j], lam_k1[j], lam_q2[j], lam_k2[j],
                                subln_g[j], batch=batch, seq=seq,
                                n_groups=diff_w // ATTN_COLS, k_col=diff_w // ATTN_COLS,
                                g_col=3 * diff_w // ATTN_COLS, lambda_init=lambda_init)
            mix_a, mix_b, b_col, w_out = o, o, 1, w_out_o[j]
        h = _post(h, mix_a, mix_b, b_col, p[i].reshape(t, -1), w_out.astype(BF16), norm_ple[i],
                  w_ple_gate[i].astype(BF16), w_ple_proj[i].astype(BF16),
                  final_norm if last else None)
    return h.reshape(batch, seq, d)
```

```python
import functools
import math

import jax
import jax.numpy as jnp
from jax import lax
from jax.experimental import pallas as pl
from jax.experimental.pallas import tpu as pltpu

F32 = jnp.float32
BF16 = jnp.bfloat16

EPS = 1e-6
HEAD_DIM = 64
LRU_C = 8.0
CONV_WIDTH = 4
ROPE_THETA = 10000.0

LANES = 128
SUBLANES = 8
KV_TILE = 256
Q_TILE = 256
SEGMENT = KV_TILE // SUBLANES
SB_STREAMS = 4
DIFF_STREAMS = 8
VMEM_LIMIT = 48 * 1024 * 1024


def _params(n_axes):
    return pltpu.CompilerParams(
        dimension_semantics=("arbitrary",) * n_axes, vmem_limit_bytes=VMEM_LIMIT)


def _sigmoid(x):
    return 1.0 / (1.0 + jnp.exp(-x))


def _silu(x):
    return x * _sigmoid(x)


def _in_proj_kernel(*refs, n_out, chunk, rope_cols, scale_range, scale):
    if rope_cols:
        x_ref, g_ref, w_ref, cos_ref, sin_ref, o_ref = refs
    else:
        x_ref, g_ref, w_ref, o_ref = refs
    x = x_ref[...]
    ms = jnp.mean(x * x, axis=-1, keepdims=True)
    xn = (x * lax.rsqrt(ms + EPS) * g_ref[...]).astype(BF16)
    if rope_cols:
        reps = chunk // LANES
        cos = jnp.concatenate([cos_ref[...]] * reps, axis=1)
        sin = jnp.concatenate([sin_ref[...]] * reps, axis=1)
        lane = lax.broadcasted_iota(jnp.int32, (x.shape[0], chunk), 1)
        first_half = (lane % HEAD_DIM) < (HEAD_DIM // 2)
    for c in range(n_out // chunk):
        lo = c * chunk
        y = jnp.dot(xn, w_ref[:, lo:lo + chunk], preferred_element_type=F32)
        if lo < rope_cols:
            partner = jnp.where(first_half,
                                pltpu.roll(y, chunk - HEAD_DIM // 2, 1),
                                pltpu.roll(y, HEAD_DIM // 2, 1))
            y = y * cos + partner * sin
        if scale_range[0] <= lo < scale_range[1]:
            y = y * scale
        o_ref[:, lo:lo + chunk] = y.astype(BF16)


def _in_proj(h, g, w, cos=None, sin=None, *, rope_cols=0, scale_range, scale, tm=512,
             chunk=512):
    t, d = h.shape
    n_out = w.shape[1]
    in_specs = [pl.BlockSpec((tm, d), lambda i: (i, 0)),
                pl.BlockSpec((1, d), lambda i: (0, 0)),
                pl.BlockSpec((d, n_out), lambda i: (0, 0))]
    args = [h, g.reshape(1, d), w]
    if rope_cols:
        in_specs += [pl.BlockSpec((tm, LANES), lambda i: (i, 0))] * 2
        args += [cos, sin]
    return pl.pallas_call(
        functools.partial(_in_proj_kernel, n_out=n_out, chunk=chunk,
                          rope_cols=rope_cols, scale_range=scale_range, scale=scale),
        grid=(t // tm,),
        in_specs=in_specs,
        out_specs=pl.BlockSpec((tm, n_out), lambda i: (i, 0)),
        out_shape=jax.ShapeDtypeStruct((t, n_out), BF16),
        compiler_params=_params(1),
        name="in_proj_rope" if rope_cols else "in_proj",
    )(*args)


def _lru_kernel(xa_ref, ga_ref, cw_ref, cb_ref, wg_ref, bg_ref, lam_ref, o_ref,
                ext_ref, h_ref, *, ts, width):
    @pl.when(pl.program_id(1) == 0)
    def _():
        ext_ref[0:SUBLANES, :] = jnp.zeros((SUBLANES, width), F32)
        h_ref[...] = jnp.zeros((1, width), F32)

    xa = xa_ref[...].astype(F32)
    ext_ref[SUBLANES:SUBLANES + ts, :] = xa
    xc = cb_ref[...] + cw_ref[CONV_WIDTH - 1:CONV_WIDTH, :] * xa
    for j in range(1, CONV_WIDTH):
        k = CONV_WIDTH - 1 - j
        xc = xc + cw_ref[k:k + 1, :] * ext_ref[SUBLANES - j:SUBLANES - j + ts, :]
    ext_ref[0:SUBLANES, :] = ext_ref[ts:ts + SUBLANES, :]

    gates = jnp.dot(xc.astype(BF16), wg_ref[...], preferred_element_type=F32) + bg_ref[...]
    r = _sigmoid(gates[:, :width])
    i = _sigmoid(gates[:, width:])
    lam = lam_ref[...]
    log_sig_lam = jnp.minimum(lam, 0.0) - jnp.log(1.0 + jnp.exp(-jnp.abs(lam)))
    log_a = LRU_C * r * log_sig_lam
    a = jnp.exp(log_a)
    u = jnp.sqrt(1.0 - jnp.exp(2.0 * log_a)) * (i * xc)

    row = lax.broadcasted_iota(jnp.int32, (ts, width), 0)
    step = 1
    while step < ts:
        valid = row >= step
        a_prev = pltpu.roll(a, step, 0)
        u_prev = pltpu.roll(u, step, 0)
        u = jnp.where(valid, a * u_prev + u, u)
        a = jnp.where(valid, a * a_prev, a)
        step *= 2
    h = u + a * h_ref[...]
    h_ref[...] = h[ts - 1:ts, :]
    o_ref[...] = (h * _silu(ga_ref[...].astype(F32))).astype(BF16)


def _lru(y, conv_w, conv_b, w_gates, b_gates, lam, *, batch, seq, width, ts=512):
    ns = seq // ts
    return pl.pallas_call(
        functools.partial(_lru_kernel, ts=ts, width=width),
        grid=(batch, ns),
        in_specs=[pl.BlockSpec((ts, width), lambda b, s: (b * ns + s, 0)),
                  pl.BlockSpec((ts, width), lambda b, s: (b * ns + s, 1)),
                  pl.BlockSpec((CONV_WIDTH, width), lambda b, s: (0, 0)),
                  pl.BlockSpec((1, width), lambda b, s: (0, 0)),
                  pl.BlockSpec((width, 2 * width), lambda b, s: (0, 0)),
                  pl.BlockSpec((1, 2 * width), lambda b, s: (0, 0)),
                  pl.BlockSpec((1, width), lambda b, s: (0, 0))],
        out_specs=pl.BlockSpec((ts, width), lambda b, s: (b * ns + s, 0)),
        out_shape=jax.ShapeDtypeStruct((batch * seq, width), BF16),
        scratch_shapes=[pltpu.VMEM((ts + SUBLANES, width), F32),
                        pltpu.VMEM((1, width), F32)],
        compiler_params=_params(2),
        name="rg_lru",
    )(y, y, conv_w, conv_b.reshape(1, width), w_gates, b_gates.reshape(1, 2 * width),
      lam.reshape(1, width))


def _store_stream_queries(q_ref, qt_scr):
    tq = q_ref.shape[0]
    row = lax.broadcasted_iota(jnp.int32, (LANES, tq), 0)
    for blk in range(qt_scr.shape[0] // 2):
        q_t = q_ref[:, blk * LANES:(blk + 1) * LANES].astype(F32).T
        for half in range(2):
            qt_scr[2 * blk + half] = jnp.where(row // HEAD_DIM == half, q_t, 0.0).astype(BF16)


def _score_stage(k_tile, qt_scr, s_scr):
    for g in range(qt_scr.shape[0]):
        s_scr[g] = jnp.dot(k_tile[:, (g // 2) * LANES:(g // 2 + 1) * LANES], qt_scr[g],
                           preferred_element_type=F32)


def _attention_scratch(streams, tq, value_rows):
    return [pltpu.VMEM((streams, LANES, tq), BF16),
            pltpu.VMEM((streams, KV_TILE, tq), F32),
            pltpu.VMEM((streams, KV_TILE, tq), BF16),
            pltpu.VMEM((streams, value_rows, tq), F32)]


def _sb_kernel(q_ref, k_ref, vt_ref, g_ref, o_ref, qt_scr, s_scr, w_scr, acc_scr, *, tq):
    qi = pl.program_id(2)
    streams = qt_scr.shape[0]
    _store_stream_queries(q_ref, qt_scr)
    acc_scr[...] = jnp.zeros(acc_scr.shape, F32)

    r = lax.broadcasted_iota(jnp.int32, (KV_TILE, tq), 0)
    c = lax.broadcasted_iota(jnp.int32, (KV_TILE, tq), 1)
    causal = ((r % SUBLANES) * SEGMENT + r // SUBLANES) < c
    srow = lax.broadcasted_iota(jnp.int32, (SUBLANES, tq), 0)

    def tile_index(t):
        return jnp.maximum(qi - t, 0)

    def weight_stage(later, masked):
        new_later = []
        for g in range(streams):
            run = jnp.ones((SUBLANES, tq), F32)
            for v in reversed(range(SEGMENT)):
                rows = slice(v * SUBLANES, (v + 1) * SUBLANES)
                half_tanh = 0.5 * jnp.tanh(s_scr[g, rows, :])
                beta = 0.5 + half_tanh
                rest = 0.5 - half_tanh
                if masked:
                    beta = jnp.where(causal[rows], beta, 0.0)
                    rest = jnp.where(causal[rows], rest, 1.0)
                s_scr[g, rows, :] = beta * run
                run = run * rest
            off = jnp.broadcast_to(later[g], (SUBLANES, tq))
            for s2 in range(1, SUBLANES):
                off = off * jnp.where(srow < s2, run[s2:s2 + 1, :], 1.0)
            off2 = jnp.concatenate([off, off], axis=0)
            for v in range(SEGMENT // 2):
                rows = slice(2 * v * SUBLANES, 2 * (v + 1) * SUBLANES)
                w_scr[g, rows, :] = (s_scr[g, rows, :] * off2).astype(BF16)
            new_later.append((off * run)[0:1, :])
        return tuple(new_later)

    def value_stage(t):
        vt_tile = vt_ref[0, tile_index(t)]
        for g in range(streams):
            acc_scr[g] += jnp.dot(vt_tile[g * HEAD_DIM:(g + 1) * HEAD_DIM, :], w_scr[g],
                                  preferred_element_type=F32)

    _score_stage(k_ref[0, qi], qt_scr, s_scr)
    later = weight_stage(tuple(jnp.ones((1, tq), F32) for _ in range(streams)), True)
    _score_stage(k_ref[0, tile_index(1)], qt_scr, s_scr)

    def step(j, later):
        value_stage(j - 1)
        later = weight_stage(later, False)
        _score_stage(k_ref[0, tile_index(j + 1)], qt_scr, s_scr)
        return later

    lax.fori_loop(1, qi + 1, step, later)
    value_stage(qi)
    out_t = acc_scr[...].reshape(streams * HEAD_DIM, tq)
    o_ref[...] = (out_t.T * _silu(g_ref[...].astype(F32))).astype(BF16)


def _sb_attention(y, k_tiles, vt_tiles, *, batch, seq, q0, g0, width):
    nq = seq // Q_TILE
    nkb = seq // KV_TILE
    cols = SB_STREAMS * HEAD_DIM
    return pl.pallas_call(
        functools.partial(_sb_kernel, tq=Q_TILE),
        grid=(batch, width // cols, nq),
        in_specs=[pl.BlockSpec((Q_TILE, cols), lambda b, p, i: (b * nq + i, q0 // cols + p)),
                  pl.BlockSpec((1, nkb, KV_TILE, cols), lambda b, p, i: (b, 0, 0, p)),
                  pl.BlockSpec((1, nkb, cols, KV_TILE), lambda b, p, i: (b, 0, p, 0)),
                  pl.BlockSpec((Q_TILE, cols), lambda b, p, i: (b * nq + i, g0 // cols + p))],
        out_specs=pl.BlockSpec((Q_TILE, cols), lambda b, p, i: (b * nq + i, p)),
        out_shape=jax.ShapeDtypeStruct((batch * seq, width), BF16),
        scratch_shapes=_attention_scratch(SB_STREAMS, Q_TILE, HEAD_DIM),
        compiler_params=_params(3),
        name="stick_breaking_attention",
    )(y, k_tiles, vt_tiles, y)


def _diff_kernel(q_ref, k_ref, vt_ref, g_ref, lq1_ref, lk1_ref, lq2_ref, lk2_ref, sg_ref,
                 o_ref, qt_scr, s_scr, p_scr, acc_scr, *, tq, lambda_init):
    qi = pl.program_id(2)
    streams = qt_scr.shape[0]
    _store_stream_queries(q_ref, qt_scr)
    acc_scr[...] = jnp.zeros(acc_scr.shape, F32)
    r = lax.broadcasted_iota(jnp.int32, (KV_TILE, tq), 0)
    c = lax.broadcasted_iota(jnp.int32, (KV_TILE, tq), 1)
    causal = r <= c
    vdim = 2 * HEAD_DIM

    def tile_index(t):
        return jnp.maximum(qi - t, 0)

    def key_tile(t):
        start = pl.multiple_of(tile_index(t) * KV_TILE, KV_TILE)
        return k_ref[pl.ds(start, KV_TILE), :]

    def softmax_stage(stats, masked):
        new = []
        for g in range(streams):
            m, l, _ = stats[g]
            s = s_scr[g]
            if masked:
                s = jnp.where(causal, s, -jnp.inf)
            m_new = jnp.maximum(m, jnp.max(s, axis=0, keepdims=True))
            p = jnp.exp(s - m_new)
            alpha = jnp.exp(m - m_new)
            l = alpha * l + jnp.sum(p, axis=0, keepdims=True)
            p_scr[g] = p.astype(BF16)
            new.append((m_new, l, alpha))
        return tuple(new)

    def value_stage(t, stats):
        vt_tile = vt_ref[0, tile_index(t)]
        for g in range(streams):
            hd = g // 2
            acc_scr[g] = stats[g][2] * acc_scr[g] + jnp.dot(
                vt_tile[hd * vdim:(hd + 1) * vdim, :], p_scr[g], preferred_element_type=F32)

    init = tuple((jnp.full((1, tq), -jnp.inf, F32), jnp.zeros((1, tq), F32),
                  jnp.zeros((1, tq), F32)) for _ in range(streams))
    _score_stage(key_tile(0), qt_scr, s_scr)
    stats = softmax_stage(init, True)
    _score_stage(key_tile(1), qt_scr, s_scr)

    def step(j, stats):
        value_stage(j - 1, stats)
        stats = softmax_stage(stats, False)
        _score_stage(key_tile(j + 1), qt_scr, s_scr)
        return stats

    stats = lax.fori_loop(1, qi + 1, step, stats)
    value_stage(qi, stats)

    lam = (jnp.exp(jnp.sum(lq1_ref[...] * lk1_ref[...], axis=-1, keepdims=True))
           - jnp.exp(jnp.sum(lq2_ref[...] * lk2_ref[...], axis=-1, keepdims=True))
           + lambda_init)
    for hd in range(streams // 2):
        l1, l2 = stats[2 * hd][1], stats[2 * hd + 1][1]
        out_t = acc_scr[2 * hd] / l1 - lam * (acc_scr[2 * hd + 1] / l2)
        ms = jnp.mean(out_t * out_t, axis=0, keepdims=True)
        out = (out_t * lax.rsqrt(ms + EPS)).T
        out = out * sg_ref[...] * (1.0 - lambda_init)
        cols = slice(hd * vdim, (hd + 1) * vdim)
        o_ref[:, cols] = (out * _silu(g_ref[:, cols].astype(F32))).astype(BF16)


def _diff_attention(y, vt_tiles, lq1, lk1, lq2, lk2, subln_g, *, batch, seq, width,
                    lambda_init):
    nq = seq // Q_TILE
    nkb = seq // KV_TILE
    cols = DIFF_STREAMS * HEAD_DIM
    k_col, g_col = width // cols, 3 * width // cols
    vec = lambda a: a.reshape(1, -1)
    small = lambda n: pl.BlockSpec((1, n), lambda b, h, i: (0, 0))
    return pl.pallas_call(
        functools.partial(_diff_kernel, tq=Q_TILE, lambda_init=lambda_init),
        grid=(batch, width // cols, nq),
        in_specs=[pl.BlockSpec((Q_TILE, cols), lambda b, h, i: (b * nq + i, h)),
                  pl.BlockSpec((seq, cols), lambda b, h, i: (b, k_col + h)),
                  pl.BlockSpec((1, nkb, cols, KV_TILE), lambda b, h, i: (b, 0, h, 0)),
                  pl.BlockSpec((Q_TILE, cols), lambda b, h, i: (b * nq + i, g_col + h)),
                  small(HEAD_DIM), small(HEAD_DIM), small(HEAD_DIM), small(HEAD_DIM),
                  small(2 * HEAD_DIM)],
        out_specs=pl.BlockSpec((Q_TILE, cols), lambda b, h, i: (b * nq + i, h)),
        out_shape=jax.ShapeDtypeStruct((batch * seq, width), BF16),
        scratch_shapes=_attention_scratch(DIFF_STREAMS, Q_TILE, 2 * HEAD_DIM),
        compiler_params=_params(3),
        name="differential_attention",
    )(y, y, vt_tiles, y, vec(lq1), vec(lk1), vec(lq2), vec(lk2), vec(subln_g))


def _post_kernel(*refs, final):
    if final:
        (h_ref, ma_ref, mb_ref, p_ref, woa_ref, wob_ref, gp_ref, wg_ref, wp_ref,
         fn_ref, o_ref) = refs
    else:
        h_ref, ma_ref, mb_ref, p_ref, woa_ref, wob_ref, gp_ref, wg_ref, wp_ref, o_ref = refs
    h = (h_ref[...]
         + jnp.dot(ma_ref[...], woa_ref[...], preferred_element_type=F32)
         + jnp.dot(mb_ref[...], wob_ref[...], preferred_element_type=F32))
    ms = jnp.mean(h * h, axis=-1, keepdims=True)
    hn = (h * lax.rsqrt(ms + EPS) * gp_ref[...]).astype(BF16)
    gate = _sigmoid(jnp.dot(hn, wg_ref[...], preferred_element_type=F32))
    ple = jnp.dot(p_ref[...].astype(BF16), wp_ref[...], preferred_element_type=F32)
    h = h + gate * ple
    if final:
        ms = jnp.mean(h * h, axis=-1, keepdims=True)
        h = h * lax.rsqrt(ms + EPS) * fn_ref[...]
    o_ref[...] = h


def _post(h, mix_a, mix_b, b_col, p, w_out, g_ple, w_gate, w_proj, final_norm=None, *, tm=512):
    t, d = h.shape
    half = w_out.shape[0] // 2
    pd = p.shape[1]
    final = final_norm is not None
    const = lambda shape: pl.BlockSpec(shape, lambda i: (0, 0))
    in_specs = [pl.BlockSpec((tm, d), lambda i: (i, 0)),
                pl.BlockSpec((tm, half), lambda i: (i, 0)),
                pl.BlockSpec((tm, half), lambda i: (i, b_col)),
                pl.BlockSpec((tm, pd), lambda i: (i, 0)),
                pl.BlockSpec((half, d), lambda i: (0, 0)),
                pl.BlockSpec((half, d), lambda i: (1, 0)),
                const((1, d)), const((d, d)), const((pd, d))]
    args = [h, mix_a, mix_b, p, w_out, w_out, g_ple.reshape(1, d), w_gate, w_proj]
    if final:
        in_specs.append(const((1, d)))
        args.append(final_norm.reshape(1, d))
    return pl.pallas_call(
        functools.partial(_post_kernel, final=final),
        grid=(t // tm,),
        in_specs=in_specs,
        out_specs=pl.BlockSpec((tm, d), lambda i: (i, 0)),
        out_shape=jax.ShapeDtypeStruct((t, d), F32),
        compiler_params=_params(1),
        name="out_proj_ple_final" if final else "out_proj_ple",
    )(*args)


def _key_tiles(x, batch, seq, permute):
    cols = x.shape[1]
    nkb = seq // KV_TILE
    if permute:
        x = x.reshape(batch, nkb, SUBLANES, SEGMENT, cols)
        x = x.transpose(0, 1, 3, 2, 4)
    return x.reshape(batch, nkb, KV_TILE, cols)


def _block_diag(w):
    nb, bs, _ = w.shape
    eye = jnp.eye(nb, dtype=w.dtype)
    return jnp.einsum('ncd,nm->ncmd', w, eye).reshape(nb * bs, nb * bs)


def kernel(x, p, positions, norm_mix, norm_ple, w_ple_gate, w_ple_proj, w_in_e, conv_w, conv_b,
           lru_wa, lru_ba, lru_wx, lru_bx, lru_lambda, w_out_e, w_in_o, lam_q1, lam_k1, lam_q2,
           lam_k2, subln_g, w_out_o, final_norm):
    batch, seq, d = x.shape
    depth = norm_mix.shape[0]
    t = batch * seq
    lru_w = conv_w.shape[-1]
    sb_w = (w_in_e.shape[-1] - 2 * lru_w) // 4
    diff_w = w_in_o.shape[-1] // 4

    half = HEAD_DIM // 2
    inv_freq = ROPE_THETA ** (-jnp.arange(0, HEAD_DIM, 2, dtype=F32) / HEAD_DIM)
    ang = positions.astype(F32).reshape(t, 1) * inv_freq
    cos = jnp.tile(jnp.cos(ang), (1, LANES // half))
    sin = jnp.tile(jnp.concatenate([-jnp.sin(ang), jnp.sin(ang)], axis=1), (1, LANES // HEAD_DIM))

    h = x.reshape(t, d)
    for i in range(depth):
        j = i // 2
        last = i == depth - 1
        if i % 2 == 0:
            q0 = 2 * lru_w
            y = _in_proj(h, norm_mix[i], w_in_e[j].astype(BF16), scale_range=(q0, q0 + sb_w),
                         scale=0.5 * HEAD_DIM ** -0.5)
            w_gates = jnp.concatenate([_block_diag(lru_wa[j]), _block_diag(lru_wx[j])], axis=1)
            b_gates = jnp.concatenate([lru_ba[j], lru_bx[j]])
            oa = _lru(y, conv_w[j], conv_b[j], w_gates.astype(BF16), b_gates, lru_lambda[j],
                      batch=batch, seq=seq, width=lru_w)
            k_tiles = _key_tiles(y[:, q0 + sb_w:q0 + 2 * sb_w], batch, seq, True)
            v_tiles = _key_tiles(y[:, q0 + 2 * sb_w:q0 + 3 * sb_w], batch, seq, True)
            vt_tiles = v_tiles.transpose(0, 1, 3, 2)
            ob = _sb_attention(y, k_tiles, vt_tiles, batch=batch, seq=seq,
                               q0=q0, g0=q0 + 3 * sb_w, width=sb_w)
            mix_a, mix_b, b_col, w_out = oa, ob, 0, w_out_e[j]
        else:
            lambda_init = 0.8 - 0.6 * math.exp(-0.3 * i)
            y = _in_proj(h, norm_mix[i], w_in_o[j].astype(BF16), cos, sin,
                         rope_cols=2 * diff_w, scale_range=(0, diff_w), scale=HEAD_DIM ** -0.5)
            vt_tiles = _key_tiles(y[:, 2 * diff_w:3 * diff_w], batch, seq,
                                  False).transpose(0, 1, 3, 2)
            o = _diff_attention(y, vt_tiles, lam_q1[j], lam_k1[j], lam_q2[j], lam_k2[j],
                                subln_g[j], batch=batch, seq=seq, width=diff_w,
                                lambda_init=lambda_init)
            mix_a, mix_b, b_col, w_out = o, o, 1, w_out_o[j]
        h = _post(h, mix_a, mix_b, b_col, p[i].reshape(t, -1), w_out.astype(BF16), norm_ple[i],
                  w_ple_gate[i].astype(BF16), w_ple_proj[i].astype(BF16),
                  final_norm if last else None)
    return h.reshape(batch, seq, d)
```

```python
import functools
import math

import jax
import jax.numpy as jnp
from jax import lax
from jax.experimental import pallas as pl
from jax.experimental.pallas import tpu as pltpu

F32 = jnp.float32
BF16 = jnp.bfloat16

EPS = 1e-6
HEAD_DIM = 64
LRU_C = 8.0
CONV_WIDTH = 4
ROPE_THETA = 10000.0

LANES = 128
SUBLANES = 8
KV_TILE = 256
Q_TILE = 256
SEGMENT = KV_TILE // SUBLANES
SB_STREAMS = 8
DIFF_STREAMS = 8
VMEM_LIMIT = 48 * 1024 * 1024


def _params(n_axes):
    return pltpu.CompilerParams(
        dimension_semantics=("arbitrary",) * n_axes, vmem_limit_bytes=VMEM_LIMIT)


def _sigmoid(x):
    return 1.0 / (1.0 + jnp.exp(-x))


def _silu(x):
    return x * _sigmoid(x)


def _in_proj_kernel(*refs, n_out, chunk, rope_cols, scale_range, scale):
    if rope_cols:
        x_ref, g_ref, w_ref, cos_ref, sin_ref, o_ref = refs
    else:
        x_ref, g_ref, w_ref, o_ref = refs
    x = x_ref[...]
    ms = jnp.mean(x * x, axis=-1, keepdims=True)
    xn = (x * lax.rsqrt(ms + EPS) * g_ref[...]).astype(BF16)
    if rope_cols:
        reps = chunk // LANES
        cos = jnp.concatenate([cos_ref[...]] * reps, axis=1)
        sin = jnp.concatenate([sin_ref[...]] * reps, axis=1)
        lane = lax.broadcasted_iota(jnp.int32, (x.shape[0], chunk), 1)
        first_half = (lane % HEAD_DIM) < (HEAD_DIM // 2)
    for c in range(n_out // chunk):
        lo = c * chunk
        y = jnp.dot(xn, w_ref[:, lo:lo + chunk], preferred_element_type=F32)
        if lo < rope_cols:
            partner = jnp.where(first_half,
                                pltpu.roll(y, chunk - HEAD_DIM // 2, 1),
                                pltpu.roll(y, HEAD_DIM // 2, 1))
            y = y * cos + partner * sin
        if scale_range[0] <= lo < scale_range[1]:
            y = y * scale
        o_ref[:, lo:lo + chunk] = y.astype(BF16)


def _in_proj(h, g, w, cos=None, sin=None, *, rope_cols=0, scale_range, scale, tm=512,
             chunk=512):
    t, d = h.shape
    n_out = w.shape[1]
    in_specs = [pl.BlockSpec((tm, d), lambda i: (i, 0)),
                pl.BlockSpec((1, d), lambda i: (0, 0)),
                pl.BlockSpec((d, n_out), lambda i: (0, 0))]
    args = [h, g.reshape(1, d), w]
    if rope_cols:
        in_specs += [pl.BlockSpec((tm, LANES), lambda i: (i, 0))] * 2
        args += [cos, sin]
    return pl.pallas_call(
        functools.partial(_in_proj_kernel, n_out=n_out, chunk=chunk,
                          rope_cols=rope_cols, scale_range=scale_range, scale=scale),
        grid=(t // tm,),
        in_specs=in_specs,
        out_specs=pl.BlockSpec((tm, n_out), lambda i: (i, 0)),
        out_shape=jax.ShapeDtypeStruct((t, n_out), BF16),
        compiler_params=_params(1),
        name="in_proj_rope" if rope_cols else "in_proj",
    )(*args)


def _lru_kernel(xa_ref, ga_ref, cw_ref, cb_ref, wg_ref, bg_ref, lam_ref, o_ref,
                ext_ref, h_ref, *, ts, width):
    @pl.when(pl.program_id(1) == 0)
    def _():
        ext_ref[0:SUBLANES, :] = jnp.zeros((SUBLANES, width), F32)
        h_ref[...] = jnp.zeros((1, width), F32)

    xa = xa_ref[...].astype(F32)
    ext_ref[SUBLANES:SUBLANES + ts, :] = xa
    xc = cb_ref[...] + cw_ref[CONV_WIDTH - 1:CONV_WIDTH, :] * xa
    for j in range(1, CONV_WIDTH):
        k = CONV_WIDTH - 1 - j
        xc = xc + cw_ref[k:k + 1, :] * ext_ref[SUBLANES - j:SUBLANES - j + ts, :]
    ext_ref[0:SUBLANES, :] = ext_ref[ts:ts + SUBLANES, :]

    gates = jnp.dot(xc.astype(BF16), wg_ref[...], preferred_element_type=F32) + bg_ref[...]
    r = _sigmoid(gates[:, :width])
    i = _sigmoid(gates[:, width:])
    lam = lam_ref[...]
    log_sig_lam = jnp.minimum(lam, 0.0) - jnp.log(1.0 + jnp.exp(-jnp.abs(lam)))
    log_a = LRU_C * r * log_sig_lam
    a = jnp.exp(log_a)
    u = jnp.sqrt(1.0 - jnp.exp(2.0 * log_a)) * (i * xc)

    row = lax.broadcasted_iota(jnp.int32, (ts, width), 0)
    step = 1
    while step < ts:
        valid = row >= step
        a_prev = pltpu.roll(a, step, 0)
        u_prev = pltpu.roll(u, step, 0)
        u = jnp.where(valid, a * u_prev + u, u)
        a = jnp.where(valid, a * a_prev, a)
        step *= 2
    h = u + a * h_ref[...]
    h_ref[...] = h[ts - 1:ts, :]
    o_ref[...] = (h * _silu(ga_ref[...].astype(F32))).astype(BF16)


def _lru(y, conv_w, conv_b, w_gates, b_gates, lam, *, batch, seq, width, ts=512):
    ns = seq // ts
    return pl.pallas_call(
        functools.partial(_lru_kernel, ts=ts, width=width),
        grid=(batch, ns),
        in_specs=[pl.BlockSpec((ts, width), lambda b, s: (b * ns + s, 0)),
                  pl.BlockSpec((ts, width), lambda b, s: (b * ns + s, 1)),
                  pl.BlockSpec((CONV_WIDTH, width), lambda b, s: (0, 0)),
                  pl.BlockSpec((1, width), lambda b, s: (0, 0)),
                  pl.BlockSpec((width, 2 * width), lambda b, s: (0, 0)),
                  pl.BlockSpec((1, 2 * width), lambda b, s: (0, 0)),
                  pl.BlockSpec((1, width), lambda b, s: (0, 0))],
        out_specs=pl.BlockSpec((ts, width), lambda b, s: (b * ns + s, 0)),
        out_shape=jax.ShapeDtypeStruct((batch * seq, width), BF16),
        scratch_shapes=[pltpu.VMEM((ts + SUBLANES, width), F32),
                        pltpu.VMEM((1, width), F32)],
        compiler_params=_params(2),
        name="rg_lru",
    )(y, y, conv_w, conv_b.reshape(1, width), w_gates, b_gates.reshape(1, 2 * width),
      lam.reshape(1, width))


def _store_transposed_queries(q_ref, qt_scr):
    for blk in range(qt_scr.shape[0]):
        qt_scr[blk] = q_ref[:, blk * LANES:(blk + 1) * LANES].astype(F32).T.astype(BF16)


def _score_stage(k_ref, kb, qt_scr, s_scr):
    for blk in range(qt_scr.shape[0]):
        both = jnp.dot(k_ref[0, kb, blk], qt_scr[blk], preferred_element_type=F32)
        s_scr[2 * blk] = both[:KV_TILE]
        s_scr[2 * blk + 1] = both[KV_TILE:]


def _attention_scratch(streams, tq, value_rows):
    return [pltpu.VMEM((streams // 2, LANES, tq), BF16),
            pltpu.VMEM((streams, KV_TILE, tq), F32),
            pltpu.VMEM((streams, KV_TILE, tq), BF16),
            pltpu.VMEM((streams, value_rows, tq), F32)]


def _sb_kernel(q_ref, k_ref, vt_ref, g_ref, o_ref, qt_scr, s_scr, w_scr, acc_scr, *, tq):
    qi = pl.program_id(2)
    streams = s_scr.shape[0]
    _store_transposed_queries(q_ref, qt_scr)
    acc_scr[...] = jnp.zeros(acc_scr.shape, F32)

    r = lax.broadcasted_iota(jnp.int32, (KV_TILE, tq), 0)
    c = lax.broadcasted_iota(jnp.int32, (KV_TILE, tq), 1)
    causal = ((r % SUBLANES) * SEGMENT + r // SUBLANES) < c
    srow = lax.broadcasted_iota(jnp.int32, (SUBLANES, tq), 0)

    def tile_index(t):
        return jnp.maximum(qi - t, 0)

    def weight_stage(later, masked):
        new_later = []
        for g in range(streams):
            run = jnp.ones((SUBLANES, tq), F32)
            for v in reversed(range(SEGMENT)):
                rows = slice(v * SUBLANES, (v + 1) * SUBLANES)
                half_tanh = 0.5 * jnp.tanh(s_scr[g, rows, :])
                beta = 0.5 + half_tanh
                rest = 0.5 - half_tanh
                if masked:
                    beta = jnp.where(causal[rows], beta, 0.0)
                    rest = jnp.where(causal[rows], rest, 1.0)
                s_scr[g, rows, :] = beta * run
                run = run * rest
            off = jnp.broadcast_to(later[g], (SUBLANES, tq))
            for s2 in range(1, SUBLANES):
                off = off * jnp.where(srow < s2, run[s2:s2 + 1, :], 1.0)
            off2 = jnp.concatenate([off, off], axis=0)
            for v in range(SEGMENT // 2):
                rows = slice(2 * v * SUBLANES, 2 * (v + 1) * SUBLANES)
                w_scr[g, rows, :] = (s_scr[g, rows, :] * off2).astype(BF16)
            new_later.append((off * run)[0:1, :])
        return tuple(new_later)

    def value_stage(t):
        vt_tile = vt_ref[0, tile_index(t)]
        for g in range(streams):
            acc_scr[g] += jnp.dot(vt_tile[g * HEAD_DIM:(g + 1) * HEAD_DIM, :], w_scr[g],
                                  preferred_element_type=F32)

    _score_stage(k_ref, qi, qt_scr, s_scr)
    later = weight_stage(tuple(jnp.ones((1, tq), F32) for _ in range(streams)), True)
    _score_stage(k_ref, tile_index(1), qt_scr, s_scr)

    def step(j, later):
        value_stage(j - 1)
        later = weight_stage(later, False)
        _score_stage(k_ref, tile_index(j + 1), qt_scr, s_scr)
        return later

    lax.fori_loop(1, qi + 1, step, later)
    value_stage(qi)
    out_t = acc_scr[...].reshape(streams * HEAD_DIM, tq)
    o_ref[...] = (out_t.T * _silu(g_ref[...].astype(F32))).astype(BF16)


def _sb_attention(y, k_tiles, vt_tiles, *, batch, seq, q0, g0, width):
    nq = seq // Q_TILE
    nkb = seq // KV_TILE
    cols = SB_STREAMS * HEAD_DIM
    return pl.pallas_call(
        functools.partial(_sb_kernel, tq=Q_TILE),
        grid=(batch, width // cols, nq),
        in_specs=[pl.BlockSpec((Q_TILE, cols), lambda b, p, i: (b * nq + i, q0 // cols + p)),
                  pl.BlockSpec((1, nkb, cols // LANES, 2 * KV_TILE, LANES),
                               lambda b, p, i: (b, 0, p, 0, 0)),
                  pl.BlockSpec((1, nkb, cols, KV_TILE), lambda b, p, i: (b, 0, p, 0)),
                  pl.BlockSpec((Q_TILE, cols), lambda b, p, i: (b * nq + i, g0 // cols + p))],
        out_specs=pl.BlockSpec((Q_TILE, cols), lambda b, p, i: (b * nq + i, p)),
        out_shape=jax.ShapeDtypeStruct((batch * seq, width), BF16),
        scratch_shapes=_attention_scratch(SB_STREAMS, Q_TILE, HEAD_DIM),
        compiler_params=_params(3),
        name="stick_breaking_attention",
    )(y, k_tiles, vt_tiles, y)


def _diff_kernel(q_ref, k_ref, vt_ref, g_ref, lq1_ref, lk1_ref, lq2_ref, lk2_ref, sg_ref,
                 o_ref, qt_scr, s_scr, p_scr, acc_scr, *, tq, lambda_init):
    qi = pl.program_id(2)
    streams = s_scr.shape[0]
    _store_transposed_queries(q_ref, qt_scr)
    acc_scr[...] = jnp.zeros(acc_scr.shape, F32)
    r = lax.broadcasted_iota(jnp.int32, (KV_TILE, tq), 0)
    c = lax.broadcasted_iota(jnp.int32, (KV_TILE, tq), 1)
    causal = r <= c
    vdim = 2 * HEAD_DIM

    def tile_index(t):
        return jnp.maximum(qi - t, 0)

    def softmax_stage(stats, masked):
        new = []
        for g in range(streams):
            m, l, _ = stats[g]
            s = s_scr[g]
            if masked:
                s = jnp.where(causal, s, -jnp.inf)
            m_new = jnp.maximum(m, jnp.max(s, axis=0, keepdims=True))
            p = jnp.exp(s - m_new)
            alpha = jnp.exp(m - m_new)
            l = alpha * l + jnp.sum(p, axis=0, keepdims=True)
            p_scr[g] = p.astype(BF16)
            new.append((m_new, l, alpha))
        return tuple(new)

    def value_stage(t, stats):
        vt_tile = vt_ref[0, tile_index(t)]
        for g in range(streams):
            hd = g // 2
            acc_scr[g] = stats[g][2] * acc_scr[g] + jnp.dot(
                vt_tile[hd * vdim:(hd + 1) * vdim, :], p_scr[g], preferred_element_type=F32)

    init = tuple((jnp.full((1, tq), -jnp.inf, F32), jnp.zeros((1, tq), F32),
                  jnp.zeros((1, tq), F32)) for _ in range(streams))
    _score_stage(k_ref, qi, qt_scr, s_scr)
    stats = softmax_stage(init, True)
    _score_stage(k_ref, tile_index(1), qt_scr, s_scr)

    def step(j, stats):
        value_stage(j - 1, stats)
        stats = softmax_stage(stats, False)
        _score_stage(k_ref, tile_index(j + 1), qt_scr, s_scr)
        return stats

    stats = lax.fori_loop(1, qi + 1, step, stats)
    value_stage(qi, stats)

    lam = (jnp.exp(jnp.sum(lq1_ref[...] * lk1_ref[...], axis=-1, keepdims=True))
           - jnp.exp(jnp.sum(lq2_ref[...] * lk2_ref[...], axis=-1, keepdims=True))
           + lambda_init)
    for hd in range(streams // 2):
        l1, l2 = stats[2 * hd][1], stats[2 * hd + 1][1]
        out_t = acc_scr[2 * hd] / l1 - lam * (acc_scr[2 * hd + 1] / l2)
        ms = jnp.mean(out_t * out_t, axis=0, keepdims=True)
        out = (out_t * lax.rsqrt(ms + EPS)).T
        out = out * sg_ref[...] * (1.0 - lambda_init)
        cols = slice(hd * vdim, (hd + 1) * vdim)
        o_ref[:, cols] = (out * _silu(g_ref[:, cols].astype(F32))).astype(BF16)


def _diff_attention(y, k_tiles, vt_tiles, lq1, lk1, lq2, lk2, subln_g, *, batch, seq, width,
                    lambda_init):
    nq = seq // Q_TILE
    nkb = seq // KV_TILE
    cols = DIFF_STREAMS * HEAD_DIM
    g_col = 3 * width // cols
    vec = lambda a: a.reshape(1, -1)
    small = lambda n: pl.BlockSpec((1, n), lambda b, h, i: (0, 0))
    return pl.pallas_call(
        functools.partial(_diff_kernel, tq=Q_TILE, lambda_init=lambda_init),
        grid=(batch, width // cols, nq),
        in_specs=[pl.BlockSpec((Q_TILE, cols), lambda b, h, i: (b * nq + i, h)),
                  pl.BlockSpec((1, nkb, cols // LANES, 2 * KV_TILE, LANES),
                               lambda b, h, i: (b, 0, h, 0, 0)),
                  pl.BlockSpec((1, nkb, cols, KV_TILE), lambda b, h, i: (b, 0, h, 0)),
                  pl.BlockSpec((Q_TILE, cols), lambda b, h, i: (b * nq + i, g_col + h)),
                  small(HEAD_DIM), small(HEAD_DIM), small(HEAD_DIM), small(HEAD_DIM),
                  small(2 * HEAD_DIM)],
        out_specs=pl.BlockSpec((Q_TILE, cols), lambda b, h, i: (b * nq + i, h)),
        out_shape=jax.ShapeDtypeStruct((batch * seq, width), BF16),
        scratch_shapes=_attention_scratch(DIFF_STREAMS, Q_TILE, 2 * HEAD_DIM),
        compiler_params=_params(3),
        name="differential_attention",
    )(y, k_tiles, vt_tiles, y, vec(lq1), vec(lk1), vec(lq2), vec(lk2), vec(subln_g))


def _post_kernel(*refs, final):
    if final:
        (h_ref, ma_ref, mb_ref, p_ref, woa_ref, wob_ref, gp_ref, wg_ref, wp_ref,
         fn_ref, o_ref) = refs
    else:
        h_ref, ma_ref, mb_ref, p_ref, woa_ref, wob_ref, gp_ref, wg_ref, wp_ref, o_ref = refs
    h = (h_ref[...]
         + jnp.dot(ma_ref[...], woa_ref[...], preferred_element_type=F32)
         + jnp.dot(mb_ref[...], wob_ref[...], preferred_element_type=F32))
    ms = jnp.mean(h * h, axis=-1, keepdims=True)
    hn = (h * lax.rsqrt(ms + EPS) * gp_ref[...]).astype(BF16)
    gate = _sigmoid(jnp.dot(hn, wg_ref[...], preferred_element_type=F32))
    ple = jnp.dot(p_ref[...].astype(BF16), wp_ref[...], preferred_element_type=F32)
    h = h + gate * ple
    if final:
        ms = jnp.mean(h * h, axis=-1, keepdims=True)
        h = h * lax.rsqrt(ms + EPS) * fn_ref[...]
    o_ref[...] = h


def _post(h, mix_a, mix_b, b_col, p, w_out, g_ple, w_gate, w_proj, final_norm=None, *, tm=512):
    t, d = h.shape
    half = w_out.shape[0] // 2
    pd = p.shape[1]
    final = final_norm is not None
    const = lambda shape: pl.BlockSpec(shape, lambda i: (0, 0))
    in_specs = [pl.BlockSpec((tm, d), lambda i: (i, 0)),
                pl.BlockSpec((tm, half), lambda i: (i, 0)),
                pl.BlockSpec((tm, half), lambda i: (i, b_col)),
                pl.BlockSpec((tm, pd), lambda i: (i, 0)),
                pl.BlockSpec((half, d), lambda i: (0, 0)),
                pl.BlockSpec((half, d), lambda i: (1, 0)),
                const((1, d)), const((d, d)), const((pd, d))]
    args = [h, mix_a, mix_b, p, w_out, w_out, g_ple.reshape(1, d), w_gate, w_proj]
    if final:
        in_specs.append(const((1, d)))
        args.append(final_norm.reshape(1, d))
    return pl.pallas_call(
        functools.partial(_post_kernel, final=final),
        grid=(t // tm,),
        in_specs=in_specs,
        out_specs=pl.BlockSpec((tm, d), lambda i: (i, 0)),
        out_shape=jax.ShapeDtypeStruct((t, d), F32),
        compiler_params=_params(1),
        name="out_proj_ple_final" if final else "out_proj_ple",
    )(*args)


def _key_tiles(x, batch, seq, permute):
    cols = x.shape[1]
    nkb = seq // KV_TILE
    if permute:
        x = x.reshape(batch, nkb, SUBLANES, SEGMENT, cols)
        x = x.transpose(0, 1, 3, 2, 4)
    return x.reshape(batch, nkb, KV_TILE, cols)


def _stacked_key_blocks(k_tiles):
    b, n, rows, cols = k_tiles.shape
    x = k_tiles.reshape(b, n, rows, cols // LANES, LANES).transpose(0, 1, 3, 2, 4)
    low = jnp.arange(LANES) < HEAD_DIM
    zero = jnp.zeros_like(x)
    return jnp.concatenate([jnp.where(low, x, zero), jnp.where(low, zero, x)], axis=3)


def _block_diag(w):
    nb, bs, _ = w.shape
    eye = jnp.eye(nb, dtype=w.dtype)
    return jnp.einsum('ncd,nm->ncmd', w, eye).reshape(nb * bs, nb * bs)


def kernel(x, p, positions, norm_mix, norm_ple, w_ple_gate, w_ple_proj, w_in_e, conv_w, conv_b,
           lru_wa, lru_ba, lru_wx, lru_bx, lru_lambda, w_out_e, w_in_o, lam_q1, lam_k1, lam_q2,
           lam_k2, subln_g, w_out_o, final_norm):
    batch, seq, d = x.shape
    depth = norm_mix.shape[0]
    t = batch * seq
    lru_w = conv_w.shape[-1]
    sb_w = (w_in_e.shape[-1] - 2 * lru_w) // 4
    diff_w = w_in_o.shape[-1] // 4

    half = HEAD_DIM // 2
    inv_freq = ROPE_THETA ** (-jnp.arange(0, HEAD_DIM, 2, dtype=F32) / HEAD_DIM)
    ang = positions.astype(F32).reshape(t, 1) * inv_freq
    cos = jnp.tile(jnp.cos(ang), (1, LANES // half))
    sin = jnp.tile(jnp.concatenate([-jnp.sin(ang), jnp.sin(ang)], axis=1), (1, LANES // HEAD_DIM))

    h = x.reshape(t, d)
    for i in range(depth):
        j = i // 2
        last = i == depth - 1
        if i % 2 == 0:
            q0 = 2 * lru_w
            y = _in_proj(h, norm_mix[i], w_in_e[j].astype(BF16), scale_range=(q0, q0 + sb_w),
                         scale=0.5 * HEAD_DIM ** -0.5)
            w_gates = jnp.concatenate([_block_diag(lru_wa[j]), _block_diag(lru_wx[j])], axis=1)
            b_gates = jnp.concatenate([lru_ba[j], lru_bx[j]])
            oa = _lru(y, conv_w[j], conv_b[j], w_gates.astype(BF16), b_gates, lru_lambda[j],
                      batch=batch, seq=seq, width=lru_w)
            k_tiles = _stacked_key_blocks(
                _key_tiles(y[:, q0 + sb_w:q0 + 2 * sb_w], batch, seq, True))
            v_tiles = _key_tiles(y[:, q0 + 2 * sb_w:q0 + 3 * sb_w], batch, seq, True)
            vt_tiles = v_tiles.transpose(0, 1, 3, 2)
            ob = _sb_attention(y, k_tiles, vt_tiles, batch=batch, seq=seq,
                               q0=q0, g0=q0 + 3 * sb_w, width=sb_w)
            mix_a, mix_b, b_col, w_out = oa, ob, 0, w_out_e[j]
        else:
            lambda_init = 0.8 - 0.6 * math.exp(-0.3 * i)
            y = _in_proj(h, norm_mix[i], w_in_o[j].astype(BF16), cos, sin,
                         rope_cols=2 * diff_w, scale_range=(0, diff_w), scale=HEAD_DIM ** -0.5)
            k_tiles = _stacked_key_blocks(
                _key_tiles(y[:, diff_w:2 * diff_w], batch, seq, False))
            vt_tiles = _key_tiles(y[:, 2 * diff_w:3 * diff_w], batch, seq,
                                  False).transpose(0, 1, 3, 2)
            o = _diff_attention(y, k_tiles, vt_tiles, lam_q1[j], lam_k1[j], lam_q2[j], lam_k2[j],
                                subln_g[j], batch=batch, seq=seq, width=diff_w,
                                lambda_init=lambda_init)
            mix_a, mix_b, b_col, w_out = o, o, 1, w_out_o[j]
        h = _post(h, mix_a, mix_b, b_col, p[i].reshape(t, -1), w_out.astype(BF16), norm_ple[i],
                  w_ple_gate[i].astype(BF16), w_ple_proj[i].astype(BF16),
                  final_norm if last else None)
    return h.reshape(batch, seq, d)
```

```python
import functools
import math

import jax
import jax.numpy as jnp
from jax import lax
from jax.experimental import pallas as pl
from jax.experimental.pallas import tpu as pltpu

F32 = jnp.float32
BF16 = jnp.bfloat16

EPS = 1e-6
HEAD_DIM = 64
LRU_C = 8.0
CONV_WIDTH = 4
ROPE_THETA = 10000.0

LANES = 128
SUBLANES = 8
KV_TILE = 256
Q_TILE = 256
SEGMENT = KV_TILE // SUBLANES
SB_STREAMS = 8
DIFF_STREAMS = 8
VMEM_LIMIT = 48 * 1024 * 1024


def _params(n_axes):
    return pltpu.CompilerParams(
        dimension_semantics=("arbitrary",) * n_axes, vmem_limit_bytes=VMEM_LIMIT)


def _sigmoid(x):
    return 1.0 / (1.0 + jnp.exp(-x))


def _silu(x):
    return x * _sigmoid(x)


def _rotary(y, cos, sin, first_half):
    width = y.shape[1]
    partner = jnp.where(first_half, pltpu.roll(y, width - HEAD_DIM // 2, 1),
                        pltpu.roll(y, HEAD_DIM // 2, 1))
    return y * cos + partner * sin


def _in_proj_kernel(*refs, chunk, rope, q_range, scale, permute_keys):
    if rope:
        x_ref, g_ref, wm_ref, wk_ref, wv_ref, cos_ref, sin_ref = refs[:7]
    else:
        x_ref, g_ref, wm_ref, wk_ref, wv_ref = refs[:5]
    m_ref, k_ref, vt_ref = refs[-4:-1] if permute_keys else refs[-3:]
    tm = x_ref.shape[0]
    x = x_ref[...]
    ms = jnp.mean(x * x, axis=-1, keepdims=True)
    xn = x * lax.rsqrt(ms + EPS) * g_ref[...]
    xn_bf = xn.astype(BF16)
    if rope:
        reps = chunk // LANES
        cos = jnp.concatenate([cos_ref[...]] * reps, axis=1)
        sin = jnp.concatenate([sin_ref[...]] * reps, axis=1)
        lane = lax.broadcasted_iota(jnp.int32, (tm, chunk), 1)
        first_half = (lane % HEAD_DIM) < (HEAD_DIM // 2)

    for c in range(wm_ref.shape[1] // chunk):
        lo = c * chunk
        y = jnp.dot(xn_bf, wm_ref[:, lo:lo + chunk], preferred_element_type=F32)
        if q_range[0] <= lo < q_range[1]:
            if rope:
                y = _rotary(y, cos, sin, first_half)
            y = y * scale
        m_ref[:, lo:lo + chunk] = y.astype(BF16)

    n_tiles = tm // KV_TILE
    if permute_keys:
        xs_ref = refs[-1]
        n_blk = xs_ref.shape[0]
        for b in range(n_blk):
            xs_ref[b] = xn[:, b * LANES:(b + 1) * LANES]
        rows = [jnp.concatenate(
                    [xs_ref[b, pl.ds(t * KV_TILE + v, SUBLANES, stride=SEGMENT), :]
                     for b in range(n_blk)], axis=1)
                for t in range(n_tiles) for v in range(SEGMENT)]
        xk = jnp.concatenate(rows, axis=0).astype(BF16)
    else:
        xk = xn_bf

    low = lax.broadcasted_iota(jnp.int32, (KV_TILE, LANES), 1) < HEAD_DIM
    for c in range(wk_ref.shape[1] // chunk):
        kk = jnp.dot(xk, wk_ref[:, c * chunk:(c + 1) * chunk], preferred_element_type=F32)
        if rope:
            kk = _rotary(kk, cos, sin, first_half)
        for t in range(n_tiles):
            for b in range(chunk // LANES):
                blk = kk[t * KV_TILE:(t + 1) * KV_TILE, b * LANES:(b + 1) * LANES]
                nb = c * (chunk // LANES) + b
                k_ref[0, t, nb, 0:KV_TILE, :] = jnp.where(low, blk, 0.0).astype(BF16)
                k_ref[0, t, nb, KV_TILE:2 * KV_TILE, :] = jnp.where(low, 0.0, blk).astype(BF16)
    for c in range(wv_ref.shape[1] // chunk):
        vv = jnp.dot(xk, wv_ref[:, c * chunk:(c + 1) * chunk], preferred_element_type=F32)
        for t in range(n_tiles):
            vt_ref[0, t, c * chunk:(c + 1) * chunk, :] = (
                vv[t * KV_TILE:(t + 1) * KV_TILE, :].T.astype(BF16))


def _in_proj(h, g, w_main, w_k, w_v, cos=None, sin=None, *, batch, seq, q_range, scale,
             permute_keys, tm=512, chunk=512):
    t, d = h.shape
    n_main, n_k, n_v = w_main.shape[1], w_k.shape[1], w_v.shape[1]
    rope = cos is not None
    steps = seq // tm
    tiles = tm // KV_TILE
    nkb = seq // KV_TILE
    const = lambda shape: pl.BlockSpec(shape, lambda i: (0, 0))
    in_specs = [pl.BlockSpec((tm, d), lambda i: (i, 0)), const((1, d)),
                const((d, n_main)), const((d, n_k)), const((d, n_v))]
    args = [h, g.reshape(1, d), w_main, w_k, w_v]
    if rope:
        in_specs += [pl.BlockSpec((tm, LANES), lambda i: (i, 0))] * 2
        args += [cos, sin]
    return pl.pallas_call(
        functools.partial(_in_proj_kernel, chunk=chunk, rope=rope, q_range=q_range,
                          scale=scale, permute_keys=permute_keys),
        grid=(t // tm,),
        in_specs=in_specs,
        out_specs=[pl.BlockSpec((tm, n_main), lambda i: (i, 0)),
                   pl.BlockSpec((1, tiles, n_k // LANES, 2 * KV_TILE, LANES),
                                lambda i: (i // steps, i % steps, 0, 0, 0)),
                   pl.BlockSpec((1, tiles, n_v, KV_TILE),
                                lambda i: (i // steps, i % steps, 0, 0))],
        out_shape=[jax.ShapeDtypeStruct((t, n_main), BF16),
                   jax.ShapeDtypeStruct((batch, nkb, n_k // LANES, 2 * KV_TILE, LANES), BF16),
                   jax.ShapeDtypeStruct((batch, nkb, n_v, KV_TILE), BF16)],
        scratch_shapes=[pltpu.VMEM((d // LANES, tm, LANES), F32)] if permute_keys else [],
        compiler_params=_params(1),
        name="in_proj_rope" if rope else "in_proj",
    )(*args)


def _lru_kernel(xa_ref, ga_ref, cw_ref, cb_ref, wg_ref, bg_ref, lam_ref, o_ref,
                ext_ref, h_ref, *, ts, width):
    @pl.when(pl.program_id(1) == 0)
    def _():
        ext_ref[0:SUBLANES, :] = jnp.zeros((SUBLANES, width), F32)
        h_ref[...] = jnp.zeros((1, width), F32)

    xa = xa_ref[...].astype(F32)
    ext_ref[SUBLANES:SUBLANES + ts, :] = xa
    xc = cb_ref[...] + cw_ref[CONV_WIDTH - 1:CONV_WIDTH, :] * xa
    for j in range(1, CONV_WIDTH):
        k = CONV_WIDTH - 1 - j
        xc = xc + cw_ref[k:k + 1, :] * ext_ref[SUBLANES - j:SUBLANES - j + ts, :]
    ext_ref[0:SUBLANES, :] = ext_ref[ts:ts + SUBLANES, :]

    gates = jnp.dot(xc.astype(BF16), wg_ref[...], preferred_element_type=F32) + bg_ref[...]
    r = _sigmoid(gates[:, :width])
    i = _sigmoid(gates[:, width:])
    lam = lam_ref[...]
    log_sig_lam = jnp.minimum(lam, 0.0) - jnp.log(1.0 + jnp.exp(-jnp.abs(lam)))
    log_a = LRU_C * r * log_sig_lam
    a = jnp.exp(log_a)
    u = jnp.sqrt(1.0 - jnp.exp(2.0 * log_a)) * (i * xc)

    row = lax.broadcasted_iota(jnp.int32, (ts, width), 0)
    step = 1
    while step < ts:
        valid = row >= step
        a_prev = pltpu.roll(a, step, 0)
        u_prev = pltpu.roll(u, step, 0)
        u = jnp.where(valid, a * u_prev + u, u)
        a = jnp.where(valid, a * a_prev, a)
        step *= 2
    h = u + a * h_ref[...]
    h_ref[...] = h[ts - 1:ts, :]
    o_ref[...] = (h * _silu(ga_ref[...].astype(F32))).astype(BF16)


def _lru(y, conv_w, conv_b, w_gates, b_gates, lam, *, batch, seq, width, ts=512):
    ns = seq // ts
    return pl.pallas_call(
        functools.partial(_lru_kernel, ts=ts, width=width),
        grid=(batch, ns),
        in_specs=[pl.BlockSpec((ts, width), lambda b, s: (b * ns + s, 0)),
                  pl.BlockSpec((ts, width), lambda b, s: (b * ns + s, 1)),
                  pl.BlockSpec((CONV_WIDTH, width), lambda b, s: (0, 0)),
                  pl.BlockSpec((1, width), lambda b, s: (0, 0)),
                  pl.BlockSpec((width, 2 * width), lambda b, s: (0, 0)),
                  pl.BlockSpec((1, 2 * width), lambda b, s: (0, 0)),
                  pl.BlockSpec((1, width), lambda b, s: (0, 0))],
        out_specs=pl.BlockSpec((ts, width), lambda b, s: (b * ns + s, 0)),
        out_shape=jax.ShapeDtypeStruct((batch * seq, width), BF16),
        scratch_shapes=[pltpu.VMEM((ts + SUBLANES, width), F32),
                        pltpu.VMEM((1, width), F32)],
        compiler_params=_params(2),
        name="rg_lru",
    )(y, y, conv_w, conv_b.reshape(1, width), w_gates, b_gates.reshape(1, 2 * width),
      lam.reshape(1, width))


def _store_transposed_queries(q_ref, qt_scr):
    for blk in range(qt_scr.shape[0]):
        qt_scr[blk] = q_ref[:, blk * LANES:(blk + 1) * LANES].astype(F32).T.astype(BF16)


def _score_stage(k_ref, kb, qt_scr, s_scr):
    for blk in range(qt_scr.shape[0]):
        both = jnp.dot(k_ref[0, kb, blk], qt_scr[blk], preferred_element_type=F32)
        s_scr[2 * blk] = both[:KV_TILE]
        s_scr[2 * blk + 1] = both[KV_TILE:]


def _attention_scratch(streams, tq, value_rows):
    return [pltpu.VMEM((streams // 2, LANES, tq), BF16),
            pltpu.VMEM((streams, KV_TILE, tq), F32),
            pltpu.VMEM((streams, KV_TILE, tq), BF16),
            pltpu.VMEM((streams, value_rows, tq), F32)]


def _sb_kernel(q_ref, k_ref, vt_ref, g_ref, o_ref, qt_scr, s_scr, w_scr, acc_scr, *, tq):
    qi = pl.program_id(2)
    streams = s_scr.shape[0]
    _store_transposed_queries(q_ref, qt_scr)
    acc_scr[...] = jnp.zeros(acc_scr.shape, F32)

    r = lax.broadcasted_iota(jnp.int32, (KV_TILE, tq), 0)
    c = lax.broadcasted_iota(jnp.int32, (KV_TILE, tq), 1)
    causal = ((r % SUBLANES) * SEGMENT + r // SUBLANES) < c
    srow = lax.broadcasted_iota(jnp.int32, (SUBLANES, tq), 0)

    def tile_index(t):
        return jnp.maximum(qi - t, 0)

    def weight_stage(later, masked):
        new_later = []
        for g in range(streams):
            run = jnp.ones((SUBLANES, tq), F32)
            for v in reversed(range(SEGMENT)):
                rows = slice(v * SUBLANES, (v + 1) * SUBLANES)
                half_tanh = 0.5 * jnp.tanh(s_scr[g, rows, :])
                beta = 0.5 + half_tanh
                rest = 0.5 - half_tanh
                if masked:
                    beta = jnp.where(causal[rows], beta, 0.0)
                    rest = jnp.where(causal[rows], rest, 1.0)
                s_scr[g, rows, :] = beta * run
                run = run * rest
            off = jnp.broadcast_to(later[g], (SUBLANES, tq))
            for s2 in range(1, SUBLANES):
                off = off * jnp.where(srow < s2, run[s2:s2 + 1, :], 1.0)
            off2 = jnp.concatenate([off, off], axis=0)
            for v in range(SEGMENT // 2):
                rows = slice(2 * v * SUBLANES, 2 * (v + 1) * SUBLANES)
                w_scr[g, rows, :] = (s_scr[g, rows, :] * off2).astype(BF16)
            new_later.append((off * run)[0:1, :])
        return tuple(new_later)

    def value_stage(t):
        vt_tile = vt_ref[0, tile_index(t)]
        for g in range(streams):
            acc_scr[g] += jnp.dot(vt_tile[g * HEAD_DIM:(g + 1) * HEAD_DIM, :], w_scr[g],
                                  preferred_element_type=F32)

    _score_stage(k_ref, qi, qt_scr, s_scr)
    later = weight_stage(tuple(jnp.ones((1, tq), F32) for _ in range(streams)), True)
    _score_stage(k_ref, tile_index(1), qt_scr, s_scr)

    def step(j, later):
        value_stage(j - 1)
        later = weight_stage(later, False)
        _score_stage(k_ref, tile_index(j + 1), qt_scr, s_scr)
        return later

    lax.fori_loop(1, qi + 1, step, later)
    value_stage(qi)
    out_t = acc_scr[...].reshape(streams * HEAD_DIM, tq)
    o_ref[...] = (out_t.T * _silu(g_ref[...].astype(F32))).astype(BF16)


def _sb_attention(y, k_tiles, vt_tiles, *, batch, seq, q0, g0, width):
    nq = seq // Q_TILE
    nkb = seq // KV_TILE
    cols = SB_STREAMS * HEAD_DIM
    return pl.pallas_call(
        functools.partial(_sb_kernel, tq=Q_TILE),
        grid=(batch, width // cols, nq),
        in_specs=[pl.BlockSpec((Q_TILE, cols), lambda b, p, i: (b * nq + i, q0 // cols + p)),
                  pl.BlockSpec((1, nkb, cols // LANES, 2 * KV_TILE, LANES),
                               lambda b, p, i: (b, 0, p, 0, 0)),
                  pl.BlockSpec((1, nkb, cols, KV_TILE), lambda b, p, i: (b, 0, p, 0)),
                  pl.BlockSpec((Q_TILE, cols), lambda b, p, i: (b * nq + i, g0 // cols + p))],
        out_specs=pl.BlockSpec((Q_TILE, cols), lambda b, p, i: (b * nq + i, p)),
        out_shape=jax.ShapeDtypeStruct((batch * seq, width), BF16),
        scratch_shapes=_attention_scratch(SB_STREAMS, Q_TILE, HEAD_DIM),
        compiler_params=_params(3),
        name="stick_breaking_attention",
    )(y, k_tiles, vt_tiles, y)


def _diff_kernel(q_ref, k_ref, vt_ref, g_ref, lq1_ref, lk1_ref, lq2_ref, lk2_ref, sg_ref,
                 o_ref, qt_scr, s_scr, p_scr, acc_scr, *, tq, lambda_init):
    qi = pl.program_id(2)
    streams = s_scr.shape[0]
    _store_transposed_queries(q_ref, qt_scr)
    acc_scr[...] = jnp.zeros(acc_scr.shape, F32)
    r = lax.broadcasted_iota(jnp.int32, (KV_TILE, tq), 0)
    c = lax.broadcasted_iota(jnp.int32, (KV_TILE, tq), 1)
    causal = r <= c
    vdim = 2 * HEAD_DIM

    def tile_index(t):
        return jnp.maximum(qi - t, 0)

    def softmax_stage(stats, masked):
        new = []
        for g in range(streams):
            m, l, _ = stats[g]
            s = s_scr[g]
            if masked:
                s = jnp.where(causal, s, -jnp.inf)
            m_new = jnp.maximum(m, jnp.max(s, axis=0, keepdims=True))
            p = jnp.exp(s - m_new)
            alpha = jnp.exp(m - m_new)
            l = alpha * l + jnp.sum(p, axis=0, keepdims=True)
            p_scr[g] = p.astype(BF16)
            new.append((m_new, l, alpha))
        return tuple(new)

    def value_stage(t, stats):
        vt_tile = vt_ref[0, tile_index(t)]
        for g in range(streams):
            hd = g // 2
            acc_scr[g] = stats[g][2] * acc_scr[g] + jnp.dot(
                vt_tile[hd * vdim:(hd + 1) * vdim, :], p_scr[g], preferred_element_type=F32)

    init = tuple((jnp.full((1, tq), -jnp.inf, F32), jnp.zeros((1, tq), F32),
                  jnp.zeros((1, tq), F32)) for _ in range(streams))
    _score_stage(k_ref, qi, qt_scr, s_scr)
    stats = softmax_stage(init, True)
    _score_stage(k_ref, tile_index(1), qt_scr, s_scr)

    def step(j, stats):
        value_stage(j - 1, stats)
        stats = softmax_stage(stats, False)
        _score_stage(k_ref, tile_index(j + 1), qt_scr, s_scr)
        return stats

    stats = lax.fori_loop(1, qi + 1, step, stats)
    value_stage(qi, stats)

    lam = (jnp.exp(jnp.sum(lq1_ref[...] * lk1_ref[...], axis=-1, keepdims=True))
           - jnp.exp(jnp.sum(lq2_ref[...] * lk2_ref[...], axis=-1, keepdims=True))
           + lambda_init)
    for hd in range(streams // 2):
        l1, l2 = stats[2 * hd][1], stats[2 * hd + 1][1]
        out_t = acc_scr[2 * hd] / l1 - lam * (acc_scr[2 * hd + 1] / l2)
        ms = jnp.mean(out_t * out_t, axis=0, keepdims=True)
        out = (out_t * lax.rsqrt(ms + EPS)).T
        out = out * sg_ref[...] * (1.0 - lambda_init)
        cols = slice(hd * vdim, (hd + 1) * vdim)
        o_ref[:, cols] = (out * _silu(g_ref[:, cols].astype(F32))).astype(BF16)


def _diff_attention(y, k_tiles, vt_tiles, lq1, lk1, lq2, lk2, subln_g, *, batch, seq, width,
                    lambda_init):
    nq = seq // Q_TILE
    nkb = seq // KV_TILE
    cols = DIFF_STREAMS * HEAD_DIM
    g_col = width // cols
    vec = lambda a: a.reshape(1, -1)
    small = lambda n: pl.BlockSpec((1, n), lambda b, h, i: (0, 0))
    return pl.pallas_call(
        functools.partial(_diff_kernel, tq=Q_TILE, lambda_init=lambda_init),
        grid=(batch, width // cols, nq),
        in_specs=[pl.BlockSpec((Q_TILE, cols), lambda b, h, i: (b * nq + i, h)),
                  pl.BlockSpec((1, nkb, cols // LANES, 2 * KV_TILE, LANES),
                               lambda b, h, i: (b, 0, h, 0, 0)),
                  pl.BlockSpec((1, nkb, cols, KV_TILE), lambda b, h, i: (b, 0, h, 0)),
                  pl.BlockSpec((Q_TILE, cols), lambda b, h, i: (b * nq + i, g_col + h)),
                  small(HEAD_DIM), small(HEAD_DIM), small(HEAD_DIM), small(HEAD_DIM),
                  small(2 * HEAD_DIM)],
        out_specs=pl.BlockSpec((Q_TILE, cols), lambda b, h, i: (b * nq + i, h)),
        out_shape=jax.ShapeDtypeStruct((batch * seq, width), BF16),
        scratch_shapes=_attention_scratch(DIFF_STREAMS, Q_TILE, 2 * HEAD_DIM),
        compiler_params=_params(3),
        name="differential_attention",
    )(y, k_tiles, vt_tiles, y, vec(lq1), vec(lk1), vec(lq2), vec(lk2), vec(subln_g))


def _post_kernel(*refs, final):
    if final:
        (h_ref, ma_ref, mb_ref, p_ref, woa_ref, wob_ref, gp_ref, wg_ref, wp_ref,
         fn_ref, o_ref) = refs
    else:
        h_ref, ma_ref, mb_ref, p_ref, woa_ref, wob_ref, gp_ref, wg_ref, wp_ref, o_ref = refs
    h = (h_ref[...]
         + jnp.dot(ma_ref[...], woa_ref[...], preferred_element_type=F32)
         + jnp.dot(mb_ref[...], wob_ref[...], preferred_element_type=F32))
    ms = jnp.mean(h * h, axis=-1, keepdims=True)
    hn = (h * lax.rsqrt(ms + EPS) * gp_ref[...]).astype(BF16)
    gate = _sigmoid(jnp.dot(hn, wg_ref[...], preferred_element_type=F32))
    ple = jnp.dot(p_ref[...].astype(BF16), wp_ref[...], preferred_element_type=F32)
    h = h + gate * ple
    if final:
        ms = jnp.mean(h * h, axis=-1, keepdims=True)
        h = h * lax.rsqrt(ms + EPS) * fn_ref[...]
    o_ref[...] = h


def _post(h, mix_a, mix_b, b_col, p, w_out, g_ple, w_gate, w_proj, final_norm=None, *, tm=512):
    t, d = h.shape
    half = w_out.shape[0] // 2
    pd = p.shape[1]
    final = final_norm is not None
    const = lambda shape: pl.BlockSpec(shape, lambda i: (0, 0))
    in_specs = [pl.BlockSpec((tm, d), lambda i: (i, 0)),
                pl.BlockSpec((tm, half), lambda i: (i, 0)),
                pl.BlockSpec((tm, half), lambda i: (i, b_col)),
                pl.BlockSpec((tm, pd), lambda i: (i, 0)),
                pl.BlockSpec((half, d), lambda i: (0, 0)),
                pl.BlockSpec((half, d), lambda i: (1, 0)),
                const((1, d)), const((d, d)), const((pd, d))]
    args = [h, mix_a, mix_b, p, w_out, w_out, g_ple.reshape(1, d), w_gate, w_proj]
    if final:
        in_specs.append(const((1, d)))
        args.append(final_norm.reshape(1, d))
    return pl.pallas_call(
        functools.partial(_post_kernel, final=final),
        grid=(t // tm,),
        in_specs=in_specs,
        out_specs=pl.BlockSpec((tm, d), lambda i: (i, 0)),
        out_shape=jax.ShapeDtypeStruct((t, d), F32),
        compiler_params=_params(1),
        name="out_proj_ple_final" if final else "out_proj_ple",
    )(*args)


def _block_diag(w):
    nb, bs, _ = w.shape
    eye = jnp.eye(nb, dtype=w.dtype)
    return jnp.einsum('ncd,nm->ncmd', w, eye).reshape(nb * bs, nb * bs)


def kernel(x, p, positions, norm_mix, norm_ple, w_ple_gate, w_ple_proj, w_in_e, conv_w, conv_b,
           lru_wa, lru_ba, lru_wx, lru_bx, lru_lambda, w_out_e, w_in_o, lam_q1, lam_k1, lam_q2,
           lam_k2, subln_g, w_out_o, final_norm):
    batch, seq, d = x.shape
    depth = norm_mix.shape[0]
    t = batch * seq
    lru_w = conv_w.shape[-1]
    sb_w = (w_in_e.shape[-1] - 2 * lru_w) // 4
    diff_w = w_in_o.shape[-1] // 4

    half = HEAD_DIM // 2
    inv_freq = ROPE_THETA ** (-jnp.arange(0, HEAD_DIM, 2, dtype=F32) / HEAD_DIM)
    ang = positions.astype(F32).reshape(t, 1) * inv_freq
    cos = jnp.tile(jnp.cos(ang), (1, LANES // half))
    sin = jnp.tile(jnp.concatenate([-jnp.sin(ang), jnp.sin(ang)], axis=1), (1, LANES // HEAD_DIM))

    h = x.reshape(t, d)
    for i in range(depth):
        j = i // 2
        last = i == depth - 1
        if i % 2 == 0:
            w = w_in_e[j].astype(BF16)
            q0 = 2 * lru_w
            k0, v0, g0 = q0 + sb_w, q0 + 2 * sb_w, q0 + 3 * sb_w
            w_main = jnp.concatenate([w[:, :k0], w[:, g0:]], axis=1)
            y, k_tiles, vt_tiles = _in_proj(
                h, norm_mix[i], w_main, w[:, k0:v0], w[:, v0:g0], batch=batch, seq=seq,
                q_range=(q0, k0), scale=0.5 * HEAD_DIM ** -0.5, permute_keys=True)
            w_gates = jnp.concatenate([_block_diag(lru_wa[j]), _block_diag(lru_wx[j])], axis=1)
            b_gates = jnp.concatenate([lru_ba[j], lru_bx[j]])
            oa = _lru(y, conv_w[j], conv_b[j], w_gates.astype(BF16), b_gates, lru_lambda[j],
                      batch=batch, seq=seq, width=lru_w)
            ob = _sb_attention(y, k_tiles, vt_tiles, batch=batch, seq=seq,
                               q0=q0, g0=k0, width=sb_w)
            mix_a, mix_b, b_col, w_out = oa, ob, 0, w_out_e[j]
        else:
            lambda_init = 0.8 - 0.6 * math.exp(-0.3 * i)
            w = w_in_o[j].astype(BF16)
            w_main = jnp.concatenate([w[:, :diff_w], w[:, 3 * diff_w:]], axis=1)
            y, k_tiles, vt_tiles = _in_proj(
                h, norm_mix[i], w_main, w[:, diff_w:2 * diff_w], w[:, 2 * diff_w:3 * diff_w],
                cos, sin, batch=batch, seq=seq, q_range=(0, diff_w), scale=HEAD_DIM ** -0.5,
                permute_keys=False)
            o = _diff_attention(y, k_tiles, vt_tiles, lam_q1[j], lam_k1[j], lam_q2[j], lam_k2[j],
                                subln_g[j], batch=batch, seq=seq, width=diff_w,
                                lambda_init=lambda_init)
            mix_a, mix_b, b_col, w_out = o, o, 1, w_out_o[j]
        h = _post(h, mix_a, mix_b, b_col, p[i].reshape(t, -1), w_out.astype(BF16), norm_ple[i],
                  w_ple_gate[i].astype(BF16), w_ple_proj[i].astype(BF16),
                  final_norm if last else None)
    return h.reshape(batch, seq, d)
```

```python
import functools
import math

import jax
import jax.numpy as jnp
from jax import lax
from jax.experimental import pallas as pl
from jax.experimental.pallas import tpu as pltpu

F32 = jnp.float32
BF16 = jnp.bfloat16

EPS = 1e-6
HEAD_DIM = 64
LRU_C = 8.0
CONV_WIDTH = 4
ROPE_THETA = 10000.0

LANES = 128
SUBLANES = 8
KV_TILE = 256
Q_TILE = 256
SEGMENT = KV_TILE // SUBLANES
SB_STREAMS = 8
DIFF_STREAMS = 8
VMEM_LIMIT = 48 * 1024 * 1024


def _params(n_axes):
    return pltpu.CompilerParams(
        dimension_semantics=("arbitrary",) * n_axes, vmem_limit_bytes=VMEM_LIMIT)


def _sigmoid(x):
    return 0.5 + 0.5 * jnp.tanh(0.5 * x)


def _silu(x):
    return x * _sigmoid(x)


def _rotary(y, cos, sin, first_half):
    width = y.shape[1]
    partner = jnp.where(first_half, pltpu.roll(y, width - HEAD_DIM // 2, 1),
                        pltpu.roll(y, HEAD_DIM // 2, 1))
    return y * cos + partner * sin


def _in_proj_kernel(*refs, chunk, rope, q_range, scale, permute_keys):
    if rope:
        x_ref, g_ref, wm_ref, wk_ref, wv_ref, cos_ref, sin_ref = refs[:7]
    else:
        x_ref, g_ref, wm_ref, wk_ref, wv_ref = refs[:5]
    m_ref, k_ref, vt_ref = refs[-3:]
    tm = x_ref.shape[0]
    x = x_ref[...]
    ms = jnp.mean(x * x, axis=-1, keepdims=True)
    xn = x * lax.rsqrt(ms + EPS) * g_ref[...]
    xn_bf = xn.astype(BF16)
    if rope:
        reps = chunk // LANES
        cos = jnp.concatenate([cos_ref[...]] * reps, axis=1)
        sin = jnp.concatenate([sin_ref[...]] * reps, axis=1)
        lane = lax.broadcasted_iota(jnp.int32, (tm, chunk), 1)
        first_half = (lane % HEAD_DIM) < (HEAD_DIM // 2)

    for c in range(wm_ref.shape[1] // chunk):
        lo = c * chunk
        y = jnp.dot(xn_bf, wm_ref[:, lo:lo + chunk], preferred_element_type=F32)
        if q_range[0] <= lo < q_range[1]:
            if rope:
                y = _rotary(y, cos, sin, first_half)
            y = y * scale
        m_ref[:, lo:lo + chunk] = y.astype(BF16)

    n_tiles = tm // KV_TILE
    if permute_keys:
        r = lax.broadcasted_iota(jnp.int32, (KV_TILE, KV_TILE), 0)
        c = lax.broadcasted_iota(jnp.int32, (KV_TILE, KV_TILE), 1)
        perm = jnp.where(c == (r % SUBLANES) * SEGMENT + r // SUBLANES, 1.0, 0.0).astype(BF16)
        xk = jnp.concatenate(
            [jnp.dot(perm, xn_bf[t * KV_TILE:(t + 1) * KV_TILE, :],
                     preferred_element_type=F32).astype(BF16) for t in range(n_tiles)], axis=0)
    else:
        xk = xn_bf

    low = lax.broadcasted_iota(jnp.int32, (KV_TILE, LANES), 1) < HEAD_DIM
    for c in range(wk_ref.shape[1] // chunk):
        kk = jnp.dot(xk, wk_ref[:, c * chunk:(c + 1) * chunk], preferred_element_type=F32)
        if rope:
            kk = _rotary(kk, cos, sin, first_half)
        for t in range(n_tiles):
            for b in range(chunk // LANES):
                blk = kk[t * KV_TILE:(t + 1) * KV_TILE, b * LANES:(b + 1) * LANES]
                nb = c * (chunk // LANES) + b
                k_ref[0, t, nb, 0:KV_TILE, :] = jnp.where(low, blk, 0.0).astype(BF16)
                k_ref[0, t, nb, KV_TILE:2 * KV_TILE, :] = jnp.where(low, 0.0, blk).astype(BF16)
    for c in range(wv_ref.shape[1] // chunk):
        vv = jnp.dot(xk, wv_ref[:, c * chunk:(c + 1) * chunk], preferred_element_type=F32)
        for t in range(n_tiles):
            vt_ref[0, t, c * chunk:(c + 1) * chunk, :] = (
                vv[t * KV_TILE:(t + 1) * KV_TILE, :].T.astype(BF16))


def _in_proj(h, g, w_main, w_k, w_v, cos=None, sin=None, *, batch, seq, q_range, scale,
             permute_keys, tm=512, chunk=512):
    t, d = h.shape
    n_main, n_k, n_v = w_main.shape[1], w_k.shape[1], w_v.shape[1]
    rope = cos is not None
    steps = seq // tm
    tiles = tm // KV_TILE
    nkb = seq // KV_TILE
    const = lambda shape: pl.BlockSpec(shape, lambda i: (0, 0))
    in_specs = [pl.BlockSpec((tm, d), lambda i: (i, 0)), const((1, d)),
                const((d, n_main)), const((d, n_k)), const((d, n_v))]
    args = [h, g.reshape(1, d), w_main, w_k, w_v]
    if rope:
        in_specs += [pl.BlockSpec((tm, LANES), lambda i: (i, 0))] * 2
        args += [cos, sin]
    return pl.pallas_call(
        functools.partial(_in_proj_kernel, chunk=chunk, rope=rope, q_range=q_range,
                          scale=scale, permute_keys=permute_keys),
        grid=(t // tm,),
        in_specs=in_specs,
        out_specs=[pl.BlockSpec((tm, n_main), lambda i: (i, 0)),
                   pl.BlockSpec((1, tiles, n_k // LANES, 2 * KV_TILE, LANES),
                                lambda i: (i // steps, i % steps, 0, 0, 0)),
                   pl.BlockSpec((1, tiles, n_v, KV_TILE),
                                lambda i: (i // steps, i % steps, 0, 0))],
        out_shape=[jax.ShapeDtypeStruct((t, n_main), BF16),
                   jax.ShapeDtypeStruct((batch, nkb, n_k // LANES, 2 * KV_TILE, LANES), BF16),
                   jax.ShapeDtypeStruct((batch, nkb, n_v, KV_TILE), BF16)],
        compiler_params=_params(1),
        name="in_proj_rope" if rope else "in_proj",
    )(*args)


def _lru_kernel(xa_ref, ga_ref, cw_ref, cb_ref, wg_ref, bg_ref, lam_ref, o_ref,
                ext_ref, h_ref, *, ts, width):
    @pl.when(pl.program_id(1) == 0)
    def _():
        ext_ref[0:SUBLANES, :] = jnp.zeros((SUBLANES, width), F32)
        h_ref[...] = jnp.zeros((1, width), F32)

    xa = xa_ref[...].astype(F32)
    ext_ref[SUBLANES:SUBLANES + ts, :] = xa
    xc = cb_ref[...] + cw_ref[CONV_WIDTH - 1:CONV_WIDTH, :] * xa
    for j in range(1, CONV_WIDTH):
        k = CONV_WIDTH - 1 - j
        xc = xc + cw_ref[k:k + 1, :] * ext_ref[SUBLANES - j:SUBLANES - j + ts, :]
    ext_ref[0:SUBLANES, :] = ext_ref[ts:ts + SUBLANES, :]

    gates = jnp.dot(xc.astype(BF16), wg_ref[...], preferred_element_type=F32) + bg_ref[...]
    r = _sigmoid(gates[:, :width])
    i = _sigmoid(gates[:, width:])
    lam = lam_ref[...]
    log_sig_lam = jnp.minimum(lam, 0.0) - jnp.log(1.0 + jnp.exp(-jnp.abs(lam)))
    log_a = LRU_C * r * log_sig_lam
    a = jnp.exp(log_a)
    u = jnp.sqrt(1.0 - a * a) * (i * xc)

    groups = ts // SUBLANES
    a = a.reshape(groups, SUBLANES, width)
    u = u.reshape(groups, SUBLANES, width)
    row = lax.broadcasted_iota(jnp.int32, (groups, SUBLANES, width), 1)
    step = 1
    while step < SUBLANES:
        valid = row >= step
        a_prev = pltpu.roll(a, step, 1)
        u_prev = pltpu.roll(u, step, 1)
        u = jnp.where(valid, a * u_prev + u, u)
        a = jnp.where(valid, a * a_prev, a)
        step *= 2
    gate = _silu(ga_ref[...].astype(F32))
    state = h_ref[...]
    for pair in range(ts // (2 * SUBLANES)):
        hs = []
        for grp in (2 * pair, 2 * pair + 1):
            h = u[grp] + a[grp] * state
            state = h[SUBLANES - 1:SUBLANES, :]
            hs.append(h)
        rows = slice(2 * pair * SUBLANES, 2 * (pair + 1) * SUBLANES)
        o_ref[rows, :] = (jnp.concatenate(hs, axis=0) * gate[rows]).astype(BF16)
    h_ref[...] = state


def _lru(y, conv_w, conv_b, w_gates, b_gates, lam, *, batch, seq, width, ts=512):
    ns = seq // ts
    return pl.pallas_call(
        functools.partial(_lru_kernel, ts=ts, width=width),
        grid=(batch, ns),
        in_specs=[pl.BlockSpec((ts, width), lambda b, s: (b * ns + s, 0)),
                  pl.BlockSpec((ts, width), lambda b, s: (b * ns + s, 1)),
                  pl.BlockSpec((CONV_WIDTH, width), lambda b, s: (0, 0)),
                  pl.BlockSpec((1, width), lambda b, s: (0, 0)),
                  pl.BlockSpec((width, 2 * width), lambda b, s: (0, 0)),
                  pl.BlockSpec((1, 2 * width), lambda b, s: (0, 0)),
                  pl.BlockSpec((1, width), lambda b, s: (0, 0))],
        out_specs=pl.BlockSpec((ts, width), lambda b, s: (b * ns + s, 0)),
        out_shape=jax.ShapeDtypeStruct((batch * seq, width), BF16),
        scratch_shapes=[pltpu.VMEM((ts + SUBLANES, width), F32),
                        pltpu.VMEM((1, width), F32)],
        compiler_params=_params(2),
        name="rg_lru",
    )(y, y, conv_w, conv_b.reshape(1, width), w_gates, b_gates.reshape(1, 2 * width),
      lam.reshape(1, width))


def _store_transposed_queries(q_ref, qt_scr):
    for blk in range(qt_scr.shape[0]):
        qt_scr[blk] = q_ref[:, blk * LANES:(blk + 1) * LANES].astype(F32).T.astype(BF16)


def _score_stage(k_ref, kb, qt_scr, s_scr):
    for blk in range(qt_scr.shape[0]):
        both = jnp.dot(k_ref[0, kb, blk], qt_scr[blk], preferred_element_type=F32)
        s_scr[2 * blk] = both[:KV_TILE]
        s_scr[2 * blk + 1] = both[KV_TILE:]


def _attention_scratch(streams, tq, value_rows):
    return [pltpu.VMEM((streams // 2, LANES, tq), BF16),
            pltpu.VMEM((streams, KV_TILE, tq), F32),
            pltpu.VMEM((streams, KV_TILE, tq), BF16),
            pltpu.VMEM((streams, value_rows, tq), F32)]


def _sb_kernel(q_ref, k_ref, vt_ref, g_ref, o_ref, qt_scr, s_scr, w_scr, acc_scr, *, tq):
    qi = pl.program_id(2)
    streams = s_scr.shape[0]
    _store_transposed_queries(q_ref, qt_scr)
    acc_scr[...] = jnp.zeros(acc_scr.shape, F32)

    r = lax.broadcasted_iota(jnp.int32, (KV_TILE, tq), 0)
    c = lax.broadcasted_iota(jnp.int32, (KV_TILE, tq), 1)
    causal = ((r % SUBLANES) * SEGMENT + r // SUBLANES) < c
    srow = lax.broadcasted_iota(jnp.int32, (SUBLANES, tq), 0)

    def weight_stage(later, masked):
        new_later = []
        for g in range(streams):
            run = jnp.ones((SUBLANES, tq), F32)
            for v in reversed(range(SEGMENT)):
                rows = slice(v * SUBLANES, (v + 1) * SUBLANES)
                half_tanh = 0.5 * jnp.tanh(s_scr[g, rows, :])
                beta = 0.5 + half_tanh
                rest = 0.5 - half_tanh
                if masked:
                    beta = jnp.where(causal[rows], beta, 0.0)
                    rest = jnp.where(causal[rows], rest, 1.0)
                s_scr[g, rows, :] = beta * run
                run = run * rest
            off = jnp.broadcast_to(later[g], (SUBLANES, tq))
            for s2 in range(1, SUBLANES):
                off = off * jnp.where(srow < s2, run[s2:s2 + 1, :], 1.0)
            off2 = jnp.concatenate([off, off], axis=0)
            for v in range(SEGMENT // 2):
                rows = slice(2 * v * SUBLANES, 2 * (v + 1) * SUBLANES)
                w_scr[g, rows, :] = (s_scr[g, rows, :] * off2).astype(BF16)
            new_later.append((off * run)[0:1, :])
        return tuple(new_later)

    def value_stage(t):
        vt_tile = vt_ref[0, qi - t]
        for g in range(streams):
            acc_scr[g] += jnp.dot(vt_tile[g * HEAD_DIM:(g + 1) * HEAD_DIM, :], w_scr[g],
                                  preferred_element_type=F32)

    _score_stage(k_ref, qi, qt_scr, s_scr)
    later = weight_stage(tuple(jnp.ones((1, tq), F32) for _ in range(streams)), True)
    _score_stage(k_ref, jnp.maximum(qi - 1, 0), qt_scr, s_scr)

    def step(j, later):
        value_stage(j - 1)
        later = weight_stage(later, False)
        _score_stage(k_ref, qi - (j + 1), qt_scr, s_scr)
        return later

    later = lax.fori_loop(1, qi, step, later)

    @pl.when(qi > 0)
    def _():
        value_stage(qi - 1)
        weight_stage(later, False)

    value_stage(qi)
    out_t = acc_scr[...].reshape(streams * HEAD_DIM, tq)
    o_ref[...] = (out_t.T * _silu(g_ref[...].astype(F32))).astype(BF16)


def _sb_attention(y, k_tiles, vt_tiles, *, batch, seq, q0, g0, width):
    nq = seq // Q_TILE
    nkb = seq // KV_TILE
    cols = SB_STREAMS * HEAD_DIM
    return pl.pallas_call(
        functools.partial(_sb_kernel, tq=Q_TILE),
        grid=(batch, width // cols, nq),
        in_specs=[pl.BlockSpec((Q_TILE, cols), lambda b, p, i: (b * nq + i, q0 // cols + p)),
                  pl.BlockSpec((1, nkb, cols // LANES, 2 * KV_TILE, LANES),
                               lambda b, p, i: (b, 0, p, 0, 0)),
                  pl.BlockSpec((1, nkb, cols, KV_TILE), lambda b, p, i: (b, 0, p, 0)),
                  pl.BlockSpec((Q_TILE, cols), lambda b, p, i: (b * nq + i, g0 // cols + p))],
        out_specs=pl.BlockSpec((Q_TILE, cols), lambda b, p, i: (b * nq + i, p)),
        out_shape=jax.ShapeDtypeStruct((batch * seq, width), BF16),
        scratch_shapes=_attention_scratch(SB_STREAMS, Q_TILE, HEAD_DIM),
        compiler_params=_params(3),
        name="stick_breaking_attention",
    )(y, k_tiles, vt_tiles, y)


def _diff_kernel(q_ref, k_ref, vt_ref, g_ref, lq1_ref, lk1_ref, lq2_ref, lk2_ref, sg_ref,
                 o_ref, qt_scr, s_scr, p_scr, acc_scr, *, tq, lambda_init):
    qi = pl.program_id(2)
    streams = s_scr.shape[0]
    _store_transposed_queries(q_ref, qt_scr)
    acc_scr[...] = jnp.zeros(acc_scr.shape, F32)
    r = lax.broadcasted_iota(jnp.int32, (KV_TILE, tq), 0)
    c = lax.broadcasted_iota(jnp.int32, (KV_TILE, tq), 1)
    causal = r <= c
    vdim = 2 * HEAD_DIM

    def softmax_stage(stats, masked):
        new = []
        for g in range(streams):
            m, l, _ = stats[g]
            s = s_scr[g]
            if masked:
                s = jnp.where(causal, s, -jnp.inf)
            m_new = jnp.maximum(m, jnp.max(s, axis=0, keepdims=True))
            p = jnp.exp(s - m_new)
            alpha = jnp.exp(m - m_new)
            l = alpha * l + jnp.sum(p, axis=0, keepdims=True)
            p_scr[g] = p.astype(BF16)
            new.append((m_new, l, alpha))
        return tuple(new)

    def value_stage(t, stats):
        vt_tile = vt_ref[0, qi - t]
        for g in range(streams):
            hd = g // 2
            acc_scr[g] = stats[g][2] * acc_scr[g] + jnp.dot(
                vt_tile[hd * vdim:(hd + 1) * vdim, :], p_scr[g], preferred_element_type=F32)

    init = tuple((jnp.full((1, tq), -jnp.inf, F32), jnp.zeros((1, tq), F32),
                  jnp.zeros((1, tq), F32)) for _ in range(streams))
    _score_stage(k_ref, qi, qt_scr, s_scr)
    stats = softmax_stage(init, True)
    _score_stage(k_ref, jnp.maximum(qi - 1, 0), qt_scr, s_scr)

    def step(j, stats):
        value_stage(j - 1, stats)
        stats = softmax_stage(stats, False)
        _score_stage(k_ref, qi - (j + 1), qt_scr, s_scr)
        return stats

    stats = lax.fori_loop(1, qi, step, stats)

    def last_step(stats):
        value_stage(qi - 1, stats)
        return softmax_stage(stats, False)

    stats = lax.cond(qi > 0, last_step, lambda st: st, stats)
    value_stage(qi, stats)

    lam = (jnp.exp(jnp.sum(lq1_ref[...] * lk1_ref[...], axis=-1, keepdims=True))
           - jnp.exp(jnp.sum(lq2_ref[...] * lk2_ref[...], axis=-1, keepdims=True))
           + lambda_init)
    for hd in range(streams // 2):
        l1, l2 = stats[2 * hd][1], stats[2 * hd + 1][1]
        out_t = acc_scr[2 * hd] / l1 - lam * (acc_scr[2 * hd + 1] / l2)
        ms = jnp.mean(out_t * out_t, axis=0, keepdims=True)
        out = (out_t * lax.rsqrt(ms + EPS)).T
        out = out * sg_ref[...] * (1.0 - lambda_init)
        cols = slice(hd * vdim, (hd + 1) * vdim)
        o_ref[:, cols] = (out * _silu(g_ref[:, cols].astype(F32))).astype(BF16)


def _diff_attention(y, k_tiles, vt_tiles, lq1, lk1, lq2, lk2, subln_g, *, batch, seq, width,
                    lambda_init):
    nq = seq // Q_TILE
    nkb = seq // KV_TILE
    cols = DIFF_STREAMS * HEAD_DIM
    g_col = width // cols
    vec = lambda a: a.reshape(1, -1)
    small = lambda n: pl.BlockSpec((1, n), lambda b, h, i: (0, 0))
    return pl.pallas_call(
        functools.partial(_diff_kernel, tq=Q_TILE, lambda_init=lambda_init),
        grid=(batch, width // cols, nq),
        in_specs=[pl.BlockSpec((Q_TILE, cols), lambda b, h, i: (b * nq + i, h)),
                  pl.BlockSpec((1, nkb, cols // LANES, 2 * KV_TILE, LANES),
                               lambda b, h, i: (b, 0, h, 0, 0)),
                  pl.BlockSpec((1, nkb, cols, KV_TILE), lambda b, h, i: (b, 0, h, 0)),
                  pl.BlockSpec((Q_TILE, cols), lambda b, h, i: (b * nq + i, g_col + h)),
                  small(HEAD_DIM), small(HEAD_DIM), small(HEAD_DIM), small(HEAD_DIM),
                  small(2 * HEAD_DIM)],
        out_specs=pl.BlockSpec((Q_TILE, cols), lambda b, h, i: (b * nq + i, h)),
        out_shape=jax.ShapeDtypeStruct((batch * seq, width), BF16),
        scratch_shapes=_attention_scratch(DIFF_STREAMS, Q_TILE, 2 * HEAD_DIM),
        compiler_params=_params(3),
        name="differential_attention",
    )(y, k_tiles, vt_tiles, y, vec(lq1), vec(lk1), vec(lq2), vec(lk2), vec(subln_g))


def _post_kernel(*refs, final):
    if final:
        (h_ref, ma_ref, mb_ref, p_ref, woa_ref, wob_ref, gp_ref, wg_ref, wp_ref,
         fn_ref, o_ref) = refs
    else:
        h_ref, ma_ref, mb_ref, p_ref, woa_ref, wob_ref, gp_ref, wg_ref, wp_ref, o_ref = refs
    h = (h_ref[...]
         + jnp.dot(ma_ref[...], woa_ref[...], preferred_element_type=F32)
         + jnp.dot(mb_ref[...], wob_ref[...], preferred_element_type=F32))
    ms = jnp.mean(h * h, axis=-1, keepdims=True)
    hn = (h * lax.rsqrt(ms + EPS) * gp_ref[...]).astype(BF16)
    gate = _sigmoid(jnp.dot(hn, wg_ref[...], preferred_element_type=F32))
    ple = jnp.dot(p_ref[...].astype(BF16), wp_ref[...], preferred_element_type=F32)
    h = h + gate * ple
    if final:
        ms = jnp.mean(h * h, axis=-1, keepdims=True)
        h = h * lax.rsqrt(ms + EPS) * fn_ref[...]
    o_ref[...] = h


def _post(h, mix_a, mix_b, b_col, p, w_out, g_ple, w_gate, w_proj, final_norm=None, *, tm=512):
    t, d = h.shape
    half = w_out.shape[0] // 2
    pd = p.shape[1]
    final = final_norm is not None
    const = lambda shape: pl.BlockSpec(shape, lambda i: (0, 0))
    in_specs = [pl.BlockSpec((tm, d), lambda i: (i, 0)),
                pl.BlockSpec((tm, half), lambda i: (i, 0)),
                pl.BlockSpec((tm, half), lambda i: (i, b_col)),
                pl.BlockSpec((tm, pd), lambda i: (i, 0)),
                pl.BlockSpec((half, d), lambda i: (0, 0)),
                pl.BlockSpec((half, d), lambda i: (1, 0)),
                const((1, d)), const((d, d)), const((pd, d))]
    args = [h, mix_a, mix_b, p, w_out, w_out, g_ple.reshape(1, d), w_gate, w_proj]
    if final:
        in_specs.append(const((1, d)))
        args.append(final_norm.reshape(1, d))
    return pl.pallas_call(
        functools.partial(_post_kernel, final=final),
        grid=(t // tm,),
        in_specs=in_specs,
        out_specs=pl.BlockSpec((tm, d), lambda i: (i, 0)),
        out_shape=jax.ShapeDtypeStruct((t, d), F32),
        compiler_params=_params(1),
        name="out_proj_ple_final" if final else "out_proj_ple",
    )(*args)


def _block_diag(w):
    nb, bs, _ = w.shape
    eye = jnp.eye(nb, dtype=w.dtype)
    return jnp.einsum('ncd,nm->ncmd', w, eye).reshape(nb * bs, nb * bs)


def kernel(x, p, positions, norm_mix, norm_ple, w_ple_gate, w_ple_proj, w_in_e, conv_w, conv_b,
           lru_wa, lru_ba, lru_wx, lru_bx, lru_lambda, w_out_e, w_in_o, lam_q1, lam_k1, lam_q2,
           lam_k2, subln_g, w_out_o, final_norm):
    batch, seq, d = x.shape
    depth = norm_mix.shape[0]
    t = batch * seq
    lru_w = conv_w.shape[-1]
    sb_w = (w_in_e.shape[-1] - 2 * lru_w) // 4
    diff_w = w_in_o.shape[-1] // 4

    half = HEAD_DIM // 2
    inv_freq = ROPE_THETA ** (-jnp.arange(0, HEAD_DIM, 2, dtype=F32) / HEAD_DIM)
    ang = positions.astype(F32).reshape(t, 1) * inv_freq
    cos = jnp.tile(jnp.cos(ang), (1, LANES // half))
    sin = jnp.tile(jnp.concatenate([-jnp.sin(ang), jnp.sin(ang)], axis=1), (1, LANES // HEAD_DIM))

    h = x.reshape(t, d)
    for i in range(depth):
        j = i // 2
        last = i == depth - 1
        if i % 2 == 0:
            w = w_in_e[j].astype(BF16)
            q0 = 2 * lru_w
            k0, v0, g0 = q0 + sb_w, q0 + 2 * sb_w, q0 + 3 * sb_w
            w_main = jnp.concatenate([w[:, :k0], w[:, g0:]], axis=1)
            y, k_tiles, vt_tiles = _in_proj(
                h, norm_mix[i], w_main, w[:, k0:v0], w[:, v0:g0], batch=batch, seq=seq,
                q_range=(q0, k0), scale=0.5 * HEAD_DIM ** -0.5, permute_keys=True)
            w_gates = jnp.concatenate([_block_diag(lru_wa[j]), _block_diag(lru_wx[j])], axis=1)
            b_gates = jnp.concatenate([lru_ba[j], lru_bx[j]])
            oa = _lru(y, conv_w[j], conv_b[j], w_gates.astype(BF16), b_gates, lru_lambda[j],
                      batch=batch, seq=seq, width=lru_w)
            ob = _sb_attention(y, k_tiles, vt_tiles, batch=batch, seq=seq,
                               q0=q0, g0=k0, width=sb_w)
            mix_a, mix_b, b_col, w_out = oa, ob, 0, w_out_e[j]
        else:
            lambda_init = 0.8 - 0.6 * math.exp(-0.3 * i)
            w = w_in_o[j].astype(BF16)
            w_main = jnp.concatenate([w[:, :diff_w], w[:, 3 * diff_w:]], axis=1)
            y, k_tiles, vt_tiles = _in_proj(
                h, norm_mix[i], w_main, w[:, diff_w:2 * diff_w], w[:, 2 * diff_w:3 * diff_w],
                cos, sin, batch=batch, seq=seq, q_range=(0, diff_w), scale=HEAD_DIM ** -0.5,
                permute_keys=False)
            o = _diff_attention(y, k_tiles, vt_tiles, lam_q1[j], lam_k1[j], lam_q2[j], lam_k2[j],
                                subln_g[j], batch=batch, seq=seq, width=diff_w,
                                lambda_init=lambda_init)
            mix_a, mix_b, b_col, w_out = o, o, 1, w_out_o[j]
        h = _post(h, mix_a, mix_b, b_col, p[i].reshape(t, -1), w_out.astype(BF16), norm_ple[i],
                  w_ple_gate[i].astype(BF16), w_ple_proj[i].astype(BF16),
                  final_norm if last else None)
    return h.reshape(batch, seq, d)
```

```python
import functools
import math

import jax
import jax.numpy as jnp
from jax import lax
from jax.experimental import pallas as pl
from jax.experimental.pallas import tpu as pltpu

F32 = jnp.float32
BF16 = jnp.bfloat16

EPS = 1e-6
HEAD_DIM = 64
LRU_C = 8.0
CONV_WIDTH = 4
ROPE_THETA = 10000.0

LANES = 128
SUBLANES = 8
KV_TILE = 256
Q_TILE = 256
SEGMENT = KV_TILE // SUBLANES
SB_STREAMS = 8
DIFF_STREAMS = 8
VMEM_LIMIT = 48 * 1024 * 1024


def _params(n_axes):
    return pltpu.CompilerParams(
        dimension_semantics=("arbitrary",) * n_axes, vmem_limit_bytes=VMEM_LIMIT)


def _sigmoid(x):
    return 0.5 + 0.5 * jnp.tanh(0.5 * x)


def _silu(x):
    return x * _sigmoid(x)


def _rotary(y, cos, sin, first_half):
    width = y.shape[1]
    partner = jnp.where(first_half, pltpu.roll(y, width - HEAD_DIM // 2, 1),
                        pltpu.roll(y, HEAD_DIM // 2, 1))
    return y * cos + partner * sin


def _in_proj_kernel(*refs, chunk, rope, q_range, scale, permute_keys):
    if rope:
        x_ref, g_ref, wm_ref, wk_ref, wv_ref, cos_ref, sin_ref = refs[:7]
    else:
        x_ref, g_ref, wm_ref, wk_ref, wv_ref = refs[:5]
    m_ref, k_ref, vt_ref = refs[-3:]
    tm = x_ref.shape[0]
    x = x_ref[...]
    ms = jnp.mean(x * x, axis=-1, keepdims=True)
    xn = x * lax.rsqrt(ms + EPS) * g_ref[...]
    xn_bf = xn.astype(BF16)
    if rope:
        reps = chunk // LANES
        cos = jnp.concatenate([cos_ref[...]] * reps, axis=1)
        sin = jnp.concatenate([sin_ref[...]] * reps, axis=1)
        lane = lax.broadcasted_iota(jnp.int32, (tm, chunk), 1)
        first_half = (lane % HEAD_DIM) < (HEAD_DIM // 2)

    for c in range(wm_ref.shape[1] // chunk):
        lo = c * chunk
        y = jnp.dot(xn_bf, wm_ref[:, lo:lo + chunk], preferred_element_type=F32)
        if q_range[0] <= lo < q_range[1]:
            if rope:
                y = _rotary(y, cos, sin, first_half)
            y = y * scale
        m_ref[:, lo:lo + chunk] = y.astype(BF16)

    n_tiles = tm // KV_TILE
    if permute_keys:
        r = lax.broadcasted_iota(jnp.int32, (KV_TILE, KV_TILE), 0)
        c = lax.broadcasted_iota(jnp.int32, (KV_TILE, KV_TILE), 1)
        perm = jnp.where(c == (r % SUBLANES) * SEGMENT + r // SUBLANES, 1.0, 0.0).astype(BF16)
        xk = jnp.concatenate(
            [jnp.dot(perm, xn_bf[t * KV_TILE:(t + 1) * KV_TILE, :],
                     preferred_element_type=F32).astype(BF16) for t in range(n_tiles)], axis=0)
    else:
        xk = xn_bf

    low = lax.broadcasted_iota(jnp.int32, (KV_TILE, LANES), 1) < HEAD_DIM
    for c in range(wk_ref.shape[1] // chunk):
        kk = jnp.dot(xk, wk_ref[:, c * chunk:(c + 1) * chunk], preferred_element_type=F32)
        if rope:
            kk = _rotary(kk, cos, sin, first_half)
        for t in range(n_tiles):
            for b in range(chunk // LANES):
                blk = kk[t * KV_TILE:(t + 1) * KV_TILE, b * LANES:(b + 1) * LANES]
                nb = c * (chunk // LANES) + b
                k_ref[0, t, nb, 0:KV_TILE, :] = jnp.where(low, blk, 0.0).astype(BF16)
                k_ref[0, t, nb, KV_TILE:2 * KV_TILE, :] = jnp.where(low, 0.0, blk).astype(BF16)
    for c in range(wv_ref.shape[1] // chunk):
        vv = jnp.dot(xk, wv_ref[:, c * chunk:(c + 1) * chunk], preferred_element_type=F32)
        for t in range(n_tiles):
            vt_ref[0, t, c * chunk:(c + 1) * chunk, :] = (
                vv[t * KV_TILE:(t + 1) * KV_TILE, :].T.astype(BF16))


def _in_proj(h, g, w_main, w_k, w_v, cos=None, sin=None, *, batch, seq, q_range, scale,
             permute_keys, tm=512, chunk=512):
    t, d = h.shape
    n_main, n_k, n_v = w_main.shape[1], w_k.shape[1], w_v.shape[1]
    rope = cos is not None
    steps = seq // tm
    tiles = tm // KV_TILE
    nkb = seq // KV_TILE
    const = lambda shape: pl.BlockSpec(shape, lambda i: (0, 0))
    in_specs = [pl.BlockSpec((tm, d), lambda i: (i, 0)), const((1, d)),
                const((d, n_main)), const((d, n_k)), const((d, n_v))]
    args = [h, g.reshape(1, d), w_main, w_k, w_v]
    if rope:
        in_specs += [pl.BlockSpec((tm, LANES), lambda i: (i, 0))] * 2
        args += [cos, sin]
    return pl.pallas_call(
        functools.partial(_in_proj_kernel, chunk=chunk, rope=rope, q_range=q_range,
                          scale=scale, permute_keys=permute_keys),
        grid=(t // tm,),
        in_specs=in_specs,
        out_specs=[pl.BlockSpec((tm, n_main), lambda i: (i, 0)),
                   pl.BlockSpec((1, tiles, n_k // LANES, 2 * KV_TILE, LANES),
                                lambda i: (i // steps, i % steps, 0, 0, 0)),
                   pl.BlockSpec((1, tiles, n_v, KV_TILE),
                                lambda i: (i // steps, i % steps, 0, 0))],
        out_shape=[jax.ShapeDtypeStruct((t, n_main), BF16),
                   jax.ShapeDtypeStruct((batch, nkb, n_k // LANES, 2 * KV_TILE, LANES), BF16),
                   jax.ShapeDtypeStruct((batch, nkb, n_v, KV_TILE), BF16)],
        compiler_params=_params(1),
        name="in_proj_rope" if rope else "in_proj",
    )(*args)


def _lru_kernel(xa_ref, ga_ref, cw_ref, cb_ref, wg_ref, bg_ref, lam_ref, o_ref,
                ext_ref, h_ref, *, ts, width):
    @pl.when(pl.program_id(1) == 0)
    def _():
        ext_ref[0:SUBLANES, :] = jnp.zeros((SUBLANES, width), F32)
        h_ref[...] = jnp.zeros((1, width), F32)

    xa = xa_ref[...].astype(F32)
    ext_ref[SUBLANES:SUBLANES + ts, :] = xa
    xc = cb_ref[...] + cw_ref[CONV_WIDTH - 1:CONV_WIDTH, :] * xa
    for j in range(1, CONV_WIDTH):
        k = CONV_WIDTH - 1 - j
        xc = xc + cw_ref[k:k + 1, :] * ext_ref[SUBLANES - j:SUBLANES - j + ts, :]
    ext_ref[0:SUBLANES, :] = ext_ref[ts:ts + SUBLANES, :]

    gates = jnp.dot(xc.astype(BF16), wg_ref[...], preferred_element_type=F32) + bg_ref[...]
    r = _sigmoid(gates[:, :width])
    i = _sigmoid(gates[:, width:])
    lam = lam_ref[...]
    log_sig_lam = jnp.minimum(lam, 0.0) - jnp.log(1.0 + jnp.exp(-jnp.abs(lam)))
    log_a = LRU_C * r * log_sig_lam
    a = jnp.exp(log_a)
    u = jnp.sqrt(1.0 - a * a) * (i * xc)

    groups = ts // SUBLANES
    a = a.reshape(groups, SUBLANES, width)
    u = u.reshape(groups, SUBLANES, width)
    row = lax.broadcasted_iota(jnp.int32, (groups, SUBLANES, width), 1)
    step = 1
    while step < SUBLANES:
        valid = row >= step
        a_prev = pltpu.roll(a, step, 1)
        u_prev = pltpu.roll(u, step, 1)
        u = jnp.where(valid, a * u_prev + u, u)
        a = jnp.where(valid, a * a_prev, a)
        step *= 2
    gate = _silu(ga_ref[...].astype(F32))
    state = h_ref[...]
    for pair in range(ts // (2 * SUBLANES)):
        hs = []
        for grp in (2 * pair, 2 * pair + 1):
            h = u[grp] + a[grp] * state
            state = h[SUBLANES - 1:SUBLANES, :]
            hs.append(h)
        rows = slice(2 * pair * SUBLANES, 2 * (pair + 1) * SUBLANES)
        o_ref[rows, :] = (jnp.concatenate(hs, axis=0) * gate[rows]).astype(BF16)
    h_ref[...] = state


def _lru(y, conv_w, conv_b, w_gates, b_gates, lam, *, batch, seq, width, ts=512):
    ns = seq // ts
    return pl.pallas_call(
        functools.partial(_lru_kernel, ts=ts, width=width),
        grid=(batch, ns),
        in_specs=[pl.BlockSpec((ts, width), lambda b, s: (b * ns + s, 0)),
                  pl.BlockSpec((ts, width), lambda b, s: (b * ns + s, 1)),
                  pl.BlockSpec((CONV_WIDTH, width), lambda b, s: (0, 0)),
                  pl.BlockSpec((1, width), lambda b, s: (0, 0)),
                  pl.BlockSpec((width, 2 * width), lambda b, s: (0, 0)),
                  pl.BlockSpec((1, 2 * width), lambda b, s: (0, 0)),
                  pl.BlockSpec((1, width), lambda b, s: (0, 0))],
        out_specs=pl.BlockSpec((ts, width), lambda b, s: (b * ns + s, 0)),
        out_shape=jax.ShapeDtypeStruct((batch * seq, width), BF16),
        scratch_shapes=[pltpu.VMEM((ts + SUBLANES, width), F32),
                        pltpu.VMEM((1, width), F32)],
        compiler_params=_params(2),
        name="rg_lru",
    )(y, y, conv_w, conv_b.reshape(1, width), w_gates, b_gates.reshape(1, 2 * width),
      lam.reshape(1, width))


def _store_transposed_queries(q_ref, qt_scr):
    for blk in range(qt_scr.shape[0]):
        qt_scr[blk] = q_ref[:, blk * LANES:(blk + 1) * LANES].astype(F32).T.astype(BF16)


def _score_stage(k_ref, kb, qt_scr, s_scr, with_max=False):
    tile_max = []
    for blk in range(qt_scr.shape[0]):
        both = jnp.dot(k_ref[0, kb, blk], qt_scr[blk], preferred_element_type=F32)
        for half in range(2):
            s = both[half * KV_TILE:(half + 1) * KV_TILE]
            s_scr[2 * blk + half] = s
            if with_max:
                tile_max.append(jnp.max(s, axis=0, keepdims=True))
    return tuple(tile_max)


def _attention_scratch(streams, tq, value_rows):
    return [pltpu.VMEM((streams // 2, LANES, tq), BF16),
            pltpu.VMEM((streams, KV_TILE, tq), F32),
            pltpu.VMEM((streams, KV_TILE, tq), BF16),
            pltpu.VMEM((streams, value_rows, tq), F32)]


def _sb_kernel(q_ref, k_ref, vt_ref, g_ref, o_ref, qt_scr, s_scr, w_scr, acc_scr, *, tq):
    qi = pl.program_id(2)
    streams = s_scr.shape[0]
    _store_transposed_queries(q_ref, qt_scr)
    acc_scr[...] = jnp.zeros(acc_scr.shape, F32)

    r = lax.broadcasted_iota(jnp.int32, (KV_TILE, tq), 0)
    c = lax.broadcasted_iota(jnp.int32, (KV_TILE, tq), 1)
    causal = ((r % SUBLANES) * SEGMENT + r // SUBLANES) < c
    srow = lax.broadcasted_iota(jnp.int32, (SUBLANES, tq), 0)

    def weight_stage(later, masked):
        new_later = []
        for g in range(streams):
            run = jnp.ones((SUBLANES, tq), F32)
            for v in reversed(range(SEGMENT)):
                rows = slice(v * SUBLANES, (v + 1) * SUBLANES)
                half_tanh = 0.5 * jnp.tanh(s_scr[g, rows, :])
                beta = 0.5 + half_tanh
                rest = 0.5 - half_tanh
                if masked:
                    beta = jnp.where(causal[rows], beta, 0.0)
                    rest = jnp.where(causal[rows], rest, 1.0)
                s_scr[g, rows, :] = beta * run
                run = run * rest
            off = jnp.broadcast_to(later[g], (SUBLANES, tq))
            for s2 in range(1, SUBLANES):
                off = off * jnp.where(srow < s2, run[s2:s2 + 1, :], 1.0)
            off2 = jnp.concatenate([off, off], axis=0)
            for v in range(SEGMENT // 2):
                rows = slice(2 * v * SUBLANES, 2 * (v + 1) * SUBLANES)
                w_scr[g, rows, :] = (s_scr[g, rows, :] * off2).astype(BF16)
            new_later.append((off * run)[0:1, :])
        return tuple(new_later)

    def value_stage(t):
        vt_tile = vt_ref[0, qi - t]
        for g in range(streams):
            acc_scr[g] += jnp.dot(vt_tile[g * HEAD_DIM:(g + 1) * HEAD_DIM, :], w_scr[g],
                                  preferred_element_type=F32)

    _score_stage(k_ref, qi, qt_scr, s_scr)
    later = weight_stage(tuple(jnp.ones((1, tq), F32) for _ in range(streams)), True)
    _score_stage(k_ref, jnp.maximum(qi - 1, 0), qt_scr, s_scr)

    def step(j, later):
        value_stage(j - 1)
        later = weight_stage(later, False)
        _score_stage(k_ref, qi - (j + 1), qt_scr, s_scr)
        return later

    later = lax.fori_loop(1, qi, step, later)

    @pl.when(qi > 0)
    def _():
        value_stage(qi - 1)
        weight_stage(later, False)

    value_stage(qi)
    out_t = acc_scr[...].reshape(streams * HEAD_DIM, tq)
    o_ref[...] = (out_t.T * _silu(g_ref[...].astype(F32))).astype(BF16)


def _sb_attention(y, k_tiles, vt_tiles, *, batch, seq, q0, g0, width):
    nq = seq // Q_TILE
    nkb = seq // KV_TILE
    cols = SB_STREAMS * HEAD_DIM
    return pl.pallas_call(
        functools.partial(_sb_kernel, tq=Q_TILE),
        grid=(batch, width // cols, nq),
        in_specs=[pl.BlockSpec((Q_TILE, cols), lambda b, p, i: (b * nq + i, q0 // cols + p)),
                  pl.BlockSpec((1, nkb, cols // LANES, 2 * KV_TILE, LANES),
                               lambda b, p, i: (b, 0, p, 0, 0)),
                  pl.BlockSpec((1, nkb, cols, KV_TILE), lambda b, p, i: (b, 0, p, 0)),
                  pl.BlockSpec((Q_TILE, cols), lambda b, p, i: (b * nq + i, g0 // cols + p))],
        out_specs=pl.BlockSpec((Q_TILE, cols), lambda b, p, i: (b * nq + i, p)),
        out_shape=jax.ShapeDtypeStruct((batch * seq, width), BF16),
        scratch_shapes=_attention_scratch(SB_STREAMS, Q_TILE, HEAD_DIM),
        compiler_params=_params(3),
        name="stick_breaking_attention",
    )(y, k_tiles, vt_tiles, y)


def _diff_kernel(q_ref, k_ref, vt_ref, g_ref, lq1_ref, lk1_ref, lq2_ref, lk2_ref, sg_ref,
                 o_ref, qt_scr, s_scr, p_scr, acc_scr, *, tq, lambda_init):
    qi = pl.program_id(2)
    streams = s_scr.shape[0]
    _store_transposed_queries(q_ref, qt_scr)
    acc_scr[...] = jnp.zeros(acc_scr.shape, F32)
    r = lax.broadcasted_iota(jnp.int32, (KV_TILE, tq), 0)
    c = lax.broadcasted_iota(jnp.int32, (KV_TILE, tq), 1)
    causal = r <= c
    vdim = 2 * HEAD_DIM

    def softmax_stage(stats, tile_max, masked):
        new = []
        for g in range(streams):
            m, l, _ = stats[g]
            s = s_scr[g]
            if masked:
                s = jnp.where(causal, s, -jnp.inf)
                m_new = jnp.maximum(m, jnp.max(s, axis=0, keepdims=True))
            else:
                m_new = jnp.maximum(m, tile_max[g])
            p = jnp.exp(s - m_new)
            alpha = jnp.exp(m - m_new)
            l = alpha * l + jnp.sum(p, axis=0, keepdims=True)
            p_scr[g] = p.astype(BF16)
            new.append((m_new, l, alpha))
        return tuple(new)

    def value_stage(t, stats):
        vt_tile = vt_ref[0, qi - t]
        for g in range(streams):
            hd = g // 2
            acc_scr[g] = stats[g][2] * acc_scr[g] + jnp.dot(
                vt_tile[hd * vdim:(hd + 1) * vdim, :], p_scr[g], preferred_element_type=F32)

    init = tuple((jnp.full((1, tq), -jnp.inf, F32), jnp.zeros((1, tq), F32),
                  jnp.zeros((1, tq), F32)) for _ in range(streams))
    _score_stage(k_ref, qi, qt_scr, s_scr)
    stats = softmax_stage(init, None, True)
    tile_max = _score_stage(k_ref, jnp.maximum(qi - 1, 0), qt_scr, s_scr, True)

    def step(j, carry):
        stats, tile_max = carry
        value_stage(j - 1, stats)
        stats = softmax_stage(stats, tile_max, False)
        return stats, _score_stage(k_ref, qi - (j + 1), qt_scr, s_scr, True)

    stats, tile_max = lax.fori_loop(1, qi, step, (stats, tile_max))

    def last_step(stats):
        value_stage(qi - 1, stats)
        return softmax_stage(stats, tile_max, False)

    stats = lax.cond(qi > 0, last_step, lambda st: st, stats)
    value_stage(qi, stats)

    lam = (jnp.exp(jnp.sum(lq1_ref[...] * lk1_ref[...], axis=-1, keepdims=True))
           - jnp.exp(jnp.sum(lq2_ref[...] * lk2_ref[...], axis=-1, keepdims=True))
           + lambda_init)
    for hd in range(streams // 2):
        l1, l2 = stats[2 * hd][1], stats[2 * hd + 1][1]
        out_t = acc_scr[2 * hd] / l1 - lam * (acc_scr[2 * hd + 1] / l2)
        ms = jnp.mean(out_t * out_t, axis=0, keepdims=True)
        out = (out_t * lax.rsqrt(ms + EPS)).T
        out = out * sg_ref[...] * (1.0 - lambda_init)
        cols = slice(hd * vdim, (hd + 1) * vdim)
        o_ref[:, cols] = (out * _silu(g_ref[:, cols].astype(F32))).astype(BF16)


def _diff_attention(y, k_tiles, vt_tiles, lq1, lk1, lq2, lk2, subln_g, *, batch, seq, width,
                    lambda_init):
    nq = seq // Q_TILE
    nkb = seq // KV_TILE
    cols = DIFF_STREAMS * HEAD_DIM
    g_col = width // cols
    vec = lambda a: a.reshape(1, -1)
    small = lambda n: pl.BlockSpec((1, n), lambda b, h, i: (0, 0))
    return pl.pallas_call(
        functools.partial(_diff_kernel, tq=Q_TILE, lambda_init=lambda_init),
        grid=(batch, width // cols, nq),
        in_specs=[pl.BlockSpec((Q_TILE, cols), lambda b, h, i: (b * nq + i, h)),
                  pl.BlockSpec((1, nkb, cols // LANES, 2 * KV_TILE, LANES),
                               lambda b, h, i: (b, 0, h, 0, 0)),
                  pl.BlockSpec((1, nkb, cols, KV_TILE), lambda b, h, i: (b, 0, h, 0)),
                  pl.BlockSpec((Q_TILE, cols), lambda b, h, i: (b * nq + i, g_col + h)),
                  small(HEAD_DIM), small(HEAD_DIM), small(HEAD_DIM), small(HEAD_DIM),
                  small(2 * HEAD_DIM)],
        out_specs=pl.BlockSpec((Q_TILE, cols), lambda b, h, i: (b * nq + i, h)),
        out_shape=jax.ShapeDtypeStruct((batch * seq, width), BF16),
        scratch_shapes=_attention_scratch(DIFF_STREAMS, Q_TILE, 2 * HEAD_DIM),
        compiler_params=_params(3),
        name="differential_attention",
    )(y, k_tiles, vt_tiles, y, vec(lq1), vec(lk1), vec(lq2), vec(lk2), vec(subln_g))


def _post_kernel(*refs, final):
    if final:
        (h_ref, ma_ref, mb_ref, p_ref, woa_ref, wob_ref, gp_ref, wg_ref, wp_ref,
         fn_ref, o_ref) = refs
    else:
        h_ref, ma_ref, mb_ref, p_ref, woa_ref, wob_ref, gp_ref, wg_ref, wp_ref, o_ref = refs
    h = (h_ref[...]
         + jnp.dot(ma_ref[...], woa_ref[...], preferred_element_type=F32)
         + jnp.dot(mb_ref[...], wob_ref[...], preferred_element_type=F32))
    ms = jnp.mean(h * h, axis=-1, keepdims=True)
    hn = (h * lax.rsqrt(ms + EPS) * gp_ref[...]).astype(BF16)
    gate = _sigmoid(jnp.dot(hn, wg_ref[...], preferred_element_type=F32))
    ple = jnp.dot(p_ref[...].astype(BF16), wp_ref[...], preferred_element_type=F32)
    h = h + gate * ple
    if final:
        ms = jnp.mean(h * h, axis=-1, keepdims=True)
        h = h * lax.rsqrt(ms + EPS) * fn_ref[...]
    o_ref[...] = h


def _post(h, mix_a, mix_b, b_col, p, w_out, g_ple, w_gate, w_proj, final_norm=None, *, tm=512):
    t, d = h.shape
    half = w_out.shape[0] // 2
    pd = p.shape[1]
    final = final_norm is not None
    const = lambda shape: pl.BlockSpec(shape, lambda i: (0, 0))
    in_specs = [pl.BlockSpec((tm, d), lambda i: (i, 0)),
                pl.BlockSpec((tm, half), lambda i: (i, 0)),
                pl.BlockSpec((tm, half), lambda i: (i, b_col)),
                pl.BlockSpec((tm, pd), lambda i: (i, 0)),
                pl.BlockSpec((half, d), lambda i: (0, 0)),
                pl.BlockSpec((half, d), lambda i: (1, 0)),
                const((1, d)), const((d, d)), const((pd, d))]
    args = [h, mix_a, mix_b, p, w_out, w_out, g_ple.reshape(1, d), w_gate, w_proj]
    if final:
        in_specs.append(const((1, d)))
        args.append(final_norm.reshape(1, d))
    return pl.pallas_call(
        functools.partial(_post_kernel, final=final),
        grid=(t // tm,),
        in_specs=in_specs,
        out_specs=pl.BlockSpec((tm, d), lambda i: (i, 0)),
        out_shape=jax.ShapeDtypeStruct((t, d), F32),
        compiler_params=_params(1),
        name="out_proj_ple_final" if final else "out_proj_ple",
    )(*args)


def _block_diag(w):
    nb, bs, _ = w.shape
    eye = jnp.eye(nb, dtype=w.dtype)
    return jnp.einsum('ncd,nm->ncmd', w, eye).reshape(nb * bs, nb * bs)


def kernel(x, p, positions, norm_mix, norm_ple, w_ple_gate, w_ple_proj, w_in_e, conv_w, conv_b,
           lru_wa, lru_ba, lru_wx, lru_bx, lru_lambda, w_out_e, w_in_o, lam_q1, lam_k1, lam_q2,
           lam_k2, subln_g, w_out_o, final_norm):
    batch, seq, d = x.shape
    depth = norm_mix.shape[0]
    t = batch * seq
    lru_w = conv_w.shape[-1]
    sb_w = (w_in_e.shape[-1] - 2 * lru_w) // 4
    diff_w = w_in_o.shape[-1] // 4

    half = HEAD_DIM // 2
    inv_freq = ROPE_THETA ** (-jnp.arange(0, HEAD_DIM, 2, dtype=F32) / HEAD_DIM)
    ang = positions.astype(F32).reshape(t, 1) * inv_freq
    cos = jnp.tile(jnp.cos(ang), (1, LANES // half))
    sin = jnp.tile(jnp.concatenate([-jnp.sin(ang), jnp.sin(ang)], axis=1), (1, LANES // HEAD_DIM))

    h = x.reshape(t, d)
    for i in range(depth):
        j = i // 2
        last = i == depth - 1
        if i % 2 == 0:
            w = w_in_e[j].astype(BF16)
            q0 = 2 * lru_w
            k0, v0, g0 = q0 + sb_w, q0 + 2 * sb_w, q0 + 3 * sb_w
            w_main = jnp.concatenate([w[:, :k0], w[:, g0:]], axis=1)
            y, k_tiles, vt_tiles = _in_proj(
                h, norm_mix[i], w_main, w[:, k0:v0], w[:, v0:g0], batch=batch, seq=seq,
                q_range=(q0, k0), scale=0.5 * HEAD_DIM ** -0.5, permute_keys=True)
            w_gates = jnp.concatenate([_block_diag(lru_wa[j]), _block_diag(lru_wx[j])], axis=1)
            b_gates = jnp.concatenate([lru_ba[j], lru_bx[j]])
            oa = _lru(y, conv_w[j], conv_b[j], w_gates.astype(BF16), b_gates, lru_lambda[j],
                      batch=batch, seq=seq, width=lru_w)
            ob = _sb_attention(y, k_tiles, vt_tiles, batch=batch, seq=seq,
                               q0=q0, g0=k0, width=sb_w)
            mix_a, mix_b, b_col, w_out = oa, ob, 0, w_out_e[j]
        else:
            lambda_init = 0.8 - 0.6 * math.exp(-0.3 * i)
            w = w_in_o[j].astype(BF16)
            w_main = jnp.concatenate([w[:, :diff_w], w[:, 3 * diff_w:]], axis=1)
            y, k_tiles, vt_tiles = _in_proj(
                h, norm_mix[i], w_main, w[:, diff_w:2 * diff_w], w[:, 2 * diff_w:3 * diff_w],
                cos, sin, batch=batch, seq=seq, q_range=(0, diff_w), scale=HEAD_DIM ** -0.5,
                permute_keys=False)
            o = _diff_attention(y, k_tiles, vt_tiles, lam_q1[j], lam_k1[j], lam_q2[j], lam_k2[j],
                                subln_g[j], batch=batch, seq=seq, width=diff_w,
                                lambda_init=lambda_init)
            mix_a, mix_b, b_col, w_out = o, o, 1, w_out_o[j]
        h = _post(h, mix_a, mix_b, b_col, p[i].reshape(t, -1), w_out.astype(BF16), norm_ple[i],
                  w_ple_gate[i].astype(BF16), w_ple_proj[i].astype(BF16),
                  final_norm if last else None)
    return h.reshape(batch, seq, d)
```

```python
import functools
import math

import jax
import jax.numpy as jnp
from jax import lax
from jax.experimental import pallas as pl
from jax.experimental.pallas import tpu as pltpu

F32 = jnp.float32
BF16 = jnp.bfloat16

EPS = 1e-6
HEAD_DIM = 64
LRU_C = 8.0
CONV_WIDTH = 4
ROPE_THETA = 10000.0

LANES = 128
SUBLANES = 8
KV_TILE = 256
Q_TILE = 256
SEGMENT = KV_TILE // SUBLANES
SB_STREAMS = 8
DIFF_STREAMS = 8
VMEM_LIMIT = 48 * 1024 * 1024


def _params(n_axes):
    return pltpu.CompilerParams(
        dimension_semantics=("arbitrary",) * n_axes, vmem_limit_bytes=VMEM_LIMIT)


def _sigmoid(x):
    return 0.5 + 0.5 * jnp.tanh(0.5 * x)


def _silu(x):
    return x * _sigmoid(x)


def _rotary(y, cos, sin, first_half):
    width = y.shape[1]
    partner = jnp.where(first_half, pltpu.roll(y, width - HEAD_DIM // 2, 1),
                        pltpu.roll(y, HEAD_DIM // 2, 1))
    return y * cos + partner * sin


def _in_proj_kernel(*refs, chunk, rope, q_range, scale, permute_keys, lru_steps):
    x_ref, g_ref, wm_ref, wk_ref, wv_ref = refs[:5]
    if rope:
        cos_ref, sin_ref = refs[5:7]
    if lru_steps:
        wl_ref, cw_ref, cb_ref, wg_ref, bg_ref, lam_ref = refs[5:11]
        m_ref, k_ref, vt_ref, oa_ref, ext_ref, h_ref = refs[11:]
    else:
        m_ref, k_ref, vt_ref = refs[-3:]
    tm = x_ref.shape[0]
    x = x_ref[...]
    ms = jnp.mean(x * x, axis=-1, keepdims=True)
    xn = x * lax.rsqrt(ms + EPS) * g_ref[...]
    xn_bf = xn.astype(BF16)
    if rope:
        reps = chunk // LANES
        cos = jnp.concatenate([cos_ref[...]] * reps, axis=1)
        sin = jnp.concatenate([sin_ref[...]] * reps, axis=1)
        lane = lax.broadcasted_iota(jnp.int32, (tm, chunk), 1)
        first_half = (lane % HEAD_DIM) < (HEAD_DIM // 2)

    lru_tasks = []
    if lru_steps:
        width = oa_ref.shape[1]

        @pl.when(pl.program_id(0) % lru_steps == 0)
        def _():
            ext_ref[0:SUBLANES, :] = jnp.zeros((SUBLANES, width), F32)
            h_ref[...] = jnp.zeros((1, width), F32)

        xa = jnp.dot(xn_bf, wl_ref[:, :width], preferred_element_type=F32)
        ga = jnp.dot(xn_bf, wl_ref[:, width:], preferred_element_type=F32)
        xc = _causal_conv(xa, cw_ref, cb_ref, ext_ref)

    def main_chunk(c):
        lo = c * chunk
        y = jnp.dot(xn_bf, wm_ref[:, lo:lo + chunk], preferred_element_type=F32)
        if q_range[0] <= lo < q_range[1]:
            if rope:
                y = _rotary(y, cos, sin, first_half)
            y = y * scale
        m_ref[:, lo:lo + chunk] = y.astype(BF16)

    n_tiles = tm // KV_TILE
    if permute_keys:
        r = lax.broadcasted_iota(jnp.int32, (KV_TILE, KV_TILE), 0)
        c = lax.broadcasted_iota(jnp.int32, (KV_TILE, KV_TILE), 1)
        perm = jnp.where(c == (r % SUBLANES) * SEGMENT + r // SUBLANES, 1.0, 0.0).astype(BF16)
        xk = jnp.concatenate(
            [jnp.dot(perm, xn_bf[t * KV_TILE:(t + 1) * KV_TILE, :],
                     preferred_element_type=F32).astype(BF16) for t in range(n_tiles)], axis=0)
    else:
        xk = xn_bf

    low = lax.broadcasted_iota(jnp.int32, (KV_TILE, LANES), 1) < HEAD_DIM

    def key_chunk(c):
        kk = jnp.dot(xk, wk_ref[:, c * chunk:(c + 1) * chunk], preferred_element_type=F32)
        if rope:
            kk = _rotary(kk, cos, sin, first_half)
        for t in range(n_tiles):
            for b in range(chunk // LANES):
                blk = kk[t * KV_TILE:(t + 1) * KV_TILE, b * LANES:(b + 1) * LANES]
                nb = c * (chunk // LANES) + b
                k_ref[0, t, nb, 0:KV_TILE, :] = jnp.where(low, blk, 0.0).astype(BF16)
                k_ref[0, t, nb, KV_TILE:2 * KV_TILE, :] = jnp.where(low, 0.0, blk).astype(BF16)
    def value_chunk(c):
        vv = jnp.dot(xk, wv_ref[:, c * chunk:(c + 1) * chunk], preferred_element_type=F32)
        for t in range(n_tiles):
            vt_ref[0, t, c * chunk:(c + 1) * chunk, :] = (
                vv[t * KV_TILE:(t + 1) * KV_TILE, :].T.astype(BF16))

    for c in range(wm_ref.shape[1] // chunk):
        main_chunk(c)
    mxu_tasks = ([functools.partial(key_chunk, c) for c in range(wk_ref.shape[1] // chunk)]
                 + [functools.partial(value_chunk, c) for c in range(wv_ref.shape[1] // chunk)])
    if lru_steps:
        gates = jnp.dot(xc.astype(BF16), wg_ref[...], preferred_element_type=F32) + bg_ref[...]
        ts = tm // 4

        def lru_rows(r0):
            rows = slice(r0, r0 + ts)
            _rg_lru_rows(xc[rows], gates[rows], ga[rows], lam_ref, oa_ref, r0, h_ref)

        lru_tasks = [functools.partial(lru_rows, r0) for r0 in range(0, tm, ts)]
    for n in range(max(len(mxu_tasks), len(lru_tasks))):
        for tasks in (mxu_tasks, lru_tasks):
            if n < len(tasks):
                tasks[n]()


def _in_proj(h, g, w_main, w_k, w_v, cos=None, sin=None, lru=None, *, batch, seq, q_range,
             scale, permute_keys, tm=512, chunk=512):
    t, d = h.shape
    n_main, n_k, n_v = w_main.shape[1], w_k.shape[1], w_v.shape[1]
    rope = cos is not None
    assert not (rope and lru is not None)
    steps = seq // tm
    tiles = tm // KV_TILE
    nkb = seq // KV_TILE
    const = lambda shape: pl.BlockSpec(shape, lambda i: (0, 0))
    in_specs = [pl.BlockSpec((tm, d), lambda i: (i, 0)), const((1, d)),
                const((d, n_main)), const((d, n_k)), const((d, n_v))]
    args = [h, g.reshape(1, d), w_main, w_k, w_v]
    if rope:
        in_specs += [pl.BlockSpec((tm, LANES), lambda i: (i, 0))] * 2
        args += [cos, sin]
    out_specs = [pl.BlockSpec((tm, n_main), lambda i: (i, 0)),
                 pl.BlockSpec((1, tiles, n_k // LANES, 2 * KV_TILE, LANES),
                              lambda i: (i // steps, i % steps, 0, 0, 0)),
                 pl.BlockSpec((1, tiles, n_v, KV_TILE), lambda i: (i // steps, i % steps, 0, 0))]
    out_shape = [jax.ShapeDtypeStruct((t, n_main), BF16),
                 jax.ShapeDtypeStruct((batch, nkb, n_k // LANES, 2 * KV_TILE, LANES), BF16),
                 jax.ShapeDtypeStruct((batch, nkb, n_v, KV_TILE), BF16)]
    scratch = []
    if lru is not None:
        w_lru, conv_w, conv_b, w_gates, b_gates, lam = lru
        width = conv_w.shape[1]
        in_specs += [const((d, 2 * width)), const((CONV_WIDTH, width)), const((1, width)),
                     const((width, 2 * width)), const((1, 2 * width)), const((1, width))]
        args += [w_lru, conv_w, conv_b.reshape(1, width), w_gates,
                 b_gates.reshape(1, 2 * width), lam.reshape(1, width)]
        out_specs.append(pl.BlockSpec((tm, width), lambda i: (i, 0)))
        out_shape.append(jax.ShapeDtypeStruct((t, width), BF16))
        scratch = [pltpu.VMEM((tm + SUBLANES, width), F32), pltpu.VMEM((1, width), F32)]
    return pl.pallas_call(
        functools.partial(_in_proj_kernel, chunk=chunk, rope=rope, q_range=q_range,
                          scale=scale, permute_keys=permute_keys,
                          lru_steps=steps if lru is not None else 0),
        grid=(t // tm,),
        in_specs=in_specs,
        out_specs=out_specs,
        out_shape=out_shape,
        scratch_shapes=scratch,
        compiler_params=_params(1),
        name="in_proj_rope" if rope else "in_proj_lru",
    )(*args)


def _causal_conv(xa, cw_ref, cb_ref, ext_ref):
    ts = xa.shape[0]
    ext_ref[SUBLANES:SUBLANES + ts, :] = xa
    xc = cb_ref[...] + cw_ref[CONV_WIDTH - 1:CONV_WIDTH, :] * xa
    for j in range(1, CONV_WIDTH):
        k = CONV_WIDTH - 1 - j
        xc = xc + cw_ref[k:k + 1, :] * ext_ref[SUBLANES - j:SUBLANES - j + ts, :]
    ext_ref[0:SUBLANES, :] = ext_ref[ts:ts + SUBLANES, :]
    return xc


def _rg_lru_rows(xc, gates, ga, lam_ref, o_ref, row0, h_ref):
    ts, width = xc.shape
    r = _sigmoid(gates[:, :width])
    i = _sigmoid(gates[:, width:])
    lam = lam_ref[...]
    log_sig_lam = jnp.minimum(lam, 0.0) - jnp.log(1.0 + jnp.exp(-jnp.abs(lam)))
    log_a = LRU_C * r * log_sig_lam
    a = jnp.exp(log_a)
    u = jnp.sqrt(1.0 - a * a) * (i * xc)

    groups = ts // SUBLANES
    a = a.reshape(groups, SUBLANES, width)
    u = u.reshape(groups, SUBLANES, width)
    row = lax.broadcasted_iota(jnp.int32, (groups, SUBLANES, width), 1)
    step = 1
    while step < SUBLANES:
        valid = row >= step
        a_prev = pltpu.roll(a, step, 1)
        u_prev = pltpu.roll(u, step, 1)
        u = jnp.where(valid, a * u_prev + u, u)
        a = jnp.where(valid, a * a_prev, a)
        step *= 2
    gate = _silu(ga)
    state = h_ref[...]
    for pair in range(ts // (2 * SUBLANES)):
        hs = []
        for grp in (2 * pair, 2 * pair + 1):
            h = u[grp] + a[grp] * state
            state = h[SUBLANES - 1:SUBLANES, :]
            hs.append(h)
        rows = slice(2 * pair * SUBLANES, 2 * (pair + 1) * SUBLANES)
        o_ref[row0 + rows.start:row0 + rows.stop, :] = (
            jnp.concatenate(hs, axis=0) * gate[rows]).astype(BF16)
    h_ref[...] = state


def _store_transposed_queries(q_ref, qt_scr):
    for blk in range(qt_scr.shape[0]):
        qt_scr[blk] = q_ref[:, blk * LANES:(blk + 1) * LANES].astype(F32).T.astype(BF16)


def _score_stage(k_ref, kb, qt_scr, s_scr, with_max=False):
    tile_max = []
    for blk in range(qt_scr.shape[0]):
        both = jnp.dot(k_ref[0, kb, blk], qt_scr[blk], preferred_element_type=F32)
        for half in range(2):
            s = both[half * KV_TILE:(half + 1) * KV_TILE]
            s_scr[2 * blk + half] = s
            if with_max:
                tile_max.append(jnp.max(s, axis=0, keepdims=True))
    return tuple(tile_max)


def _attention_scratch(streams, tq, value_rows):
    return [pltpu.VMEM((streams // 2, LANES, tq), BF16),
            pltpu.VMEM((streams, KV_TILE, tq), F32),
            pltpu.VMEM((streams, KV_TILE, tq), BF16),
            pltpu.VMEM((streams, value_rows, tq), F32)]


def _sb_kernel(q_ref, k_ref, vt_ref, g_ref, o_ref, qt_scr, s_scr, w_scr, acc_scr, *, tq):
    qi = pl.program_id(2)
    streams = s_scr.shape[0]
    _store_transposed_queries(q_ref, qt_scr)
    acc_scr[...] = jnp.zeros(acc_scr.shape, F32)

    r = lax.broadcasted_iota(jnp.int32, (KV_TILE, tq), 0)
    c = lax.broadcasted_iota(jnp.int32, (KV_TILE, tq), 1)
    causal = ((r % SUBLANES) * SEGMENT + r // SUBLANES) < c
    srow = lax.broadcasted_iota(jnp.int32, (SUBLANES, tq), 0)

    def weight_stage(later, masked):
        new_later = []
        for g in range(streams):
            run = jnp.ones((SUBLANES, tq), F32)
            for v in reversed(range(SEGMENT)):
                rows = slice(v * SUBLANES, (v + 1) * SUBLANES)
                half_tanh = 0.5 * jnp.tanh(s_scr[g, rows, :])
                beta = 0.5 + half_tanh
                rest = 0.5 - half_tanh
                if masked:
                    beta = jnp.where(causal[rows], beta, 0.0)
                    rest = jnp.where(causal[rows], rest, 1.0)
                s_scr[g, rows, :] = beta * run
                run = run * rest
            off = jnp.broadcast_to(later[g], (SUBLANES, tq))
            for s2 in range(1, SUBLANES):
                off = off * jnp.where(srow < s2, run[s2:s2 + 1, :], 1.0)
            off2 = jnp.concatenate([off, off], axis=0)
            for v in range(SEGMENT // 2):
                rows = slice(2 * v * SUBLANES, 2 * (v + 1) * SUBLANES)
                w_scr[g, rows, :] = (s_scr[g, rows, :] * off2).astype(BF16)
            new_later.append((off * run)[0:1, :])
        return tuple(new_later)

    def value_stage(t):
        vt_tile = vt_ref[0, qi - t]
        for g in range(streams):
            acc_scr[g] += jnp.dot(vt_tile[g * HEAD_DIM:(g + 1) * HEAD_DIM, :], w_scr[g],
                                  preferred_element_type=F32)

    _score_stage(k_ref, qi, qt_scr, s_scr)
    later = weight_stage(tuple(jnp.ones((1, tq), F32) for _ in range(streams)), True)
    _score_stage(k_ref, jnp.maximum(qi - 1, 0), qt_scr, s_scr)

    def step(j, later):
        value_stage(j - 1)
        later = weight_stage(later, False)
        _score_stage(k_ref, qi - (j + 1), qt_scr, s_scr)
        return later

    later = lax.fori_loop(1, qi, step, later)

    @pl.when(qi > 0)
    def _():
        value_stage(qi - 1)
        weight_stage(later, False)

    value_stage(qi)
    out_t = acc_scr[...].reshape(streams * HEAD_DIM, tq)
    o_ref[...] = (out_t.T * _silu(g_ref[...].astype(F32))).astype(BF16)


def _sb_attention(y, k_tiles, vt_tiles, *, batch, seq, q0, g0, width):
    nq = seq // Q_TILE
    nkb = seq // KV_TILE
    cols = SB_STREAMS * HEAD_DIM
    return pl.pallas_call(
        functools.partial(_sb_kernel, tq=Q_TILE),
        grid=(batch, width // cols, nq),
        in_specs=[pl.BlockSpec((Q_TILE, cols), lambda b, p, i: (b * nq + i, q0 // cols + p)),
                  pl.BlockSpec((1, nkb, cols // LANES, 2 * KV_TILE, LANES),
                               lambda b, p, i: (b, 0, p, 0, 0)),
                  pl.BlockSpec((1, nkb, cols, KV_TILE), lambda b, p, i: (b, 0, p, 0)),
                  pl.BlockSpec((Q_TILE, cols), lambda b, p, i: (b * nq + i, g0 // cols + p))],
        out_specs=pl.BlockSpec((Q_TILE, cols), lambda b, p, i: (b * nq + i, p)),
        out_shape=jax.ShapeDtypeStruct((batch * seq, width), BF16),
        scratch_shapes=_attention_scratch(SB_STREAMS, Q_TILE, HEAD_DIM),
        compiler_params=_params(3),
        name="stick_breaking_attention",
    )(y, k_tiles, vt_tiles, y)


def _diff_kernel(q_ref, k_ref, vt_ref, g_ref, lq1_ref, lk1_ref, lq2_ref, lk2_ref, sg_ref,
                 o_ref, qt_scr, s_scr, p_scr, acc_scr, *, tq, lambda_init):
    qi = pl.program_id(2)
    streams = s_scr.shape[0]
    _store_transposed_queries(q_ref, qt_scr)
    acc_scr[...] = jnp.zeros(acc_scr.shape, F32)
    r = lax.broadcasted_iota(jnp.int32, (KV_TILE, tq), 0)
    c = lax.broadcasted_iota(jnp.int32, (KV_TILE, tq), 1)
    causal = r <= c
    vdim = 2 * HEAD_DIM

    def softmax_stage(stats, tile_max, masked):
        new = []
        for g in range(streams):
            m, l, _ = stats[g]
            s = s_scr[g]
            if masked:
                s = jnp.where(causal, s, -jnp.inf)
                m_new = jnp.maximum(m, jnp.max(s, axis=0, keepdims=True))
            else:
                m_new = jnp.maximum(m, tile_max[g])
            p = jnp.exp(s - m_new)
            alpha = jnp.exp(m - m_new)
            l = alpha * l + jnp.sum(p, axis=0, keepdims=True)
            p_scr[g] = p.astype(BF16)
            new.append((m_new, l, alpha))
        return tuple(new)

    def value_stage(t, stats):
        vt_tile = vt_ref[0, qi - t]
        for g in range(streams):
            hd = g // 2
            acc_scr[g] = stats[g][2] * acc_scr[g] + jnp.dot(
                vt_tile[hd * vdim:(hd + 1) * vdim, :], p_scr[g], preferred_element_type=F32)

    init = tuple((jnp.full((1, tq), -jnp.inf, F32), jnp.zeros((1, tq), F32),
                  jnp.zeros((1, tq), F32)) for _ in range(streams))
    _score_stage(k_ref, qi, qt_scr, s_scr)
    stats = softmax_stage(init, None, True)
    tile_max = _score_stage(k_ref, jnp.maximum(qi - 1, 0), qt_scr, s_scr, True)

    def step(j, carry):
        stats, tile_max = carry
        value_stage(j - 1, stats)
        stats = softmax_stage(stats, tile_max, False)
        return stats, _score_stage(k_ref, qi - (j + 1), qt_scr, s_scr, True)

    stats, tile_max = lax.fori_loop(1, qi, step, (stats, tile_max))

    def last_step(stats):
        value_stage(qi - 1, stats)
        return softmax_stage(stats, tile_max, False)

    stats = lax.cond(qi > 0, last_step, lambda st: st, stats)
    value_stage(qi, stats)

    lam = (jnp.exp(jnp.sum(lq1_ref[...] * lk1_ref[...], axis=-1, keepdims=True))
           - jnp.exp(jnp.sum(lq2_ref[...] * lk2_ref[...], axis=-1, keepdims=True))
           + lambda_init)
    for hd in range(streams // 2):
        l1, l2 = stats[2 * hd][1], stats[2 * hd + 1][1]
        out_t = acc_scr[2 * hd] / l1 - lam * (acc_scr[2 * hd + 1] / l2)
        ms = jnp.mean(out_t * out_t, axis=0, keepdims=True)
        out = (out_t * lax.rsqrt(ms + EPS)).T
        out = out * sg_ref[...] * (1.0 - lambda_init)
        cols = slice(hd * vdim, (hd + 1) * vdim)
        o_ref[:, cols] = (out * _silu(g_ref[:, cols].astype(F32))).astype(BF16)


def _diff_attention(y, k_tiles, vt_tiles, lq1, lk1, lq2, lk2, subln_g, *, batch, seq, width,
                    lambda_init):
    nq = seq // Q_TILE
    nkb = seq // KV_TILE
    cols = DIFF_STREAMS * HEAD_DIM
    g_col = width // cols
    vec = lambda a: a.reshape(1, -1)
    small = lambda n: pl.BlockSpec((1, n), lambda b, h, i: (0, 0))
    return pl.pallas_call(
        functools.partial(_diff_kernel, tq=Q_TILE, lambda_init=lambda_init),
        grid=(batch, width // cols, nq),
        in_specs=[pl.BlockSpec((Q_TILE, cols), lambda b, h, i: (b * nq + i, h)),
                  pl.BlockSpec((1, nkb, cols // LANES, 2 * KV_TILE, LANES),
                               lambda b, h, i: (b, 0, h, 0, 0)),
                  pl.BlockSpec((1, nkb, cols, KV_TILE), lambda b, h, i: (b, 0, h, 0)),
                  pl.BlockSpec((Q_TILE, cols), lambda b, h, i: (b * nq + i, g_col + h)),
                  small(HEAD_DIM), small(HEAD_DIM), small(HEAD_DIM), small(HEAD_DIM),
                  small(2 * HEAD_DIM)],
        out_specs=pl.BlockSpec((Q_TILE, cols), lambda b, h, i: (b * nq + i, h)),
        out_shape=jax.ShapeDtypeStruct((batch * seq, width), BF16),
        scratch_shapes=_attention_scratch(DIFF_STREAMS, Q_TILE, 2 * HEAD_DIM),
        compiler_params=_params(3),
        name="differential_attention",
    )(y, k_tiles, vt_tiles, y, vec(lq1), vec(lk1), vec(lq2), vec(lk2), vec(subln_g))


def _post_kernel(*refs, final):
    if final:
        (h_ref, ma_ref, mb_ref, p_ref, woa_ref, wob_ref, gp_ref, wg_ref, wp_ref,
         fn_ref, o_ref) = refs
    else:
        h_ref, ma_ref, mb_ref, p_ref, woa_ref, wob_ref, gp_ref, wg_ref, wp_ref, o_ref = refs
    h = (h_ref[...]
         + jnp.dot(ma_ref[...], woa_ref[...], preferred_element_type=F32)
         + jnp.dot(mb_ref[...], wob_ref[...], preferred_element_type=F32))
    ms = jnp.mean(h * h, axis=-1, keepdims=True)
    hn = (h * lax.rsqrt(ms + EPS) * gp_ref[...]).astype(BF16)
    gate = _sigmoid(jnp.dot(hn, wg_ref[...], preferred_element_type=F32))
    ple = jnp.dot(p_ref[...].astype(BF16), wp_ref[...], preferred_element_type=F32)
    h = h + gate * ple
    if final:
        ms = jnp.mean(h * h, axis=-1, keepdims=True)
        h = h * lax.rsqrt(ms + EPS) * fn_ref[...]
    o_ref[...] = h


def _post(h, mix_a, mix_b, b_col, p, w_out, g_ple, w_gate, w_proj, final_norm=None, *, tm=512):
    t, d = h.shape
    half = w_out.shape[0] // 2
    pd = p.shape[1]
    final = final_norm is not None
    const = lambda shape: pl.BlockSpec(shape, lambda i: (0, 0))
    in_specs = [pl.BlockSpec((tm, d), lambda i: (i, 0)),
                pl.BlockSpec((tm, half), lambda i: (i, 0)),
                pl.BlockSpec((tm, half), lambda i: (i, b_col)),
                pl.BlockSpec((tm, pd), lambda i: (i, 0)),
                pl.BlockSpec((half, d), lambda i: (0, 0)),
                pl.BlockSpec((half, d), lambda i: (1, 0)),
                const((1, d)), const((d, d)), const((pd, d))]
    args = [h, mix_a, mix_b, p, w_out, w_out, g_ple.reshape(1, d), w_gate, w_proj]
    if final:
        in_specs.append(const((1, d)))
        args.append(final_norm.reshape(1, d))
    return pl.pallas_call(
        functools.partial(_post_kernel, final=final),
        grid=(t // tm,),
        in_specs=in_specs,
        out_specs=pl.BlockSpec((tm, d), lambda i: (i, 0)),
        out_shape=jax.ShapeDtypeStruct((t, d), F32),
        compiler_params=_params(1),
        name="out_proj_ple_final" if final else "out_proj_ple",
    )(*args)


def _block_diag(w):
    nb, bs, _ = w.shape
    eye = jnp.eye(nb, dtype=w.dtype)
    return jnp.einsum('ncd,nm->ncmd', w, eye).reshape(nb * bs, nb * bs)


def kernel(x, p, positions, norm_mix, norm_ple, w_ple_gate, w_ple_proj, w_in_e, conv_w, conv_b,
           lru_wa, lru_ba, lru_wx, lru_bx, lru_lambda, w_out_e, w_in_o, lam_q1, lam_k1, lam_q2,
           lam_k2, subln_g, w_out_o, final_norm):
    batch, seq, d = x.shape
    depth = norm_mix.shape[0]
    t = batch * seq
    lru_w = conv_w.shape[-1]
    sb_w = (w_in_e.shape[-1] - 2 * lru_w) // 4
    diff_w = w_in_o.shape[-1] // 4

    half = HEAD_DIM // 2
    inv_freq = ROPE_THETA ** (-jnp.arange(0, HEAD_DIM, 2, dtype=F32) / HEAD_DIM)
    ang = positions.astype(F32).reshape(t, 1) * inv_freq
    cos = jnp.tile(jnp.cos(ang), (1, LANES // half))
    sin = jnp.tile(jnp.concatenate([-jnp.sin(ang), jnp.sin(ang)], axis=1), (1, LANES // HEAD_DIM))

    h = x.reshape(t, d)
    for i in range(depth):
        j = i // 2
        last = i == depth - 1
        if i % 2 == 0:
            w = w_in_e[j].astype(BF16)
            q0 = 2 * lru_w
            k0, v0, g0 = q0 + sb_w, q0 + 2 * sb_w, q0 + 3 * sb_w
            w_main = jnp.concatenate([w[:, q0:k0], w[:, g0:]], axis=1)
            w_gates = jnp.concatenate([_block_diag(lru_wa[j]), _block_diag(lru_wx[j])], axis=1)
            b_gates = jnp.concatenate([lru_ba[j], lru_bx[j]])
            lru = (w[:, :q0], conv_w[j], conv_b[j], w_gates.astype(BF16), b_gates, lru_lambda[j])
            y, k_tiles, vt_tiles, oa = _in_proj(
                h, norm_mix[i], w_main, w[:, k0:v0], w[:, v0:g0], lru=lru, batch=batch, seq=seq,
                q_range=(0, sb_w), scale=0.5 * HEAD_DIM ** -0.5, permute_keys=True)
            ob = _sb_attention(y, k_tiles, vt_tiles, batch=batch, seq=seq,
                               q0=0, g0=sb_w, width=sb_w)
            mix_a, mix_b, b_col, w_out = oa, ob, 0, w_out_e[j]
        else:
            lambda_init = 0.8 - 0.6 * math.exp(-0.3 * i)
            w = w_in_o[j].astype(BF16)
            w_main = jnp.concatenate([w[:, :diff_w], w[:, 3 * diff_w:]], axis=1)
            y, k_tiles, vt_tiles = _in_proj(
                h, norm_mix[i], w_main, w[:, diff_w:2 * diff_w], w[:, 2 * diff_w:3 * diff_w],
                cos, sin, batch=batch, seq=seq, q_range=(0, diff_w), scale=HEAD_DIM ** -0.5,
                permute_keys=False)
            o = _diff_attention(y, k_tiles, vt_tiles, lam_q1[j], lam_k1[j], lam_q2[j], lam_k2[j],
                                subln_g[j], batch=batch, seq=seq, width=diff_w,
                                lambda_init=lambda_init)
            mix_a, mix_b, b_col, w_out = o, o, 1, w_out_o[j]
        h = _post(h, mix_a, mix_b, b_col, p[i].reshape(t, -1), w_out.astype(BF16), norm_ple[i],
                  w_ple_gate[i].astype(BF16), w_ple_proj[i].astype(BF16),
                  final_norm if last else None)
    return h.reshape(batch, seq, d)
```

```python
import functools
import math

import jax
import jax.numpy as jnp
from jax import lax
from jax.experimental import pallas as pl
from jax.experimental.pallas import tpu as pltpu

F32 = jnp.float32
BF16 = jnp.bfloat16

EPS = 1e-6
HEAD_DIM = 64
LRU_C = 8.0
CONV_WIDTH = 4
ROPE_THETA = 10000.0

LANES = 128
SUBLANES = 8
KV_TILE = 256
Q_TILE = 256
SEGMENT = KV_TILE // SUBLANES
SB_STREAMS = 8
DIFF_STREAMS = 8
VMEM_LIMIT = 48 * 1024 * 1024


def _params(n_axes):
    return pltpu.CompilerParams(
        dimension_semantics=("arbitrary",) * n_axes, vmem_limit_bytes=VMEM_LIMIT)


def _sigmoid(x):
    return 0.5 + 0.5 * jnp.tanh(0.5 * x)


def _silu(x):
    return x * _sigmoid(x)


def _rotary(y, cos, sin, first_half):
    width = y.shape[1]
    partner = jnp.where(first_half, pltpu.roll(y, width - HEAD_DIM // 2, 1),
                        pltpu.roll(y, HEAD_DIM // 2, 1))
    return y * cos + partner * sin


def _in_proj_kernel(*refs, chunk, rope, q_range, scale, permute_keys, lru_steps):
    x_ref, g_ref, wm_ref, wk_ref, wv_ref = refs[:5]
    if rope:
        cos_ref, sin_ref = refs[5:7]
    if lru_steps:
        wl_ref, cw_ref, cb_ref, wg_ref, bg_ref, lam_ref = refs[5:11]
        m_ref, k_ref, vt_ref, oa_ref, ext_ref, h_ref = refs[11:]
    else:
        m_ref, k_ref, vt_ref = refs[-3:]
    tm = x_ref.shape[0]
    x = x_ref[...]
    ms = jnp.mean(x * x, axis=-1, keepdims=True)
    xn = x * lax.rsqrt(ms + EPS) * g_ref[...]
    xn_bf = xn.astype(BF16)
    if rope:
        reps = chunk // LANES
        cos = jnp.concatenate([cos_ref[...]] * reps, axis=1)
        sin = jnp.concatenate([sin_ref[...]] * reps, axis=1)
        lane = lax.broadcasted_iota(jnp.int32, (tm, chunk), 1)
        first_half = (lane % HEAD_DIM) < (HEAD_DIM // 2)

    lru_tasks = []
    if lru_steps:
        width = oa_ref.shape[1]

        @pl.when(pl.program_id(0) % lru_steps == 0)
        def _():
            ext_ref[0:SUBLANES, :] = jnp.zeros((SUBLANES, width), F32)
            h_ref[...] = jnp.zeros((1, width), F32)

        xa = jnp.dot(xn_bf, wl_ref[:, :width], preferred_element_type=F32)
        ga = jnp.dot(xn_bf, wl_ref[:, width:], preferred_element_type=F32)
        xc = _causal_conv(xa, cw_ref, cb_ref, ext_ref)

    def main_chunk(c):
        lo = c * chunk
        y = jnp.dot(xn_bf, wm_ref[:, lo:lo + chunk], preferred_element_type=F32)
        if q_range[0] <= lo < q_range[1]:
            if rope:
                y = _rotary(y, cos, sin, first_half)
            y = y * scale
        m_ref[:, lo:lo + chunk] = y.astype(BF16)

    n_tiles = tm // KV_TILE
    if permute_keys:
        r = lax.broadcasted_iota(jnp.int32, (KV_TILE, KV_TILE), 0)
        c = lax.broadcasted_iota(jnp.int32, (KV_TILE, KV_TILE), 1)
        perm = jnp.where(c == (r % SUBLANES) * SEGMENT + r // SUBLANES, 1.0, 0.0).astype(BF16)
        xk = jnp.concatenate(
            [jnp.dot(perm, xn_bf[t * KV_TILE:(t + 1) * KV_TILE, :],
                     preferred_element_type=F32).astype(BF16) for t in range(n_tiles)], axis=0)
    else:
        xk = xn_bf

    low = lax.broadcasted_iota(jnp.int32, (KV_TILE, LANES), 1) < HEAD_DIM

    def key_chunk(c):
        kk = jnp.dot(xk, wk_ref[:, c * chunk:(c + 1) * chunk], preferred_element_type=F32)
        if rope:
            kk = _rotary(kk, cos, sin, first_half)
        for t in range(n_tiles):
            for b in range(chunk // LANES):
                blk = kk[t * KV_TILE:(t + 1) * KV_TILE, b * LANES:(b + 1) * LANES]
                nb = c * (chunk // LANES) + b
                k_ref[0, t, nb, 0:KV_TILE, :] = jnp.where(low, blk, 0.0).astype(BF16)
                k_ref[0, t, nb, KV_TILE:2 * KV_TILE, :] = jnp.where(low, 0.0, blk).astype(BF16)
    def value_chunk(c):
        vv = jnp.dot(xk, wv_ref[:, c * chunk:(c + 1) * chunk], preferred_element_type=F32)
        for t in range(n_tiles):
            vt_ref[0, t, c * chunk:(c + 1) * chunk, :] = (
                vv[t * KV_TILE:(t + 1) * KV_TILE, :].T.astype(BF16))

    for c in range(wm_ref.shape[1] // chunk):
        main_chunk(c)
    mxu_tasks = ([functools.partial(key_chunk, c) for c in range(wk_ref.shape[1] // chunk)]
                 + [functools.partial(value_chunk, c) for c in range(wv_ref.shape[1] // chunk)])
    if lru_steps:
        gates = jnp.dot(xc.astype(BF16), wg_ref[...], preferred_element_type=F32) + bg_ref[...]
        ts = tm // 4

        def lru_rows(r0):
            rows = slice(r0, r0 + ts)
            _rg_lru_rows(xc[rows], gates[rows], ga[rows], lam_ref, oa_ref, r0, h_ref)

        lru_tasks = [functools.partial(lru_rows, r0) for r0 in range(0, tm, ts)]
    for n in range(max(len(mxu_tasks), len(lru_tasks))):
        for tasks in (mxu_tasks, lru_tasks):
            if n < len(tasks):
                tasks[n]()


def _in_proj(h, g, w_main, w_k, w_v, cos=None, sin=None, lru=None, *, batch, seq, q_range,
             scale, permute_keys, tm=512, chunk=512):
    t, d = h.shape
    n_main, n_k, n_v = w_main.shape[1], w_k.shape[1], w_v.shape[1]
    rope = cos is not None
    assert not (rope and lru is not None)
    steps = seq // tm
    tiles = tm // KV_TILE
    nkb = seq // KV_TILE
    const = lambda shape: pl.BlockSpec(shape, lambda i: (0, 0))
    in_specs = [pl.BlockSpec((tm, d), lambda i: (i, 0)), const((1, d)),
                const((d, n_main)), const((d, n_k)), const((d, n_v))]
    args = [h, g.reshape(1, d), w_main, w_k, w_v]
    if rope:
        in_specs += [pl.BlockSpec((tm, LANES), lambda i: (i, 0))] * 2
        args += [cos, sin]
    out_specs = [pl.BlockSpec((tm, n_main), lambda i: (i, 0)),
                 pl.BlockSpec((1, tiles, n_k // LANES, 2 * KV_TILE, LANES),
                              lambda i: (i // steps, i % steps, 0, 0, 0)),
                 pl.BlockSpec((1, tiles, n_v, KV_TILE), lambda i: (i // steps, i % steps, 0, 0))]
    out_shape = [jax.ShapeDtypeStruct((t, n_main), BF16),
                 jax.ShapeDtypeStruct((batch, nkb, n_k // LANES, 2 * KV_TILE, LANES), BF16),
                 jax.ShapeDtypeStruct((batch, nkb, n_v, KV_TILE), BF16)]
    scratch = []
    if lru is not None:
        w_lru, conv_w, conv_b, w_gates, b_gates, lam = lru
        width = conv_w.shape[1]
        in_specs += [const((d, 2 * width)), const((CONV_WIDTH, width)), const((1, width)),
                     const((width, 2 * width)), const((1, 2 * width)), const((1, width))]
        args += [w_lru, conv_w, conv_b.reshape(1, width), w_gates,
                 b_gates.reshape(1, 2 * width), lam.reshape(1, width)]
        out_specs.append(pl.BlockSpec((tm, width), lambda i: (i, 0)))
        out_shape.append(jax.ShapeDtypeStruct((t, width), BF16))
        scratch = [pltpu.VMEM((tm + SUBLANES, width), F32), pltpu.VMEM((1, width), F32)]
    return pl.pallas_call(
        functools.partial(_in_proj_kernel, chunk=chunk, rope=rope, q_range=q_range,
                          scale=scale, permute_keys=permute_keys,
                          lru_steps=steps if lru is not None else 0),
        grid=(t // tm,),
        in_specs=in_specs,
        out_specs=out_specs,
        out_shape=out_shape,
        scratch_shapes=scratch,
        compiler_params=_params(1),
        name="in_proj_rope" if rope else "in_proj_lru",
    )(*args)


def _causal_conv(xa, cw_ref, cb_ref, ext_ref):
    ts = xa.shape[0]
    ext_ref[SUBLANES:SUBLANES + ts, :] = xa
    xc = cb_ref[...] + cw_ref[CONV_WIDTH - 1:CONV_WIDTH, :] * xa
    for j in range(1, CONV_WIDTH):
        k = CONV_WIDTH - 1 - j
        xc = xc + cw_ref[k:k + 1, :] * ext_ref[SUBLANES - j:SUBLANES - j + ts, :]
    ext_ref[0:SUBLANES, :] = ext_ref[ts:ts + SUBLANES, :]
    return xc


def _rg_lru_rows(xc, gates, ga, lam_ref, o_ref, row0, h_ref):
    ts, width = xc.shape
    r = _sigmoid(gates[:, :width])
    i = _sigmoid(gates[:, width:])
    lam = lam_ref[...]
    log_sig_lam = jnp.minimum(lam, 0.0) - jnp.log(1.0 + jnp.exp(-jnp.abs(lam)))
    log_a = LRU_C * r * log_sig_lam
    a = jnp.exp(log_a)
    u = jnp.sqrt(1.0 - a * a) * (i * xc)

    groups = ts // SUBLANES
    a = a.reshape(groups, SUBLANES, width)
    u = u.reshape(groups, SUBLANES, width)
    row = lax.broadcasted_iota(jnp.int32, (groups, SUBLANES, width), 1)
    step = 1
    while step < SUBLANES:
        valid = row >= step
        a_prev = pltpu.roll(a, step, 1)
        u_prev = pltpu.roll(u, step, 1)
        u = jnp.where(valid, a * u_prev + u, u)
        a = jnp.where(valid, a * a_prev, a)
        step *= 2
    gate = _silu(ga)
    state = h_ref[...]
    for pair in range(ts // (2 * SUBLANES)):
        hs = []
        for grp in (2 * pair, 2 * pair + 1):
            h = u[grp] + a[grp] * state
            state = h[SUBLANES - 1:SUBLANES, :]
            hs.append(h)
        rows = slice(2 * pair * SUBLANES, 2 * (pair + 1) * SUBLANES)
        o_ref[row0 + rows.start:row0 + rows.stop, :] = (
            jnp.concatenate(hs, axis=0) * gate[rows]).astype(BF16)
    h_ref[...] = state


def _store_transposed_queries(q_ref, qt_scr):
    for blk in range(qt_scr.shape[0]):
        qt_scr[blk] = q_ref[:, blk * LANES:(blk + 1) * LANES].astype(F32).T.astype(BF16)


def _score_stage(k_ref, kb, qt_scr, s_scr, with_max=False):
    tile_max = []
    for blk in range(qt_scr.shape[0]):
        both = jnp.dot(k_ref[0, kb, blk], qt_scr[blk], preferred_element_type=F32)
        for half in range(2):
            s = both[half * KV_TILE:(half + 1) * KV_TILE]
            s_scr[2 * blk + half] = s
            if with_max:
                tile_max.append(jnp.max(s, axis=0, keepdims=True))
    return tuple(tile_max)


def _attention_scratch(streams, tq, value_rows):
    return [pltpu.VMEM((streams // 2, LANES, tq), BF16),
            pltpu.VMEM((streams, KV_TILE, tq), F32),
            pltpu.VMEM((streams, KV_TILE, tq), BF16),
            pltpu.VMEM((streams, value_rows, tq), F32)]


def _sb_kernel(q_ref, k_ref, vt_ref, g_ref, o_ref, qt_scr, s_scr, w_scr, acc_scr, *, tq):
    qi = pl.program_id(2)
    streams = s_scr.shape[0]
    _store_transposed_queries(q_ref, qt_scr)
    acc_scr[...] = jnp.zeros(acc_scr.shape, F32)

    r = lax.broadcasted_iota(jnp.int32, (KV_TILE, tq), 0)
    c = lax.broadcasted_iota(jnp.int32, (KV_TILE, tq), 1)
    causal = ((r % SUBLANES) * SEGMENT + r // SUBLANES) < c
    srow = lax.broadcasted_iota(jnp.int32, (SUBLANES, tq), 0)

    def weight_stage(later, masked):
        new_later = []
        for g in range(streams):
            run = jnp.ones((SUBLANES, tq), F32)
            for v in reversed(range(SEGMENT)):
                rows = slice(v * SUBLANES, (v + 1) * SUBLANES)
                half_tanh = 0.5 * jnp.tanh(s_scr[g, rows, :])
                beta = 0.5 + half_tanh
                rest = 0.5 - half_tanh
                if masked:
                    beta = jnp.where(causal[rows], beta, 0.0)
                    rest = jnp.where(causal[rows], rest, 1.0)
                s_scr[g, rows, :] = beta * run
                run = run * rest
            off = jnp.broadcast_to(later[g], (SUBLANES, tq))
            for s2 in range(1, SUBLANES):
                off = off * jnp.where(srow < s2, run[s2:s2 + 1, :], 1.0)
            off2 = jnp.concatenate([off, off], axis=0)
            for v in range(SEGMENT // 2):
                rows = slice(2 * v * SUBLANES, 2 * (v + 1) * SUBLANES)
                w_scr[g, rows, :] = (s_scr[g, rows, :] * off2).astype(BF16)
            new_later.append((off * run)[0:1, :])
        return tuple(new_later)

    def value_stage(t):
        vt_tile = vt_ref[0, qi - t]
        for g in range(streams):
            acc_scr[g] += jnp.dot(vt_tile[g * HEAD_DIM:(g + 1) * HEAD_DIM, :], w_scr[g],
                                  preferred_element_type=F32)

    _score_stage(k_ref, qi, qt_scr, s_scr)
    later = weight_stage(tuple(jnp.ones((1, tq), F32) for _ in range(streams)), True)
    _score_stage(k_ref, jnp.maximum(qi - 1, 0), qt_scr, s_scr)

    def step(j, later):
        value_stage(j - 1)
        later = weight_stage(later, False)
        _score_stage(k_ref, qi - (j + 1), qt_scr, s_scr)
        return later

    later = lax.fori_loop(1, qi, step, later)

    @pl.when(qi > 0)
    def _():
        value_stage(qi - 1)
        weight_stage(later, False)

    value_stage(qi)
    out_t = acc_scr[...].reshape(streams * HEAD_DIM, tq)
    o_ref[...] = (out_t.T * _silu(g_ref[...].astype(F32))).astype(BF16)


def _sb_attention(y, k_tiles, vt_tiles, *, batch, seq, q0, g0, width):
    nq = seq // Q_TILE
    nkb = seq // KV_TILE
    cols = SB_STREAMS * HEAD_DIM
    return pl.pallas_call(
        functools.partial(_sb_kernel, tq=Q_TILE),
        grid=(batch, width // cols, nq),
        in_specs=[pl.BlockSpec((Q_TILE, cols), lambda b, p, i: (b * nq + i, q0 // cols + p)),
                  pl.BlockSpec((1, nkb, cols // LANES, 2 * KV_TILE, LANES),
                               lambda b, p, i: (b, 0, p, 0, 0)),
                  pl.BlockSpec((1, nkb, cols, KV_TILE), lambda b, p, i: (b, 0, p, 0)),
                  pl.BlockSpec((Q_TILE, cols), lambda b, p, i: (b * nq + i, g0 // cols + p))],
        out_specs=pl.BlockSpec((Q_TILE, cols), lambda b, p, i: (b * nq + i, p)),
        out_shape=jax.ShapeDtypeStruct((batch * seq, width), BF16),
        scratch_shapes=_attention_scratch(SB_STREAMS, Q_TILE, HEAD_DIM),
        compiler_params=_params(3),
        name="stick_breaking_attention",
    )(y, k_tiles, vt_tiles, y)


def _diff_kernel(q_ref, k_ref, vt_ref, g_ref, lq1_ref, lk1_ref, lq2_ref, lk2_ref, sg_ref,
                 o_ref, qt_scr, s_scr, p_scr, acc_scr, *, tq, lambda_init):
    qi = pl.program_id(2)
    streams = s_scr.shape[0]
    _store_transposed_queries(q_ref, qt_scr)
    acc_scr[...] = jnp.zeros(acc_scr.shape, F32)
    r = lax.broadcasted_iota(jnp.int32, (KV_TILE, tq), 0)
    c = lax.broadcasted_iota(jnp.int32, (KV_TILE, tq), 1)
    causal = r <= c
    vdim = 2 * HEAD_DIM

    def softmax_stage(stats, tile_max, masked):
        new = []
        for g in range(streams):
            m, l, _ = stats[g]
            s = s_scr[g]
            if masked:
                s = jnp.where(causal, s, -jnp.inf)
                m_new = jnp.maximum(m, jnp.max(s, axis=0, keepdims=True))
            else:
                m_new = jnp.maximum(m, tile_max[g])
            p = jnp.exp2(s - m_new)
            alpha = jnp.exp2(m - m_new)
            l = alpha * l + jnp.sum(p, axis=0, keepdims=True)
            p_scr[g] = p.astype(BF16)
            new.append((m_new, l, alpha))
        return tuple(new)

    def value_stage(t, stats):
        vt_tile = vt_ref[0, qi - t]
        for g in range(streams):
            hd = g // 2
            acc_scr[g] = stats[g][2] * acc_scr[g] + jnp.dot(
                vt_tile[hd * vdim:(hd + 1) * vdim, :], p_scr[g], preferred_element_type=F32)

    init = tuple((jnp.full((1, tq), -jnp.inf, F32), jnp.zeros((1, tq), F32),
                  jnp.zeros((1, tq), F32)) for _ in range(streams))
    _score_stage(k_ref, qi, qt_scr, s_scr)
    stats = softmax_stage(init, None, True)
    tile_max = _score_stage(k_ref, jnp.maximum(qi - 1, 0), qt_scr, s_scr, True)

    def step(j, carry):
        stats, tile_max = carry
        value_stage(j - 1, stats)
        stats = softmax_stage(stats, tile_max, False)
        return stats, _score_stage(k_ref, qi - (j + 1), qt_scr, s_scr, True)

    stats, tile_max = lax.fori_loop(1, qi, step, (stats, tile_max))

    def last_step(stats):
        value_stage(qi - 1, stats)
        return softmax_stage(stats, tile_max, False)

    stats = lax.cond(qi > 0, last_step, lambda st: st, stats)
    value_stage(qi, stats)

    lam = (jnp.exp(jnp.sum(lq1_ref[...] * lk1_ref[...], axis=-1, keepdims=True))
           - jnp.exp(jnp.sum(lq2_ref[...] * lk2_ref[...], axis=-1, keepdims=True))
           + lambda_init)
    for hd in range(streams // 2):
        l1, l2 = stats[2 * hd][1], stats[2 * hd + 1][1]
        out_t = acc_scr[2 * hd] / l1 - lam * (acc_scr[2 * hd + 1] / l2)
        ms = jnp.mean(out_t * out_t, axis=0, keepdims=True)
        out = (out_t * lax.rsqrt(ms + EPS)).T
        out = out * sg_ref[...] * (1.0 - lambda_init)
        cols = slice(hd * vdim, (hd + 1) * vdim)
        o_ref[:, cols] = (out * _silu(g_ref[:, cols].astype(F32))).astype(BF16)


def _diff_attention(y, k_tiles, vt_tiles, lq1, lk1, lq2, lk2, subln_g, *, batch, seq, width,
                    lambda_init):
    nq = seq // Q_TILE
    nkb = seq // KV_TILE
    cols = DIFF_STREAMS * HEAD_DIM
    g_col = width // cols
    vec = lambda a: a.reshape(1, -1)
    small = lambda n: pl.BlockSpec((1, n), lambda b, h, i: (0, 0))
    return pl.pallas_call(
        functools.partial(_diff_kernel, tq=Q_TILE, lambda_init=lambda_init),
        grid=(batch, width // cols, nq),
        in_specs=[pl.BlockSpec((Q_TILE, cols), lambda b, h, i: (b * nq + i, h)),
                  pl.BlockSpec((1, nkb, cols // LANES, 2 * KV_TILE, LANES),
                               lambda b, h, i: (b, 0, h, 0, 0)),
                  pl.BlockSpec((1, nkb, cols, KV_TILE), lambda b, h, i: (b, 0, h, 0)),
                  pl.BlockSpec((Q_TILE, cols), lambda b, h, i: (b * nq + i, g_col + h)),
                  small(HEAD_DIM), small(HEAD_DIM), small(HEAD_DIM), small(HEAD_DIM),
                  small(2 * HEAD_DIM)],
        out_specs=pl.BlockSpec((Q_TILE, cols), lambda b, h, i: (b * nq + i, h)),
        out_shape=jax.ShapeDtypeStruct((batch * seq, width), BF16),
        scratch_shapes=_attention_scratch(DIFF_STREAMS, Q_TILE, 2 * HEAD_DIM),
        compiler_params=_params(3),
        name="differential_attention",
    )(y, k_tiles, vt_tiles, y, vec(lq1), vec(lk1), vec(lq2), vec(lk2), vec(subln_g))


def _post_kernel(*refs, final):
    if final:
        (h_ref, ma_ref, mb_ref, p_ref, woa_ref, wob_ref, gp_ref, wg_ref, wp_ref,
         fn_ref, o_ref) = refs
    else:
        h_ref, ma_ref, mb_ref, p_ref, woa_ref, wob_ref, gp_ref, wg_ref, wp_ref, o_ref = refs
    h = (h_ref[...]
         + jnp.dot(ma_ref[...], woa_ref[...], preferred_element_type=F32)
         + jnp.dot(mb_ref[...], wob_ref[...], preferred_element_type=F32))
    ms = jnp.mean(h * h, axis=-1, keepdims=True)
    hn = (h * lax.rsqrt(ms + EPS) * gp_ref[...]).astype(BF16)
    gate = _sigmoid(jnp.dot(hn, wg_ref[...], preferred_element_type=F32))
    ple = jnp.dot(p_ref[...].astype(BF16), wp_ref[...], preferred_element_type=F32)
    h = h + gate * ple
    if final:
        ms = jnp.mean(h * h, axis=-1, keepdims=True)
        h = h * lax.rsqrt(ms + EPS) * fn_ref[...]
    o_ref[...] = h


def _post(h, mix_a, mix_b, b_col, p, w_out, g_ple, w_gate, w_proj, final_norm=None, *, tm=512):
    t, d = h.shape
    half = w_out.shape[0] // 2
    pd = p.shape[1]
    final = final_norm is not None
    const = lambda shape: pl.BlockSpec(shape, lambda i: (0, 0))
    in_specs = [pl.BlockSpec((tm, d), lambda i: (i, 0)),
                pl.BlockSpec((tm, half), lambda i: (i, 0)),
                pl.BlockSpec((tm, half), lambda i: (i, b_col)),
                pl.BlockSpec((tm, pd), lambda i: (i, 0)),
                pl.BlockSpec((half, d), lambda i: (0, 0)),
                pl.BlockSpec((half, d), lambda i: (1, 0)),
                const((1, d)), const((d, d)), const((pd, d))]
    args = [h, mix_a, mix_b, p, w_out, w_out, g_ple.reshape(1, d), w_gate, w_proj]
    if final:
        in_specs.append(const((1, d)))
        args.append(final_norm.reshape(1, d))
    return pl.pallas_call(
        functools.partial(_post_kernel, final=final),
        grid=(t // tm,),
        in_specs=in_specs,
        out_specs=pl.BlockSpec((tm, d), lambda i: (i, 0)),
        out_shape=jax.ShapeDtypeStruct((t, d), F32),
        compiler_params=_params(1),
        name="out_proj_ple_final" if final else "out_proj_ple",
    )(*args)


def _block_diag(w):
    nb, bs, _ = w.shape
    eye = jnp.eye(nb, dtype=w.dtype)
    return jnp.einsum('ncd,nm->ncmd', w, eye).reshape(nb * bs, nb * bs)


def kernel(x, p, positions, norm_mix, norm_ple, w_ple_gate, w_ple_proj, w_in_e, conv_w, conv_b,
           lru_wa, lru_ba, lru_wx, lru_bx, lru_lambda, w_out_e, w_in_o, lam_q1, lam_k1, lam_q2,
           lam_k2, subln_g, w_out_o, final_norm):
    batch, seq, d = x.shape
    depth = norm_mix.shape[0]
    t = batch * seq
    lru_w = conv_w.shape[-1]
    sb_w = (w_in_e.shape[-1] - 2 * lru_w) // 4
    diff_w = w_in_o.shape[-1] // 4

    half = HEAD_DIM // 2
    inv_freq = ROPE_THETA ** (-jnp.arange(0, HEAD_DIM, 2, dtype=F32) / HEAD_DIM)
    ang = positions.astype(F32).reshape(t, 1) * inv_freq
    cos = jnp.tile(jnp.cos(ang), (1, LANES // half))
    sin = jnp.tile(jnp.concatenate([-jnp.sin(ang), jnp.sin(ang)], axis=1), (1, LANES // HEAD_DIM))

    h = x.reshape(t, d)
    for i in range(depth):
        j = i // 2
        last = i == depth - 1
        if i % 2 == 0:
            w = w_in_e[j].astype(BF16)
            q0 = 2 * lru_w
            k0, v0, g0 = q0 + sb_w, q0 + 2 * sb_w, q0 + 3 * sb_w
            w_main = jnp.concatenate([w[:, q0:k0], w[:, g0:]], axis=1)
            w_gates = jnp.concatenate([_block_diag(lru_wa[j]), _block_diag(lru_wx[j])], axis=1)
            b_gates = jnp.concatenate([lru_ba[j], lru_bx[j]])
            lru = (w[:, :q0], conv_w[j], conv_b[j], w_gates.astype(BF16), b_gates, lru_lambda[j])
            y, k_tiles, vt_tiles, oa = _in_proj(
                h, norm_mix[i], w_main, w[:, k0:v0], w[:, v0:g0], lru=lru, batch=batch, seq=seq,
                q_range=(0, sb_w), scale=0.5 * HEAD_DIM ** -0.5, permute_keys=True)
            ob = _sb_attention(y, k_tiles, vt_tiles, batch=batch, seq=seq,
                               q0=0, g0=sb_w, width=sb_w)
            mix_a, mix_b, b_col, w_out = oa, ob, 0, w_out_e[j]
        else:
            lambda_init = 0.8 - 0.6 * math.exp(-0.3 * i)
            w = w_in_o[j].astype(BF16)
            w_main = jnp.concatenate([w[:, :diff_w], w[:, 3 * diff_w:]], axis=1)
            y, k_tiles, vt_tiles = _in_proj(
                h, norm_mix[i], w_main, w[:, diff_w:2 * diff_w], w[:, 2 * diff_w:3 * diff_w],
                cos, sin, batch=batch, seq=seq, q_range=(0, diff_w),
                scale=HEAD_DIM ** -0.5 * math.log2(math.e), permute_keys=False)
            o = _diff_attention(y, k_tiles, vt_tiles, lam_q1[j], lam_k1[j], lam_q2[j], lam_k2[j],
                                subln_g[j], batch=batch, seq=seq, width=diff_w,
                                lambda_init=lambda_init)
            mix_a, mix_b, b_col, w_out = o, o, 1, w_out_o[j]
        h = _post(h, mix_a, mix_b, b_col, p[i].reshape(t, -1), w_out.astype(BF16), norm_ple[i],
                  w_ple_gate[i].astype(BF16), w_ple_proj[i].astype(BF16),
                  final_norm if last else None)
    return h.reshape(batch, seq, d)
```

```python
import functools
import math

import jax
import jax.numpy as jnp
from jax import lax
from jax.experimental import pallas as pl
from jax.experimental.pallas import tpu as pltpu

F32 = jnp.float32
BF16 = jnp.bfloat16

EPS = 1e-6
HEAD_DIM = 64
LRU_C = 8.0
CONV_WIDTH = 4
ROPE_THETA = 10000.0

LANES = 128
SUBLANES = 8
KV_TILE = 256
Q_TILE = 256
SEGMENT = KV_TILE // SUBLANES
SB_STREAMS = 8
DIFF_STREAMS = 8
SUM_ROWS = 16
VMEM_LIMIT = 48 * 1024 * 1024


def _params(n_axes):
    return pltpu.CompilerParams(
        dimension_semantics=("arbitrary",) * n_axes, vmem_limit_bytes=VMEM_LIMIT)


def _sigmoid(x):
    return 0.5 + 0.5 * jnp.tanh(0.5 * x)


def _silu(x):
    return x * _sigmoid(x)


def _rotary(y, cos, sin, first_half):
    width = y.shape[1]
    partner = jnp.where(first_half, pltpu.roll(y, width - HEAD_DIM // 2, 1),
                        pltpu.roll(y, HEAD_DIM // 2, 1))
    return y * cos + partner * sin


def _in_proj_kernel(*refs, chunk, rope, q_range, scale, permute_keys, lru_steps, value_head):
    x_ref, g_ref, wm_ref, wk_ref, wv_ref = refs[:5]
    if rope:
        cos_ref, sin_ref = refs[5:7]
    if lru_steps:
        wl_ref, cw_ref, cb_ref, wg_ref, bg_ref, lam_ref = refs[5:11]
        m_ref, k_ref, vt_ref, oa_ref, ext_ref, h_ref = refs[11:]
    else:
        m_ref, k_ref, vt_ref = refs[-3:]
    tm = x_ref.shape[0]
    x = x_ref[...]
    ms = jnp.mean(x * x, axis=-1, keepdims=True)
    xn = x * lax.rsqrt(ms + EPS) * g_ref[...]
    xn_bf = xn.astype(BF16)
    if rope:
        reps = chunk // LANES
        cos = jnp.concatenate([cos_ref[...]] * reps, axis=1)
        sin = jnp.concatenate([sin_ref[...]] * reps, axis=1)
        lane = lax.broadcasted_iota(jnp.int32, (tm, chunk), 1)
        first_half = (lane % HEAD_DIM) < (HEAD_DIM // 2)

    lru_tasks = []
    if lru_steps:
        width = oa_ref.shape[1]

        @pl.when(pl.program_id(0) % lru_steps == 0)
        def _():
            ext_ref[0:SUBLANES, :] = jnp.zeros((SUBLANES, width), F32)
            h_ref[...] = jnp.zeros((1, width), F32)

        xa = jnp.dot(xn_bf, wl_ref[:, :width], preferred_element_type=F32)
        ga = jnp.dot(xn_bf, wl_ref[:, width:], preferred_element_type=F32)
        xc = _causal_conv(xa, cw_ref, cb_ref, ext_ref)

    def main_chunk(c):
        lo = c * chunk
        y = jnp.dot(xn_bf, wm_ref[:, lo:lo + chunk], preferred_element_type=F32)
        if q_range[0] <= lo < q_range[1]:
            if rope:
                y = _rotary(y, cos, sin, first_half)
            y = y * scale
        m_ref[:, lo:lo + chunk] = y.astype(BF16)

    n_tiles = tm // KV_TILE
    if permute_keys:
        r = lax.broadcasted_iota(jnp.int32, (KV_TILE, KV_TILE), 0)
        c = lax.broadcasted_iota(jnp.int32, (KV_TILE, KV_TILE), 1)
        perm = jnp.where(c == (r % SUBLANES) * SEGMENT + r // SUBLANES, 1.0, 0.0).astype(BF16)
        xk = jnp.concatenate(
            [jnp.dot(perm, xn_bf[t * KV_TILE:(t + 1) * KV_TILE, :],
                     preferred_element_type=F32).astype(BF16) for t in range(n_tiles)], axis=0)
    else:
        xk = xn_bf

    low = lax.broadcasted_iota(jnp.int32, (KV_TILE, LANES), 1) < HEAD_DIM

    def key_chunk(c):
        kk = jnp.dot(xk, wk_ref[:, c * chunk:(c + 1) * chunk], preferred_element_type=F32)
        if rope:
            kk = _rotary(kk, cos, sin, first_half)
        for t in range(n_tiles):
            for b in range(chunk // LANES):
                blk = kk[t * KV_TILE:(t + 1) * KV_TILE, b * LANES:(b + 1) * LANES]
                nb = c * (chunk // LANES) + b
                k_ref[0, t, nb, 0:KV_TILE, :] = jnp.where(low, blk, 0.0).astype(BF16)
                k_ref[0, t, nb, KV_TILE:2 * KV_TILE, :] = jnp.where(low, 0.0, blk).astype(BF16)
    def value_chunk(c):
        vv = jnp.dot(xk, wv_ref[:, c * chunk:(c + 1) * chunk], preferred_element_type=F32)
        for t in range(n_tiles):
            vt = vv[t * KV_TILE:(t + 1) * KV_TILE, :].T.astype(BF16)
            if not value_head:
                vt_ref[0, t, c * chunk:(c + 1) * chunk, :] = vt
                continue
            for hh in range(chunk // value_head):
                r0 = (c * (chunk // value_head) + hh) * (value_head + SUM_ROWS)
                vt_ref[0, t, r0:r0 + value_head, :] = vt[hh * value_head:(hh + 1) * value_head]
                vt_ref[0, t, r0 + value_head:r0 + value_head + SUM_ROWS, :] = jnp.ones(
                    (SUM_ROWS, KV_TILE), BF16)

    for c in range(wm_ref.shape[1] // chunk):
        main_chunk(c)
    mxu_tasks = ([functools.partial(key_chunk, c) for c in range(wk_ref.shape[1] // chunk)]
                 + [functools.partial(value_chunk, c) for c in range(wv_ref.shape[1] // chunk)])
    if lru_steps:
        gates = jnp.dot(xc.astype(BF16), wg_ref[...], preferred_element_type=F32) + bg_ref[...]
        ts = tm // 4

        def lru_rows(r0):
            rows = slice(r0, r0 + ts)
            _rg_lru_rows(xc[rows], gates[rows], ga[rows], lam_ref, oa_ref, r0, h_ref)

        lru_tasks = [functools.partial(lru_rows, r0) for r0 in range(0, tm, ts)]
    for n in range(max(len(mxu_tasks), len(lru_tasks))):
        for tasks in (mxu_tasks, lru_tasks):
            if n < len(tasks):
                tasks[n]()


def _in_proj(h, g, w_main, w_k, w_v, cos=None, sin=None, lru=None, *, batch, seq, q_range,
             scale, permute_keys, value_head=0, tm=512, chunk=512):
    t, d = h.shape
    n_main, n_k, n_v = w_main.shape[1], w_k.shape[1], w_v.shape[1]
    vt_rows = n_v // value_head * (value_head + SUM_ROWS) if value_head else n_v
    rope = cos is not None
    assert not (rope and lru is not None)
    steps = seq // tm
    tiles = tm // KV_TILE
    nkb = seq // KV_TILE
    const = lambda shape: pl.BlockSpec(shape, lambda i: (0, 0))
    in_specs = [pl.BlockSpec((tm, d), lambda i: (i, 0)), const((1, d)),
                const((d, n_main)), const((d, n_k)), const((d, n_v))]
    args = [h, g.reshape(1, d), w_main, w_k, w_v]
    if rope:
        in_specs += [pl.BlockSpec((tm, LANES), lambda i: (i, 0))] * 2
        args += [cos, sin]
    out_specs = [pl.BlockSpec((tm, n_main), lambda i: (i, 0)),
                 pl.BlockSpec((1, tiles, n_k // LANES, 2 * KV_TILE, LANES),
                              lambda i: (i // steps, i % steps, 0, 0, 0)),
                 pl.BlockSpec((1, tiles, vt_rows, KV_TILE),
                              lambda i: (i // steps, i % steps, 0, 0))]
    out_shape = [jax.ShapeDtypeStruct((t, n_main), BF16),
                 jax.ShapeDtypeStruct((batch, nkb, n_k // LANES, 2 * KV_TILE, LANES), BF16),
                 jax.ShapeDtypeStruct((batch, nkb, vt_rows, KV_TILE), BF16)]
    scratch = []
    if lru is not None:
        w_lru, conv_w, conv_b, w_gates, b_gates, lam = lru
        width = conv_w.shape[1]
        in_specs += [const((d, 2 * width)), const((CONV_WIDTH, width)), const((1, width)),
                     const((width, 2 * width)), const((1, 2 * width)), const((1, width))]
        args += [w_lru, conv_w, conv_b.reshape(1, width), w_gates,
                 b_gates.reshape(1, 2 * width), lam.reshape(1, width)]
        out_specs.append(pl.BlockSpec((tm, width), lambda i: (i, 0)))
        out_shape.append(jax.ShapeDtypeStruct((t, width), BF16))
        scratch = [pltpu.VMEM((tm + SUBLANES, width), F32), pltpu.VMEM((1, width), F32)]
    return pl.pallas_call(
        functools.partial(_in_proj_kernel, chunk=chunk, rope=rope, q_range=q_range,
                          scale=scale, permute_keys=permute_keys, value_head=value_head,
                          lru_steps=steps if lru is not None else 0),
        grid=(t // tm,),
        in_specs=in_specs,
        out_specs=out_specs,
        out_shape=out_shape,
        scratch_shapes=scratch,
        compiler_params=_params(1),
        name="in_proj_rope" if rope else "in_proj_lru",
    )(*args)


def _causal_conv(xa, cw_ref, cb_ref, ext_ref):
    ts = xa.shape[0]
    ext_ref[SUBLANES:SUBLANES + ts, :] = xa
    xc = cb_ref[...] + cw_ref[CONV_WIDTH - 1:CONV_WIDTH, :] * xa
    for j in range(1, CONV_WIDTH):
        k = CONV_WIDTH - 1 - j
        xc = xc + cw_ref[k:k + 1, :] * ext_ref[SUBLANES - j:SUBLANES - j + ts, :]
    ext_ref[0:SUBLANES, :] = ext_ref[ts:ts + SUBLANES, :]
    return xc


def _rg_lru_rows(xc, gates, ga, lam_ref, o_ref, row0, h_ref):
    ts, width = xc.shape
    r = _sigmoid(gates[:, :width])
    i = _sigmoid(gates[:, width:])
    lam = lam_ref[...]
    log_sig_lam = jnp.minimum(lam, 0.0) - jnp.log(1.0 + jnp.exp(-jnp.abs(lam)))
    log_a = LRU_C * r * log_sig_lam
    a = jnp.exp(log_a)
    u = jnp.sqrt(1.0 - a * a) * (i * xc)

    groups = ts // SUBLANES
    a = a.reshape(groups, SUBLANES, width)
    u = u.reshape(groups, SUBLANES, width)
    row = lax.broadcasted_iota(jnp.int32, (groups, SUBLANES, width), 1)
    step = 1
    while step < SUBLANES:
        valid = row >= step
        a_prev = pltpu.roll(a, step, 1)
        u_prev = pltpu.roll(u, step, 1)
        u = jnp.where(valid, a * u_prev + u, u)
        a = jnp.where(valid, a * a_prev, a)
        step *= 2
    gate = _silu(ga)
    state = h_ref[...]
    for pair in range(ts // (2 * SUBLANES)):
        hs = []
        for grp in (2 * pair, 2 * pair + 1):
            h = u[grp] + a[grp] * state
            state = h[SUBLANES - 1:SUBLANES, :]
            hs.append(h)
        rows = slice(2 * pair * SUBLANES, 2 * (pair + 1) * SUBLANES)
        o_ref[row0 + rows.start:row0 + rows.stop, :] = (
            jnp.concatenate(hs, axis=0) * gate[rows]).astype(BF16)
    h_ref[...] = state


def _store_transposed_queries(q_ref, qt_scr):
    for blk in range(qt_scr.shape[0]):
        qt_scr[blk] = q_ref[:, blk * LANES:(blk + 1) * LANES].astype(F32).T.astype(BF16)


def _score_stage(k_ref, kb, qt_scr, s_scr, with_max=False):
    tile_max = []
    for blk in range(qt_scr.shape[0]):
        both = jnp.dot(k_ref[0, kb, blk], qt_scr[blk], preferred_element_type=F32)
        for half in range(2):
            s = both[half * KV_TILE:(half + 1) * KV_TILE]
            s_scr[2 * blk + half] = s
            if with_max:
                tile_max.append(jnp.max(s, axis=0, keepdims=True))
    return tuple(tile_max)


def _attention_scratch(streams, tq, value_rows):
    return [pltpu.VMEM((streams // 2, LANES, tq), BF16),
            pltpu.VMEM((streams, KV_TILE, tq), F32),
            pltpu.VMEM((streams, KV_TILE, tq), BF16),
            pltpu.VMEM((streams, value_rows, tq), F32)]


def _sb_kernel(q_ref, k_ref, vt_ref, g_ref, o_ref, qt_scr, s_scr, w_scr, acc_scr, *, tq):
    qi = pl.program_id(2)
    streams = s_scr.shape[0]
    _store_transposed_queries(q_ref, qt_scr)
    acc_scr[...] = jnp.zeros(acc_scr.shape, F32)

    r = lax.broadcasted_iota(jnp.int32, (KV_TILE, tq), 0)
    c = lax.broadcasted_iota(jnp.int32, (KV_TILE, tq), 1)
    causal = ((r % SUBLANES) * SEGMENT + r // SUBLANES) < c
    srow = lax.broadcasted_iota(jnp.int32, (SUBLANES, tq), 0)

    def weight_stage(later, masked):
        new_later = []
        for g in range(streams):
            run = jnp.ones((SUBLANES, tq), F32)
            for v in reversed(range(SEGMENT)):
                rows = slice(v * SUBLANES, (v + 1) * SUBLANES)
                half_tanh = 0.5 * jnp.tanh(s_scr[g, rows, :])
                beta = 0.5 + half_tanh
                rest = 0.5 - half_tanh
                if masked:
                    beta = jnp.where(causal[rows], beta, 0.0)
                    rest = jnp.where(causal[rows], rest, 1.0)
                s_scr[g, rows, :] = beta * run
                run = run * rest
            off = jnp.broadcast_to(later[g], (SUBLANES, tq))
            for s2 in range(1, SUBLANES):
                off = off * jnp.where(srow < s2, run[s2:s2 + 1, :], 1.0)
            off2 = jnp.concatenate([off, off], axis=0)
            for v in range(SEGMENT // 2):
                rows = slice(2 * v * SUBLANES, 2 * (v + 1) * SUBLANES)
                w_scr[g, rows, :] = (s_scr[g, rows, :] * off2).astype(BF16)
            new_later.append((off * run)[0:1, :])
        return tuple(new_later)

    def value_stage(t):
        vt_tile = vt_ref[0, qi - t]
        for g in range(streams):
            acc_scr[g] += jnp.dot(vt_tile[g * HEAD_DIM:(g + 1) * HEAD_DIM, :], w_scr[g],
                                  preferred_element_type=F32)

    _score_stage(k_ref, qi, qt_scr, s_scr)
    later = weight_stage(tuple(jnp.ones((1, tq), F32) for _ in range(streams)), True)
    _score_stage(k_ref, jnp.maximum(qi - 1, 0), qt_scr, s_scr)

    def step(j, later):
        value_stage(j - 1)
        later = weight_stage(later, False)
        _score_stage(k_ref, qi - (j + 1), qt_scr, s_scr)
        return later

    later = lax.fori_loop(1, qi, step, later)

    @pl.when(qi > 0)
    def _():
        value_stage(qi - 1)
        weight_stage(later, False)

    value_stage(qi)
    out_t = acc_scr[...].reshape(streams * HEAD_DIM, tq)
    o_ref[...] = (out_t.T * _silu(g_ref[...].astype(F32))).astype(BF16)


def _sb_attention(y, k_tiles, vt_tiles, *, batch, seq, q0, g0, width):
    nq = seq // Q_TILE
    nkb = seq // KV_TILE
    cols = SB_STREAMS * HEAD_DIM
    return pl.pallas_call(
        functools.partial(_sb_kernel, tq=Q_TILE),
        grid=(batch, width // cols, nq),
        in_specs=[pl.BlockSpec((Q_TILE, cols), lambda b, p, i: (b * nq + i, q0 // cols + p)),
                  pl.BlockSpec((1, nkb, cols // LANES, 2 * KV_TILE, LANES),
                               lambda b, p, i: (b, 0, p, 0, 0)),
                  pl.BlockSpec((1, nkb, cols, KV_TILE), lambda b, p, i: (b, 0, p, 0)),
                  pl.BlockSpec((Q_TILE, cols), lambda b, p, i: (b * nq + i, g0 // cols + p))],
        out_specs=pl.BlockSpec((Q_TILE, cols), lambda b, p, i: (b * nq + i, p)),
        out_shape=jax.ShapeDtypeStruct((batch * seq, width), BF16),
        scratch_shapes=_attention_scratch(SB_STREAMS, Q_TILE, HEAD_DIM),
        compiler_params=_params(3),
        name="stick_breaking_attention",
    )(y, k_tiles, vt_tiles, y)


def _diff_kernel(q_ref, k_ref, vt_ref, g_ref, lq1_ref, lk1_ref, lq2_ref, lk2_ref, sg_ref,
                 o_ref, qt_scr, s_scr, p_scr, acc_scr, *, tq, lambda_init):
    qi = pl.program_id(2)
    streams = s_scr.shape[0]
    _store_transposed_queries(q_ref, qt_scr)
    acc_scr[...] = jnp.zeros(acc_scr.shape, F32)
    r = lax.broadcasted_iota(jnp.int32, (KV_TILE, tq), 0)
    c = lax.broadcasted_iota(jnp.int32, (KV_TILE, tq), 1)
    causal = r <= c
    vdim = 2 * HEAD_DIM
    vrows = vdim + SUM_ROWS

    def softmax_stage(stats, tile_max, masked):
        new = []
        for g in range(streams):
            m, _ = stats[g]
            s = s_scr[g]
            if masked:
                s = jnp.where(causal, s, -jnp.inf)
                m_new = jnp.maximum(m, jnp.max(s, axis=0, keepdims=True))
            else:
                m_new = jnp.maximum(m, tile_max[g])
            p = jnp.exp2(s - m_new)
            alpha = jnp.exp2(m - m_new)
            p_scr[g] = p.astype(BF16)
            new.append((m_new, alpha))
        return tuple(new)

    def value_stage(t, stats):
        vt_tile = vt_ref[0, qi - t]
        for g in range(streams):
            hd = g // 2
            acc_scr[g] = stats[g][1] * acc_scr[g] + jnp.dot(
                vt_tile[hd * vrows:(hd + 1) * vrows, :], p_scr[g], preferred_element_type=F32)

    init = tuple((jnp.full((1, tq), -jnp.inf, F32), jnp.zeros((1, tq), F32))
                 for _ in range(streams))
    _score_stage(k_ref, qi, qt_scr, s_scr)
    stats = softmax_stage(init, None, True)
    tile_max = _score_stage(k_ref, jnp.maximum(qi - 1, 0), qt_scr, s_scr, True)

    def step(j, carry):
        stats, tile_max = carry
        value_stage(j - 1, stats)
        stats = softmax_stage(stats, tile_max, False)
        return stats, _score_stage(k_ref, qi - (j + 1), qt_scr, s_scr, True)

    stats, tile_max = lax.fori_loop(1, qi, step, (stats, tile_max))

    def last_step(stats):
        value_stage(qi - 1, stats)
        return softmax_stage(stats, tile_max, False)

    stats = lax.cond(qi > 0, last_step, lambda st: st, stats)
    value_stage(qi, stats)

    lam = (jnp.exp(jnp.sum(lq1_ref[...] * lk1_ref[...], axis=-1, keepdims=True))
           - jnp.exp(jnp.sum(lq2_ref[...] * lk2_ref[...], axis=-1, keepdims=True))
           + lambda_init)
    for hd in range(streams // 2):
        acc1, acc2 = acc_scr[2 * hd], acc_scr[2 * hd + 1]
        out_t = (acc1[:vdim] / acc1[vdim:vdim + 1]
                 - lam * (acc2[:vdim] / acc2[vdim:vdim + 1]))
        ms = jnp.mean(out_t * out_t, axis=0, keepdims=True)
        out = (out_t * lax.rsqrt(ms + EPS)).T
        out = out * sg_ref[...] * (1.0 - lambda_init)
        cols = slice(hd * vdim, (hd + 1) * vdim)
        o_ref[:, cols] = (out * _silu(g_ref[:, cols].astype(F32))).astype(BF16)


def _diff_attention(y, k_tiles, vt_tiles, lq1, lk1, lq2, lk2, subln_g, *, batch, seq, width,
                    lambda_init):
    nq = seq // Q_TILE
    nkb = seq // KV_TILE
    cols = DIFF_STREAMS * HEAD_DIM
    g_col = width // cols
    vt_rows = DIFF_STREAMS // 2 * (2 * HEAD_DIM + SUM_ROWS)
    vec = lambda a: a.reshape(1, -1)
    small = lambda n: pl.BlockSpec((1, n), lambda b, h, i: (0, 0))
    return pl.pallas_call(
        functools.partial(_diff_kernel, tq=Q_TILE, lambda_init=lambda_init),
        grid=(batch, width // cols, nq),
        in_specs=[pl.BlockSpec((Q_TILE, cols), lambda b, h, i: (b * nq + i, h)),
                  pl.BlockSpec((1, nkb, cols // LANES, 2 * KV_TILE, LANES),
                               lambda b, h, i: (b, 0, h, 0, 0)),
                  pl.BlockSpec((1, nkb, vt_rows, KV_TILE), lambda b, h, i: (b, 0, h, 0)),
                  pl.BlockSpec((Q_TILE, cols), lambda b, h, i: (b * nq + i, g_col + h)),
                  small(HEAD_DIM), small(HEAD_DIM), small(HEAD_DIM), small(HEAD_DIM),
                  small(2 * HEAD_DIM)],
        out_specs=pl.BlockSpec((Q_TILE, cols), lambda b, h, i: (b * nq + i, h)),
        out_shape=jax.ShapeDtypeStruct((batch * seq, width), BF16),
        scratch_shapes=_attention_scratch(DIFF_STREAMS, Q_TILE, 2 * HEAD_DIM + SUM_ROWS),
        compiler_params=_params(3),
        name="differential_attention",
    )(y, k_tiles, vt_tiles, y, vec(lq1), vec(lk1), vec(lq2), vec(lk2), vec(subln_g))


def _post_kernel(*refs, final):
    if final:
        (h_ref, ma_ref, mb_ref, p_ref, woa_ref, wob_ref, gp_ref, wg_ref, wp_ref,
         fn_ref, o_ref) = refs
    else:
        h_ref, ma_ref, mb_ref, p_ref, woa_ref, wob_ref, gp_ref, wg_ref, wp_ref, o_ref = refs
    h = (h_ref[...]
         + jnp.dot(ma_ref[...], woa_ref[...], preferred_element_type=F32)
         + jnp.dot(mb_ref[...], wob_ref[...], preferred_element_type=F32))
    ms = jnp.mean(h * h, axis=-1, keepdims=True)
    hn = (h * lax.rsqrt(ms + EPS) * gp_ref[...]).astype(BF16)
    gate = _sigmoid(jnp.dot(hn, wg_ref[...], preferred_element_type=F32))
    ple = jnp.dot(p_ref[...].astype(BF16), wp_ref[...], preferred_element_type=F32)
    h = h + gate * ple
    if final:
        ms = jnp.mean(h * h, axis=-1, keepdims=True)
        h = h * lax.rsqrt(ms + EPS) * fn_ref[...]
    o_ref[...] = h


def _post(h, mix_a, mix_b, b_col, p, w_out, g_ple, w_gate, w_proj, final_norm=None, *, tm=512):
    t, d = h.shape
    half = w_out.shape[0] // 2
    pd = p.shape[1]
    final = final_norm is not None
    const = lambda shape: pl.BlockSpec(shape, lambda i: (0, 0))
    in_specs = [pl.BlockSpec((tm, d), lambda i: (i, 0)),
                pl.BlockSpec((tm, half), lambda i: (i, 0)),
                pl.BlockSpec((tm, half), lambda i: (i, b_col)),
                pl.BlockSpec((tm, pd), lambda i: (i, 0)),
                pl.BlockSpec((half, d), lambda i: (0, 0)),
                pl.BlockSpec((half, d), lambda i: (1, 0)),
                const((1, d)), const((d, d)), const((pd, d))]
    args = [h, mix_a, mix_b, p, w_out, w_out, g_ple.reshape(1, d), w_gate, w_proj]
    if final:
        in_specs.append(const((1, d)))
        args.append(final_norm.reshape(1, d))
    return pl.pallas_call(
        functools.partial(_post_kernel, final=final),
        grid=(t // tm,),
        in_specs=in_specs,
        out_specs=pl.BlockSpec((tm, d), lambda i: (i, 0)),
        out_shape=jax.ShapeDtypeStruct((t, d), F32),
        compiler_params=_params(1),
        name="out_proj_ple_final" if final else "out_proj_ple",
    )(*args)


def _block_diag(w):
    nb, bs, _ = w.shape
    eye = jnp.eye(nb, dtype=w.dtype)
    return jnp.einsum('ncd,nm->ncmd', w, eye).reshape(nb * bs, nb * bs)


def kernel(x, p, positions, norm_mix, norm_ple, w_ple_gate, w_ple_proj, w_in_e, conv_w, conv_b,
           lru_wa, lru_ba, lru_wx, lru_bx, lru_lambda, w_out_e, w_in_o, lam_q1, lam_k1, lam_q2,
           lam_k2, subln_g, w_out_o, final_norm):
    batch, seq, d = x.shape
    depth = norm_mix.shape[0]
    t = batch * seq
    lru_w = conv_w.shape[-1]
    sb_w = (w_in_e.shape[-1] - 2 * lru_w) // 4
    diff_w = w_in_o.shape[-1] // 4

    half = HEAD_DIM // 2
    inv_freq = ROPE_THETA ** (-jnp.arange(0, HEAD_DIM, 2, dtype=F32) / HEAD_DIM)
    ang = positions.astype(F32).reshape(t, 1) * inv_freq
    cos = jnp.tile(jnp.cos(ang), (1, LANES // half))
    sin = jnp.tile(jnp.concatenate([-jnp.sin(ang), jnp.sin(ang)], axis=1), (1, LANES // HEAD_DIM))

    h = x.reshape(t, d)
    for i in range(depth):
        j = i // 2
        last = i == depth - 1
        if i % 2 == 0:
            w = w_in_e[j].astype(BF16)
            q0 = 2 * lru_w
            k0, v0, g0 = q0 + sb_w, q0 + 2 * sb_w, q0 + 3 * sb_w
            w_main = jnp.concatenate([w[:, q0:k0], w[:, g0:]], axis=1)
            w_gates = jnp.concatenate([_block_diag(lru_wa[j]), _block_diag(lru_wx[j])], axis=1)
            b_gates = jnp.concatenate([lru_ba[j], lru_bx[j]])
            lru = (w[:, :q0], conv_w[j], conv_b[j], w_gates.astype(BF16), b_gates, lru_lambda[j])
            y, k_tiles, vt_tiles, oa = _in_proj(
                h, norm_mix[i], w_main, w[:, k0:v0], w[:, v0:g0], lru=lru, batch=batch, seq=seq,
                q_range=(0, sb_w), scale=0.5 * HEAD_DIM ** -0.5, permute_keys=True)
            ob = _sb_attention(y, k_tiles, vt_tiles, batch=batch, seq=seq,
                               q0=0, g0=sb_w, width=sb_w)
            mix_a, mix_b, b_col, w_out = oa, ob, 0, w_out_e[j]
        else:
            lambda_init = 0.8 - 0.6 * math.exp(-0.3 * i)
            w = w_in_o[j].astype(BF16)
            w_main = jnp.concatenate([w[:, :diff_w], w[:, 3 * diff_w:]], axis=1)
            y, k_tiles, vt_tiles = _in_proj(
                h, norm_mix[i], w_main, w[:, diff_w:2 * diff_w], w[:, 2 * diff_w:3 * diff_w],
                cos, sin, batch=batch, seq=seq, q_range=(0, diff_w),
                scale=HEAD_DIM ** -0.5 * math.log2(math.e), permute_keys=False,
                value_head=2 * HEAD_DIM)
            o = _diff_attention(y, k_tiles, vt_tiles, lam_q1[j], lam_k1[j], lam_q2[j], lam_k2[j],
                                subln_g[j], batch=batch, seq=seq, width=diff_w,
                                lambda_init=lambda_init)
            mix_a, mix_b, b_col, w_out = o, o, 1, w_out_o[j]
        h = _post(h, mix_a, mix_b, b_col, p[i].reshape(t, -1), w_out.astype(BF16), norm_ple[i],
                  w_ple_gate[i].astype(BF16), w_ple_proj[i].astype(BF16),
                  final_norm if last else None)
    return h.reshape(batch, seq, d)
```

```python
import functools
import math

import jax
import jax.numpy as jnp
from jax import lax
from jax.experimental import pallas as pl
from jax.experimental.pallas import tpu as pltpu

F32 = jnp.float32
BF16 = jnp.bfloat16

EPS = 1e-6
HEAD_DIM = 64
LRU_C = 8.0
CONV_WIDTH = 4
ROPE_THETA = 10000.0

LANES = 128
SUBLANES = 8
KV_TILE = 256
Q_TILE = 256
SEGMENT = KV_TILE // SUBLANES
SB_STREAMS = 8
DIFF_STREAMS = 8
SUM_ROWS = 16
VMEM_LIMIT = 48 * 1024 * 1024


def _params(n_axes):
    return pltpu.CompilerParams(
        dimension_semantics=("arbitrary",) * n_axes, vmem_limit_bytes=VMEM_LIMIT)


def _sigmoid(x):
    return 0.5 + 0.5 * jnp.tanh(0.5 * x)


def _silu(x):
    return x * _sigmoid(x)


def _rotary(y, cos, sin, first_half):
    width = y.shape[1]
    partner = jnp.where(first_half, pltpu.roll(y, width - HEAD_DIM // 2, 1),
                        pltpu.roll(y, HEAD_DIM // 2, 1))
    return y * cos + partner * sin


def _in_proj_kernel(*refs, chunk, rope, q_range, scale, permute_keys, lru_steps, value_head):
    x_ref, g_ref, wm_ref, wk_ref, wv_ref = refs[:5]
    if rope:
        cos_ref, sin_ref = refs[5:7]
    if lru_steps:
        wl_ref, cw_ref, cb_ref, wg_ref, bg_ref, lam_ref = refs[5:11]
        m_ref, k_ref, vt_ref, oa_ref, ext_ref, h_ref = refs[11:]
    else:
        m_ref, k_ref, vt_ref = refs[-3:]
    tm = x_ref.shape[0]
    x = x_ref[...]
    ms = jnp.mean(x * x, axis=-1, keepdims=True)
    xn = x * lax.rsqrt(ms + EPS) * g_ref[...]
    xn_bf = xn.astype(BF16)
    if rope:
        reps = chunk // LANES
        cos = jnp.concatenate([cos_ref[...]] * reps, axis=1)
        sin = jnp.concatenate([sin_ref[...]] * reps, axis=1)
        lane = lax.broadcasted_iota(jnp.int32, (tm, chunk), 1)
        first_half = (lane % HEAD_DIM) < (HEAD_DIM // 2)

    lru_tasks = []
    if lru_steps:
        width = oa_ref.shape[1]

        @pl.when(pl.program_id(0) % lru_steps == 0)
        def _():
            ext_ref[0:SUBLANES, :] = jnp.zeros((SUBLANES, width), F32)
            h_ref[...] = jnp.zeros((1, width), F32)

        xa = jnp.dot(xn_bf, wl_ref[:, :width], preferred_element_type=F32)
        ga = jnp.dot(xn_bf, wl_ref[:, width:], preferred_element_type=F32)
        xc = _causal_conv(xa, cw_ref, cb_ref, ext_ref)

    def main_chunk(c):
        lo = c * chunk
        y = jnp.dot(xn_bf, wm_ref[:, lo:lo + chunk], preferred_element_type=F32)
        if q_range[0] <= lo < q_range[1]:
            if rope:
                y = _rotary(y, cos, sin, first_half)
            y = y * scale
        m_ref[:, lo:lo + chunk] = y.astype(BF16)

    n_tiles = tm // KV_TILE
    if permute_keys:
        r = lax.broadcasted_iota(jnp.int32, (KV_TILE, KV_TILE), 0)
        c = lax.broadcasted_iota(jnp.int32, (KV_TILE, KV_TILE), 1)
        perm = jnp.where(c == (r % SUBLANES) * SEGMENT + r // SUBLANES, 1.0, 0.0).astype(BF16)
        xk = jnp.concatenate(
            [jnp.dot(perm, xn_bf[t * KV_TILE:(t + 1) * KV_TILE, :],
                     preferred_element_type=F32).astype(BF16) for t in range(n_tiles)], axis=0)
    else:
        xk = xn_bf

    low = lax.broadcasted_iota(jnp.int32, (KV_TILE, LANES), 1) < HEAD_DIM

    def key_chunk(c):
        kk = jnp.dot(xk, wk_ref[:, c * chunk:(c + 1) * chunk], preferred_element_type=F32)
        if rope:
            kk = _rotary(kk, cos, sin, first_half)
        for t in range(n_tiles):
            for b in range(chunk // LANES):
                blk = kk[t * KV_TILE:(t + 1) * KV_TILE, b * LANES:(b + 1) * LANES]
                nb = c * (chunk // LANES) + b
                k_ref[0, t, nb, 0:KV_TILE, :] = jnp.where(low, blk, 0.0).astype(BF16)
                k_ref[0, t, nb, KV_TILE:2 * KV_TILE, :] = jnp.where(low, 0.0, blk).astype(BF16)
    def value_chunk(c):
        vv = jnp.dot(xk, wv_ref[:, c * chunk:(c + 1) * chunk], preferred_element_type=F32)
        for t in range(n_tiles):
            vt = vv[t * KV_TILE:(t + 1) * KV_TILE, :].T.astype(BF16)
            if not value_head:
                vt_ref[0, t, c * chunk:(c + 1) * chunk, :] = vt
                continue
            for hh in range(chunk // value_head):
                r0 = (c * (chunk // value_head) + hh) * (value_head + SUM_ROWS)
                vt_ref[0, t, r0:r0 + value_head, :] = vt[hh * value_head:(hh + 1) * value_head]
                vt_ref[0, t, r0 + value_head:r0 + value_head + SUM_ROWS, :] = jnp.ones(
                    (SUM_ROWS, KV_TILE), BF16)

    for c in range(wm_ref.shape[1] // chunk):
        main_chunk(c)
    mxu_tasks = ([functools.partial(key_chunk, c) for c in range(wk_ref.shape[1] // chunk)]
                 + [functools.partial(value_chunk, c) for c in range(wv_ref.shape[1] // chunk)])
    if lru_steps:
        gates = jnp.dot(xc.astype(BF16), wg_ref[...], preferred_element_type=F32) + bg_ref[...]
        ts = tm // 4

        def lru_rows(r0):
            rows = slice(r0, r0 + ts)
            _rg_lru_rows(xc[rows], gates[rows], ga[rows], lam_ref, oa_ref, r0, h_ref)

        lru_tasks = [functools.partial(lru_rows, r0) for r0 in range(0, tm, ts)]
    for n in range(max(len(mxu_tasks), len(lru_tasks))):
        for tasks in (mxu_tasks, lru_tasks):
            if n < len(tasks):
                tasks[n]()


def _in_proj(h, g, w_main, w_k, w_v, cos=None, sin=None, lru=None, *, batch, seq, q_range,
             scale, permute_keys, value_head=0, tm=512, chunk=512):
    t, d = h.shape
    n_main, n_k, n_v = w_main.shape[1], w_k.shape[1], w_v.shape[1]
    vt_rows = n_v // value_head * (value_head + SUM_ROWS) if value_head else n_v
    rope = cos is not None
    assert not (rope and lru is not None)
    steps = seq // tm
    tiles = tm // KV_TILE
    nkb = seq // KV_TILE
    const = lambda shape: pl.BlockSpec(shape, lambda i: (0, 0))
    in_specs = [pl.BlockSpec((tm, d), lambda i: (i, 0)), const((1, d)),
                const((d, n_main)), const((d, n_k)), const((d, n_v))]
    args = [h, g.reshape(1, d), w_main, w_k, w_v]
    if rope:
        in_specs += [pl.BlockSpec((tm, LANES), lambda i: (i, 0))] * 2
        args += [cos, sin]
    out_specs = [pl.BlockSpec((tm, n_main), lambda i: (i, 0)),
                 pl.BlockSpec((1, tiles, n_k // LANES, 2 * KV_TILE, LANES),
                              lambda i: (i // steps, i % steps, 0, 0, 0)),
                 pl.BlockSpec((1, tiles, vt_rows, KV_TILE),
                              lambda i: (i // steps, i % steps, 0, 0))]
    out_shape = [jax.ShapeDtypeStruct((t, n_main), BF16),
                 jax.ShapeDtypeStruct((batch, nkb, n_k // LANES, 2 * KV_TILE, LANES), BF16),
                 jax.ShapeDtypeStruct((batch, nkb, vt_rows, KV_TILE), BF16)]
    scratch = []
    if lru is not None:
        w_lru, conv_w, conv_b, w_gates, b_gates, lam = lru
        width = conv_w.shape[1]
        in_specs += [const((d, 2 * width)), const((CONV_WIDTH, width)), const((1, width)),
                     const((width, 2 * width)), const((1, 2 * width)), const((1, width))]
        args += [w_lru, conv_w, conv_b.reshape(1, width), w_gates,
                 b_gates.reshape(1, 2 * width), lam.reshape(1, width)]
        out_specs.append(pl.BlockSpec((tm, width), lambda i: (i, 0)))
        out_shape.append(jax.ShapeDtypeStruct((t, width), BF16))
        scratch = [pltpu.VMEM((tm + SUBLANES, width), F32), pltpu.VMEM((1, width), F32)]
    return pl.pallas_call(
        functools.partial(_in_proj_kernel, chunk=chunk, rope=rope, q_range=q_range,
                          scale=scale, permute_keys=permute_keys, value_head=value_head,
                          lru_steps=steps if lru is not None else 0),
        grid=(t // tm,),
        in_specs=in_specs,
        out_specs=out_specs,
        out_shape=out_shape,
        scratch_shapes=scratch,
        compiler_params=_params(1),
        name="in_proj_rope" if rope else "in_proj_lru",
    )(*args)


def _causal_conv(xa, cw_ref, cb_ref, ext_ref):
    ts = xa.shape[0]
    ext_ref[SUBLANES:SUBLANES + ts, :] = xa
    xc = cb_ref[...] + cw_ref[CONV_WIDTH - 1:CONV_WIDTH, :] * xa
    for j in range(1, CONV_WIDTH):
        k = CONV_WIDTH - 1 - j
        xc = xc + cw_ref[k:k + 1, :] * ext_ref[SUBLANES - j:SUBLANES - j + ts, :]
    ext_ref[0:SUBLANES, :] = ext_ref[ts:ts + SUBLANES, :]
    return xc


def _rg_lru_rows(xc, gates, ga, lam_ref, o_ref, row0, h_ref):
    ts, width = xc.shape
    r = _sigmoid(gates[:, :width])
    i = _sigmoid(gates[:, width:])
    lam = lam_ref[...]
    log_sig_lam = jnp.minimum(lam, 0.0) - jnp.log(1.0 + jnp.exp(-jnp.abs(lam)))
    log_a = LRU_C * r * log_sig_lam
    a = jnp.exp(log_a)
    u = jnp.sqrt(1.0 - a * a) * (i * xc)

    groups = ts // SUBLANES
    a = a.reshape(groups, SUBLANES, width)
    u = u.reshape(groups, SUBLANES, width)
    row = lax.broadcasted_iota(jnp.int32, (groups, SUBLANES, width), 1)
    step = 1
    while step < SUBLANES:
        valid = row >= step
        a_prev = pltpu.roll(a, step, 1)
        u_prev = pltpu.roll(u, step, 1)
        u = jnp.where(valid, a * u_prev + u, u)
        a = jnp.where(valid, a * a_prev, a)
        step *= 2
    gate = _silu(ga)
    state = h_ref[...]
    for pair in range(ts // (2 * SUBLANES)):
        hs = []
        for grp in (2 * pair, 2 * pair + 1):
            h = u[grp] + a[grp] * state
            state = h[SUBLANES - 1:SUBLANES, :]
            hs.append(h)
        rows = slice(2 * pair * SUBLANES, 2 * (pair + 1) * SUBLANES)
        o_ref[row0 + rows.start:row0 + rows.stop, :] = (
            jnp.concatenate(hs, axis=0) * gate[rows]).astype(BF16)
    h_ref[...] = state


def _store_transposed_queries(q_ref, qt_scr):
    for blk in range(qt_scr.shape[0]):
        qt_scr[blk] = q_ref[:, blk * LANES:(blk + 1) * LANES].astype(F32).T.astype(BF16)


def _score_stage(k_ref, kb, qt_scr, s_scr, with_max=False):
    tile_max = []
    for blk in range(qt_scr.shape[0]):
        both = jnp.dot(k_ref[0, kb, blk], qt_scr[blk], preferred_element_type=F32)
        for half in range(2):
            s = both[half * KV_TILE:(half + 1) * KV_TILE]
            s_scr[2 * blk + half] = s
            if with_max:
                tile_max.append(jnp.max(s, axis=0, keepdims=True))
    return tuple(tile_max)


def _attention_scratch(streams, tq, value_rows):
    return [pltpu.VMEM((streams // 2, LANES, tq), BF16),
            pltpu.VMEM((streams, KV_TILE, tq), F32),
            pltpu.VMEM((streams, KV_TILE, tq), BF16),
            pltpu.VMEM((streams, value_rows, tq), F32)]


def _sb_kernel(q_ref, k_ref, vt_ref, g_ref, o_ref, qt_scr, s_scr, w_scr, acc_scr, *, tq):
    qi = pl.program_id(2)
    streams = s_scr.shape[0]
    _store_transposed_queries(q_ref, qt_scr)
    acc_scr[...] = jnp.zeros(acc_scr.shape, F32)

    r = lax.broadcasted_iota(jnp.int32, (KV_TILE, tq), 0)
    c = lax.broadcasted_iota(jnp.int32, (KV_TILE, tq), 1)
    causal = ((r % SUBLANES) * SEGMENT + r // SUBLANES) < c
    srow = lax.broadcasted_iota(jnp.int32, (SUBLANES, tq), 0)

    def weight_stage(later, masked):
        new_later = []
        for g in range(streams):
            run = jnp.ones((SUBLANES, tq), F32)
            for v in reversed(range(SEGMENT)):
                rows = slice(v * SUBLANES, (v + 1) * SUBLANES)
                half_tanh = 0.5 * jnp.tanh(s_scr[g, rows, :])
                beta = 0.5 + half_tanh
                rest = 0.5 - half_tanh
                if masked:
                    beta = jnp.where(causal[rows], beta, 0.0)
                    rest = jnp.where(causal[rows], rest, 1.0)
                s_scr[g, rows, :] = beta * run
                run = run * rest
            off = jnp.broadcast_to(later[g], (SUBLANES, tq))
            for s2 in range(1, SUBLANES):
                off = off * jnp.where(srow < s2, run[s2:s2 + 1, :], 1.0)
            off2 = jnp.concatenate([off, off], axis=0).astype(BF16)
            for v in range(SEGMENT // 2):
                rows = slice(2 * v * SUBLANES, 2 * (v + 1) * SUBLANES)
                w_scr[g, rows, :] = s_scr[g, rows, :].astype(BF16) * off2
            new_later.append((off * run)[0:1, :])
        return tuple(new_later)

    def value_stage(t):
        vt_tile = vt_ref[0, qi - t]
        for g in range(streams):
            acc_scr[g] += jnp.dot(vt_tile[g * HEAD_DIM:(g + 1) * HEAD_DIM, :], w_scr[g],
                                  preferred_element_type=F32)

    _score_stage(k_ref, qi, qt_scr, s_scr)
    later = weight_stage(tuple(jnp.ones((1, tq), F32) for _ in range(streams)), True)
    _score_stage(k_ref, jnp.maximum(qi - 1, 0), qt_scr, s_scr)

    def step(j, later):
        value_stage(j - 1)
        later = weight_stage(later, False)
        _score_stage(k_ref, qi - (j + 1), qt_scr, s_scr)
        return later

    later = lax.fori_loop(1, qi, step, later)

    @pl.when(qi > 0)
    def _():
        value_stage(qi - 1)
        weight_stage(later, False)

    value_stage(qi)
    out_t = acc_scr[...].reshape(streams * HEAD_DIM, tq)
    o_ref[...] = (out_t.T * _silu(g_ref[...].astype(F32))).astype(BF16)


def _sb_attention(y, k_tiles, vt_tiles, *, batch, seq, q0, g0, width):
    nq = seq // Q_TILE
    nkb = seq // KV_TILE
    cols = SB_STREAMS * HEAD_DIM
    return pl.pallas_call(
        functools.partial(_sb_kernel, tq=Q_TILE),
        grid=(batch, width // cols, nq),
        in_specs=[pl.BlockSpec((Q_TILE, cols), lambda b, p, i: (b * nq + i, q0 // cols + p)),
                  pl.BlockSpec((1, nkb, cols // LANES, 2 * KV_TILE, LANES),
                               lambda b, p, i: (b, 0, p, 0, 0)),
                  pl.BlockSpec((1, nkb, cols, KV_TILE), lambda b, p, i: (b, 0, p, 0)),
                  pl.BlockSpec((Q_TILE, cols), lambda b, p, i: (b * nq + i, g0 // cols + p))],
        out_specs=pl.BlockSpec((Q_TILE, cols), lambda b, p, i: (b * nq + i, p)),
        out_shape=jax.ShapeDtypeStruct((batch * seq, width), BF16),
        scratch_shapes=_attention_scratch(SB_STREAMS, Q_TILE, HEAD_DIM),
        compiler_params=_params(3),
        name="stick_breaking_attention",
    )(y, k_tiles, vt_tiles, y)


def _diff_kernel(q_ref, k_ref, vt_ref, g_ref, lq1_ref, lk1_ref, lq2_ref, lk2_ref, sg_ref,
                 o_ref, qt_scr, s_scr, p_scr, acc_scr, *, tq, lambda_init):
    qi = pl.program_id(2)
    streams = s_scr.shape[0]
    _store_transposed_queries(q_ref, qt_scr)
    acc_scr[...] = jnp.zeros(acc_scr.shape, F32)
    r = lax.broadcasted_iota(jnp.int32, (KV_TILE, tq), 0)
    c = lax.broadcasted_iota(jnp.int32, (KV_TILE, tq), 1)
    causal = r <= c
    vdim = 2 * HEAD_DIM
    vrows = vdim + SUM_ROWS

    def softmax_stage(stats, tile_max, masked):
        new = []
        for g in range(streams):
            m, _ = stats[g]
            s = s_scr[g]
            if masked:
                s = jnp.where(causal, s, -jnp.inf)
                m_new = jnp.maximum(m, jnp.max(s, axis=0, keepdims=True))
            else:
                m_new = jnp.maximum(m, tile_max[g])
            p = jnp.exp2(s - m_new)
            alpha = jnp.exp2(m - m_new)
            p_scr[g] = p.astype(BF16)
            new.append((m_new, alpha))
        return tuple(new)

    def value_stage(t, stats):
        vt_tile = vt_ref[0, qi - t]
        for g in range(streams):
            hd = g // 2
            acc_scr[g] = stats[g][1] * acc_scr[g] + jnp.dot(
                vt_tile[hd * vrows:(hd + 1) * vrows, :], p_scr[g], preferred_element_type=F32)

    init = tuple((jnp.full((1, tq), -jnp.inf, F32), jnp.zeros((1, tq), F32))
                 for _ in range(streams))
    _score_stage(k_ref, qi, qt_scr, s_scr)
    stats = softmax_stage(init, None, True)
    tile_max = _score_stage(k_ref, jnp.maximum(qi - 1, 0), qt_scr, s_scr, True)

    def step(j, carry):
        stats, tile_max = carry
        value_stage(j - 1, stats)
        stats = softmax_stage(stats, tile_max, False)
        return stats, _score_stage(k_ref, qi - (j + 1), qt_scr, s_scr, True)

    stats, tile_max = lax.fori_loop(1, qi, step, (stats, tile_max))

    def last_step(stats):
        value_stage(qi - 1, stats)
        return softmax_stage(stats, tile_max, False)

    stats = lax.cond(qi > 0, last_step, lambda st: st, stats)
    value_stage(qi, stats)

    lam = (jnp.exp(jnp.sum(lq1_ref[...] * lk1_ref[...], axis=-1, keepdims=True))
           - jnp.exp(jnp.sum(lq2_ref[...] * lk2_ref[...], axis=-1, keepdims=True))
           + lambda_init)
    for hd in range(streams // 2):
        acc1, acc2 = acc_scr[2 * hd], acc_scr[2 * hd + 1]
        out_t = (acc1[:vdim] / acc1[vdim:vdim + 1]
                 - lam * (acc2[:vdim] / acc2[vdim:vdim + 1]))
        ms = jnp.mean(out_t * out_t, axis=0, keepdims=True)
        out = (out_t * lax.rsqrt(ms + EPS)).T
        out = out * sg_ref[...] * (1.0 - lambda_init)
        cols = slice(hd * vdim, (hd + 1) * vdim)
        o_ref[:, cols] = (out * _silu(g_ref[:, cols].astype(F32))).astype(BF16)


def _diff_attention(y, k_tiles, vt_tiles, lq1, lk1, lq2, lk2, subln_g, *, batch, seq, width,
                    lambda_init):
    nq = seq // Q_TILE
    nkb = seq // KV_TILE
    cols = DIFF_STREAMS * HEAD_DIM
    g_col = width // cols
    vt_rows = DIFF_STREAMS // 2 * (2 * HEAD_DIM + SUM_ROWS)
    vec = lambda a: a.reshape(1, -1)
    small = lambda n: pl.BlockSpec((1, n), lambda b, h, i: (0, 0))
    return pl.pallas_call(
        functools.partial(_diff_kernel, tq=Q_TILE, lambda_init=lambda_init),
        grid=(batch, width // cols, nq),
        in_specs=[pl.BlockSpec((Q_TILE, cols), lambda b, h, i: (b * nq + i, h)),
                  pl.BlockSpec((1, nkb, cols // LANES, 2 * KV_TILE, LANES),
                               lambda b, h, i: (b, 0, h, 0, 0)),
                  pl.BlockSpec((1, nkb, vt_rows, KV_TILE), lambda b, h, i: (b, 0, h, 0)),
                  pl.BlockSpec((Q_TILE, cols), lambda b, h, i: (b * nq + i, g_col + h)),
                  small(HEAD_DIM), small(HEAD_DIM), small(HEAD_DIM), small(HEAD_DIM),
                  small(2 * HEAD_DIM)],
        out_specs=pl.BlockSpec((Q_TILE, cols), lambda b, h, i: (b * nq + i, h)),
        out_shape=jax.ShapeDtypeStruct((batch * seq, width), BF16),
        scratch_shapes=_attention_scratch(DIFF_STREAMS, Q_TILE, 2 * HEAD_DIM + SUM_ROWS),
        compiler_params=_params(3),
        name="differential_attention",
    )(y, k_tiles, vt_tiles, y, vec(lq1), vec(lk1), vec(lq2), vec(lk2), vec(subln_g))


def _post_kernel(*refs, final):
    if final:
        (h_ref, ma_ref, mb_ref, p_ref, woa_ref, wob_ref, gp_ref, wg_ref, wp_ref,
         fn_ref, o_ref) = refs
    else:
        h_ref, ma_ref, mb_ref, p_ref, woa_ref, wob_ref, gp_ref, wg_ref, wp_ref, o_ref = refs
    h = (h_ref[...]
         + jnp.dot(ma_ref[...], woa_ref[...], preferred_element_type=F32)
         + jnp.dot(mb_ref[...], wob_ref[...], preferred_element_type=F32))
    ms = jnp.mean(h * h, axis=-1, keepdims=True)
    hn = (h * lax.rsqrt(ms + EPS) * gp_ref[...]).astype(BF16)
    gate = _sigmoid(jnp.dot(hn, wg_ref[...], preferred_element_type=F32))
    ple = jnp.dot(p_ref[...].astype(BF16), wp_ref[...], preferred_element_type=F32)
    h = h + gate * ple
    if final:
        ms = jnp.mean(h * h, axis=-1, keepdims=True)
        h = h * lax.rsqrt(ms + EPS) * fn_ref[...]
    o_ref[...] = h


def _post(h, mix_a, mix_b, b_col, p, w_out, g_ple, w_gate, w_proj, final_norm=None, *, tm=512):
    t, d = h.shape
    half = w_out.shape[0] // 2
    pd = p.shape[1]
    final = final_norm is not None
    const = lambda shape: pl.BlockSpec(shape, lambda i: (0, 0))
    in_specs = [pl.BlockSpec((tm, d), lambda i: (i, 0)),
                pl.BlockSpec((tm, half), lambda i: (i, 0)),
                pl.BlockSpec((tm, half), lambda i: (i, b_col)),
                pl.BlockSpec((tm, pd), lambda i: (i, 0)),
                pl.BlockSpec((half, d), lambda i: (0, 0)),
                pl.BlockSpec((half, d), lambda i: (1, 0)),
                const((1, d)), const((d, d)), const((pd, d))]
    args = [h, mix_a, mix_b, p, w_out, w_out, g_ple.reshape(1, d), w_gate, w_proj]
    if final:
        in_specs.append(const((1, d)))
        args.append(final_norm.reshape(1, d))
    return pl.pallas_call(
        functools.partial(_post_kernel, final=final),
        grid=(t // tm,),
        in_specs=in_specs,
        out_specs=pl.BlockSpec((tm, d), lambda i: (i, 0)),
        out_shape=jax.ShapeDtypeStruct((t, d), F32),
        compiler_params=_params(1),
        name="out_proj_ple_final" if final else "out_proj_ple",
    )(*args)


def _block_diag(w):
    nb, bs, _ = w.shape
    eye = jnp.eye(nb, dtype=w.dtype)
    return jnp.einsum('ncd,nm->ncmd', w, eye).reshape(nb * bs, nb * bs)


def kernel(x, p, positions, norm_mix, norm_ple, w_ple_gate, w_ple_proj, w_in_e, conv_w, conv_b,
           lru_wa, lru_ba, lru_wx, lru_bx, lru_lambda, w_out_e, w_in_o, lam_q1, lam_k1, lam_q2,
           lam_k2, subln_g, w_out_o, final_norm):
    batch, seq, d = x.shape
    depth = norm_mix.shape[0]
    t = batch * seq
    lru_w = conv_w.shape[-1]
    sb_w = (w_in_e.shape[-1] - 2 * lru_w) // 4
    diff_w = w_in_o.shape[-1] // 4

    half = HEAD_DIM // 2
    inv_freq = ROPE_THETA ** (-jnp.arange(0, HEAD_DIM, 2, dtype=F32) / HEAD_DIM)
    ang = positions.astype(F32).reshape(t, 1) * inv_freq
    cos = jnp.tile(jnp.cos(ang), (1, LANES // half))
    sin = jnp.tile(jnp.concatenate([-jnp.sin(ang), jnp.sin(ang)], axis=1), (1, LANES // HEAD_DIM))

    h = x.reshape(t, d)
    for i in range(depth):
        j = i // 2
        last = i == depth - 1
        if i % 2 == 0:
            w = w_in_e[j].astype(BF16)
            q0 = 2 * lru_w
            k0, v0, g0 = q0 + sb_w, q0 + 2 * sb_w, q0 + 3 * sb_w
            w_main = jnp.concatenate([w[:, q0:k0], w[:, g0:]], axis=1)
            w_gates = jnp.concatenate([_block_diag(lru_wa[j]), _block_diag(lru_wx[j])], axis=1)
            b_gates = jnp.concatenate([lru_ba[j], lru_bx[j]])
            lru = (w[:, :q0], conv_w[j], conv_b[j], w_gates.astype(BF16), b_gates, lru_lambda[j])
            y, k_tiles, vt_tiles, oa = _in_proj(
                h, norm_mix[i], w_main, w[:, k0:v0], w[:, v0:g0], lru=lru, batch=batch, seq=seq,
                q_range=(0, sb_w), scale=0.5 * HEAD_DIM ** -0.5, permute_keys=True)
            ob = _sb_attention(y, k_tiles, vt_tiles, batch=batch, seq=seq,
                               q0=0, g0=sb_w, width=sb_w)
            mix_a, mix_b, b_col, w_out = oa, ob, 0, w_out_e[j]
        else:
            lambda_init = 0.8 - 0.6 * math.exp(-0.3 * i)
            w = w_in_o[j].astype(BF16)
            w_main = jnp.concatenate([w[:, :diff_w], w[:, 3 * diff_w:]], axis=1)
            y, k_tiles, vt_tiles = _in_proj(
                h, norm_mix[i], w_main, w[:, diff_w:2 * diff_w], w[:, 2 * diff_w:3 * diff_w],
                cos, sin, batch=batch, seq=seq, q_range=(0, diff_w),
                scale=HEAD_DIM ** -0.5 * math.log2(math.e), permute_keys=False,
                value_head=2 * HEAD_DIM)
            o = _diff_attention(y, k_tiles, vt_tiles, lam_q1[j], lam_k1[j], lam_q2[j], lam_k2[j],
                                subln_g[j], batch=batch, seq=seq, width=diff_w,
                                lambda_init=lambda_init)
            mix_a, mix_b, b_col, w_out = o, o, 1, w_out_o[j]
        h = _post(h, mix_a, mix_b, b_col, p[i].reshape(t, -1), w_out.astype(BF16), norm_ple[i],
                  w_ple_gate[i].astype(BF16), w_ple_proj[i].astype(BF16),
                  final_norm if last else None)
    return h.reshape(batch, seq, d)
```

```python
import functools
import math

import jax
import jax.numpy as jnp
from jax import lax
from jax.experimental import pallas as pl
from jax.experimental.pallas import tpu as pltpu

F32 = jnp.float32
BF16 = jnp.bfloat16

EPS = 1e-6
HEAD_DIM = 64
LRU_C = 8.0
CONV_WIDTH = 4
ROPE_THETA = 10000.0

LANES = 128
SUBLANES = 8
KV_TILE = 256
Q_TILE = 256
SEGMENT = KV_TILE // SUBLANES
SB_STREAMS = 8
DIFF_STREAMS = 8
SUM_ROWS = 16
VMEM_LIMIT = 48 * 1024 * 1024


def _params(n_axes):
    return pltpu.CompilerParams(
        dimension_semantics=("arbitrary",) * n_axes, vmem_limit_bytes=VMEM_LIMIT)


def _sigmoid(x):
    return 0.5 + 0.5 * jnp.tanh(0.5 * x)


def _silu(x):
    return x * _sigmoid(x)


def _rotary(y, cos, sin, first_half):
    width = y.shape[1]
    partner = jnp.where(first_half, pltpu.roll(y, width - HEAD_DIM // 2, 1),
                        pltpu.roll(y, HEAD_DIM // 2, 1))
    return y * cos + partner * sin


def _in_proj_kernel(*refs, chunk, rope, q_range, scale, permute_keys, lru_steps, value_head):
    x_ref, g_ref, wm_ref, wk_ref, wv_ref = refs[:5]
    if rope:
        cos_ref, sin_ref = refs[5:7]
    if lru_steps:
        wl_ref, cw_ref, cb_ref, wg_ref, bg_ref, lam_ref = refs[5:11]
        m_ref, k_ref, vt_ref, oa_ref, ext_ref, h_ref = refs[11:]
    else:
        m_ref, k_ref, vt_ref = refs[-3:]
    tm = x_ref.shape[0]
    x = x_ref[...]
    ms = jnp.mean(x * x, axis=-1, keepdims=True)
    xn = x * lax.rsqrt(ms + EPS) * g_ref[...]
    xn_bf = xn.astype(BF16)
    if rope:
        reps = chunk // LANES
        cos = jnp.concatenate([cos_ref[...]] * reps, axis=1)
        sin = jnp.concatenate([sin_ref[...]] * reps, axis=1)
        lane = lax.broadcasted_iota(jnp.int32, (tm, chunk), 1)
        first_half = (lane % HEAD_DIM) < (HEAD_DIM // 2)

    lru_tasks = []
    if lru_steps:
        width = oa_ref.shape[1]

        @pl.when(pl.program_id(0) % lru_steps == 0)
        def _():
            ext_ref[0:SUBLANES, :] = jnp.zeros((SUBLANES, width), F32)
            h_ref[...] = jnp.zeros((1, width), F32)

        xa = jnp.dot(xn_bf, wl_ref[:, :width], preferred_element_type=F32)
        ga = jnp.dot(xn_bf, wl_ref[:, width:], preferred_element_type=F32)
        xc = _causal_conv(xa, cw_ref, cb_ref, ext_ref)

    def main_chunk(c):
        lo = c * chunk
        y = jnp.dot(xn_bf, wm_ref[:, lo:lo + chunk], preferred_element_type=F32)
        if q_range[0] <= lo < q_range[1]:
            if rope:
                y = _rotary(y, cos, sin, first_half)
            y = y * scale
        m_ref[:, lo:lo + chunk] = y.astype(BF16)

    n_tiles = tm // KV_TILE
    if permute_keys:
        r = lax.broadcasted_iota(jnp.int32, (KV_TILE, KV_TILE), 0)
        c = lax.broadcasted_iota(jnp.int32, (KV_TILE, KV_TILE), 1)
        perm = jnp.where(c == (r % SUBLANES) * SEGMENT + r // SUBLANES, 1.0, 0.0).astype(BF16)
        xk = jnp.concatenate(
            [jnp.dot(perm, xn_bf[t * KV_TILE:(t + 1) * KV_TILE, :],
                     preferred_element_type=F32).astype(BF16) for t in range(n_tiles)], axis=0)
    else:
        xk = xn_bf

    low = lax.broadcasted_iota(jnp.int32, (KV_TILE, LANES), 1) < HEAD_DIM

    def key_chunk(c):
        kk = jnp.dot(xk, wk_ref[:, c * chunk:(c + 1) * chunk], preferred_element_type=F32)
        if rope:
            kk = _rotary(kk, cos, sin, first_half)
        for t in range(n_tiles):
            for b in range(chunk // LANES):
                blk = kk[t * KV_TILE:(t + 1) * KV_TILE, b * LANES:(b + 1) * LANES]
                nb = c * (chunk // LANES) + b
                k_ref[0, t, nb, 0:KV_TILE, :] = jnp.where(low, blk, 0.0).astype(BF16)
                k_ref[0, t, nb, KV_TILE:2 * KV_TILE, :] = jnp.where(low, 0.0, blk).astype(BF16)
    def value_chunk(c):
        vv = jnp.dot(xk, wv_ref[:, c * chunk:(c + 1) * chunk], preferred_element_type=F32)
        for t in range(n_tiles):
            vt = vv[t * KV_TILE:(t + 1) * KV_TILE, :].T.astype(BF16)
            if not value_head:
                vt_ref[0, t, c * chunk:(c + 1) * chunk, :] = vt
                continue
            for hh in range(chunk // value_head):
                r0 = (c * (chunk // value_head) + hh) * (value_head + SUM_ROWS)
                vt_ref[0, t, r0:r0 + value_head, :] = vt[hh * value_head:(hh + 1) * value_head]
                vt_ref[0, t, r0 + value_head:r0 + value_head + SUM_ROWS, :] = jnp.ones(
                    (SUM_ROWS, KV_TILE), BF16)

    for c in range(wm_ref.shape[1] // chunk):
        main_chunk(c)
    mxu_tasks = ([functools.partial(key_chunk, c) for c in range(wk_ref.shape[1] // chunk)]
                 + [functools.partial(value_chunk, c) for c in range(wv_ref.shape[1] // chunk)])
    if lru_steps:
        gates = jnp.dot(xc.astype(BF16), wg_ref[...], preferred_element_type=F32) + bg_ref[...]
        ts = tm // 4

        def lru_rows(r0):
            rows = slice(r0, r0 + ts)
            _rg_lru_rows(xc[rows], gates[rows], ga[rows], lam_ref, oa_ref, r0, h_ref)

        lru_tasks = [functools.partial(lru_rows, r0) for r0 in range(0, tm, ts)]
    for n in range(max(len(mxu_tasks), len(lru_tasks))):
        for tasks in (mxu_tasks, lru_tasks):
            if n < len(tasks):
                tasks[n]()


def _in_proj(h, g, w_main, w_k, w_v, cos=None, sin=None, lru=None, *, batch, seq, q_range,
             scale, permute_keys, value_head=0, tm=512, chunk=512):
    t, d = h.shape
    n_main, n_k, n_v = w_main.shape[1], w_k.shape[1], w_v.shape[1]
    vt_rows = n_v // value_head * (value_head + SUM_ROWS) if value_head else n_v
    rope = cos is not None
    assert not (rope and lru is not None)
    steps = seq // tm
    tiles = tm // KV_TILE
    nkb = seq // KV_TILE
    const = lambda shape: pl.BlockSpec(shape, lambda i: (0, 0))
    in_specs = [pl.BlockSpec((tm, d), lambda i: (i, 0)), const((1, d)),
                const((d, n_main)), const((d, n_k)), const((d, n_v))]
    args = [h, g.reshape(1, d), w_main, w_k, w_v]
    if rope:
        in_specs += [pl.BlockSpec((tm, LANES), lambda i: (i, 0))] * 2
        args += [cos, sin]
    out_specs = [pl.BlockSpec((tm, n_main), lambda i: (i, 0)),
                 pl.BlockSpec((1, tiles, n_k // LANES, 2 * KV_TILE, LANES),
                              lambda i: (i // steps, i % steps, 0, 0, 0)),
                 pl.BlockSpec((1, tiles, vt_rows, KV_TILE),
                              lambda i: (i // steps, i % steps, 0, 0))]
    out_shape = [jax.ShapeDtypeStruct((t, n_main), BF16),
                 jax.ShapeDtypeStruct((batch, nkb, n_k // LANES, 2 * KV_TILE, LANES), BF16),
                 jax.ShapeDtypeStruct((batch, nkb, vt_rows, KV_TILE), BF16)]
    scratch = []
    if lru is not None:
        w_lru, conv_w, conv_b, w_gates, b_gates, lam = lru
        width = conv_w.shape[1]
        in_specs += [const((d, 2 * width)), const((CONV_WIDTH, width)), const((1, width)),
                     const((width, 2 * width)), const((1, 2 * width)), const((1, width))]
        args += [w_lru, conv_w, conv_b.reshape(1, width), w_gates,
                 b_gates.reshape(1, 2 * width), lam.reshape(1, width)]
        out_specs.append(pl.BlockSpec((tm, width), lambda i: (i, 0)))
        out_shape.append(jax.ShapeDtypeStruct((t, width), BF16))
        scratch = [pltpu.VMEM((tm + SUBLANES, width), F32), pltpu.VMEM((1, width), F32)]
    return pl.pallas_call(
        functools.partial(_in_proj_kernel, chunk=chunk, rope=rope, q_range=q_range,
                          scale=scale, permute_keys=permute_keys, value_head=value_head,
                          lru_steps=steps if lru is not None else 0),
        grid=(t // tm,),
        in_specs=in_specs,
        out_specs=out_specs,
        out_shape=out_shape,
        scratch_shapes=scratch,
        compiler_params=_params(1),
        name="in_proj_rope" if rope else "in_proj_lru",
    )(*args)


def _causal_conv(xa, cw_ref, cb_ref, ext_ref):
    ts = xa.shape[0]
    ext_ref[SUBLANES:SUBLANES + ts, :] = xa
    xc = cb_ref[...] + cw_ref[CONV_WIDTH - 1:CONV_WIDTH, :] * xa
    for j in range(1, CONV_WIDTH):
        k = CONV_WIDTH - 1 - j
        xc = xc + cw_ref[k:k + 1, :] * ext_ref[SUBLANES - j:SUBLANES - j + ts, :]
    ext_ref[0:SUBLANES, :] = ext_ref[ts:ts + SUBLANES, :]
    return xc


def _rg_lru_rows(xc, gates, ga, lam_ref, o_ref, row0, h_ref):
    ts, width = xc.shape
    r = _sigmoid(gates[:, :width])
    i = _sigmoid(gates[:, width:])
    lam = lam_ref[...]
    log_sig_lam = jnp.minimum(lam, 0.0) - jnp.log(1.0 + jnp.exp(-jnp.abs(lam)))
    log_a = LRU_C * r * log_sig_lam
    a = jnp.exp(log_a)
    u = jnp.sqrt(1.0 - a * a) * (i * xc)

    groups = ts // SUBLANES
    a = a.reshape(groups, SUBLANES, width)
    u = u.reshape(groups, SUBLANES, width)
    row = lax.broadcasted_iota(jnp.int32, (groups, SUBLANES, width), 1)
    step = 1
    while step < SUBLANES:
        valid = row >= step
        a_prev = pltpu.roll(a, step, 1)
        u_prev = pltpu.roll(u, step, 1)
        u = jnp.where(valid, a * u_prev + u, u)
        a = jnp.where(valid, a * a_prev, a)
        step *= 2
    gate = _silu(ga)
    state = h_ref[...]
    for pair in range(ts // (2 * SUBLANES)):
        hs = []
        for grp in (2 * pair, 2 * pair + 1):
            h = u[grp] + a[grp] * state
            state = h[SUBLANES - 1:SUBLANES, :]
            hs.append(h)
        rows = slice(2 * pair * SUBLANES, 2 * (pair + 1) * SUBLANES)
        o_ref[row0 + rows.start:row0 + rows.stop, :] = (
            jnp.concatenate(hs, axis=0) * gate[rows]).astype(BF16)
    h_ref[...] = state


def _store_transposed_queries(q_ref, qt_scr):
    for blk in range(qt_scr.shape[0]):
        qt_scr[blk] = q_ref[:, blk * LANES:(blk + 1) * LANES].astype(F32).T.astype(BF16)


def _score_stage(k_ref, kb, qt_scr, s_scr, with_max=False):
    tile_max = []
    for blk in range(qt_scr.shape[0]):
        both = jnp.dot(k_ref[0, kb, blk], qt_scr[blk], preferred_element_type=F32)
        for half in range(2):
            s = both[half * KV_TILE:(half + 1) * KV_TILE]
            s_scr[2 * blk + half] = s
            if with_max:
                tile_max.append(jnp.max(s, axis=0, keepdims=True))
    return tuple(tile_max)


def _attention_scratch(streams, tq, value_rows):
    return [pltpu.VMEM((streams // 2, LANES, tq), BF16),
            pltpu.VMEM((streams, KV_TILE, tq), F32),
            pltpu.VMEM((streams, KV_TILE, tq), BF16),
            pltpu.VMEM((streams, value_rows, tq), F32)]


def _sb_kernel(q_ref, k_ref, vt_ref, g_ref, o_ref, qt_scr, s_scr, w_scr, acc_scr, *, tq):
    qi = pl.program_id(2)
    streams = s_scr.shape[0]
    _store_transposed_queries(q_ref, qt_scr)
    acc_scr[...] = jnp.zeros(acc_scr.shape, F32)

    r = lax.broadcasted_iota(jnp.int32, (KV_TILE, tq), 0)
    c = lax.broadcasted_iota(jnp.int32, (KV_TILE, tq), 1)
    causal = ((r % SUBLANES) * SEGMENT + r // SUBLANES) < c
    srow = lax.broadcasted_iota(jnp.int32, (SUBLANES, tq), 0)

    def weight_stage(later, masked):
        new_later = []
        for g in range(streams):
            run = jnp.ones((SUBLANES, tq), F32)
            for v in reversed(range(SEGMENT)):
                rows = slice(v * SUBLANES, (v + 1) * SUBLANES)
                half_tanh = 0.5 * jnp.tanh(s_scr[g, rows, :])
                beta = 0.5 + half_tanh
                rest = 0.5 - half_tanh
                if masked:
                    beta = jnp.where(causal[rows], beta, 0.0)
                    rest = jnp.where(causal[rows], rest, 1.0)
                s_scr[g, rows, :] = beta * run
                run = run * rest
            off = jnp.broadcast_to(later[g], (SUBLANES, tq))
            for s2 in range(1, SUBLANES):
                off = off * jnp.where(srow < s2, run[s2:s2 + 1, :], 1.0)
            off2 = jnp.concatenate([off, off], axis=0)
            for v in range(SEGMENT // 2):
                rows = slice(2 * v * SUBLANES, 2 * (v + 1) * SUBLANES)
                w_scr[g, rows, :] = (s_scr[g, rows, :] * off2).astype(BF16)
            new_later.append((off * run)[0:1, :])
        return tuple(new_later)

    def value_stage(t):
        vt_tile = vt_ref[0, qi - t]
        for g in range(streams):
            acc_scr[g] += jnp.dot(vt_tile[g * HEAD_DIM:(g + 1) * HEAD_DIM, :], w_scr[g],
                                  preferred_element_type=F32)

    _score_stage(k_ref, qi, qt_scr, s_scr)
    later = weight_stage(tuple(jnp.ones((1, tq), F32) for _ in range(streams)), True)
    _score_stage(k_ref, jnp.maximum(qi - 1, 0), qt_scr, s_scr)

    def step(j, later):
        value_stage(j - 1)
        later = weight_stage(later, False)
        _score_stage(k_ref, qi - (j + 1), qt_scr, s_scr)
        return later

    later = lax.fori_loop(1, qi, step, later)

    @pl.when(qi > 0)
    def _():
        value_stage(qi - 1)
        weight_stage(later, False)

    value_stage(qi)
    out_t = acc_scr[...].reshape(streams * HEAD_DIM, tq)
    o_ref[...] = (out_t.T * _silu(g_ref[...].astype(F32))).astype(BF16)


def _sb_attention(y, k_tiles, vt_tiles, *, batch, seq, q0, g0, width):
    nq = seq // Q_TILE
    nkb = seq // KV_TILE
    cols = SB_STREAMS * HEAD_DIM
    return pl.pallas_call(
        functools.partial(_sb_kernel, tq=Q_TILE),
        grid=(batch, width // cols, nq),
        in_specs=[pl.BlockSpec((Q_TILE, cols), lambda b, p, i: (b * nq + i, q0 // cols + p)),
                  pl.BlockSpec((1, nkb, cols // LANES, 2 * KV_TILE, LANES),
                               lambda b, p, i: (b, 0, p, 0, 0)),
                  pl.BlockSpec((1, nkb, cols, KV_TILE), lambda b, p, i: (b, 0, p, 0)),
                  pl.BlockSpec((Q_TILE, cols), lambda b, p, i: (b * nq + i, g0 // cols + p))],
        out_specs=pl.BlockSpec((Q_TILE, cols), lambda b, p, i: (b * nq + i, p)),
        out_shape=jax.ShapeDtypeStruct((batch * seq, width), BF16),
        scratch_shapes=_attention_scratch(SB_STREAMS, Q_TILE, HEAD_DIM),
        compiler_params=_params(3),
        name="stick_breaking_attention",
    )(y, k_tiles, vt_tiles, y)


def _diff_kernel(q_ref, k_ref, vt_ref, g_ref, lq1_ref, lk1_ref, lq2_ref, lk2_ref, sg_ref,
                 o_ref, qt_scr, s_scr, p_scr, acc_scr, *, tq, lambda_init):
    qi = pl.program_id(2)
    streams = s_scr.shape[0]
    _store_transposed_queries(q_ref, qt_scr)
    acc_scr[...] = jnp.zeros(acc_scr.shape, F32)
    r = lax.broadcasted_iota(jnp.int32, (KV_TILE, tq), 0)
    c = lax.broadcasted_iota(jnp.int32, (KV_TILE, tq), 1)
    causal = r <= c
    vdim = 2 * HEAD_DIM
    vrows = vdim + SUM_ROWS

    def softmax_stage(stats, tile_max, masked):
        new = []
        for g in range(streams):
            m, _ = stats[g]
            s = s_scr[g]
            if masked:
                s = jnp.where(causal, s, -jnp.inf)
                m_new = jnp.maximum(m, jnp.max(s, axis=0, keepdims=True))
            else:
                m_new = jnp.maximum(m, tile_max[g])
            p = jnp.exp2(s - m_new)
            alpha = jnp.exp2(m - m_new)
            p_scr[g] = p.astype(BF16)
            new.append((m_new, alpha))
        return tuple(new)

    def value_stage(t, stats):
        vt_tile = vt_ref[0, qi - t]
        for g in range(streams):
            hd = g // 2
            acc_scr[g] = stats[g][1] * acc_scr[g] + jnp.dot(
                vt_tile[hd * vrows:(hd + 1) * vrows, :], p_scr[g], preferred_element_type=F32)

    init = tuple((jnp.full((1, tq), -jnp.inf, F32), jnp.zeros((1, tq), F32))
                 for _ in range(streams))
    _score_stage(k_ref, qi, qt_scr, s_scr)
    stats = softmax_stage(init, None, True)
    tile_max = _score_stage(k_ref, jnp.maximum(qi - 1, 0), qt_scr, s_scr, True)

    def step(j, carry):
        stats, tile_max = carry
        value_stage(j - 1, stats)
        stats = softmax_stage(stats, tile_max, False)
        return stats, _score_stage(k_ref, qi - (j + 1), qt_scr, s_scr, True)

    stats, tile_max = lax.fori_loop(1, qi, step, (stats, tile_max))

    def last_step(stats):
        value_stage(qi - 1, stats)
        return softmax_stage(stats, tile_max, False)

    stats = lax.cond(qi > 0, last_step, lambda st: st, stats)
    value_stage(qi, stats)

    lam = (jnp.exp(jnp.sum(lq1_ref[...] * lk1_ref[...], axis=-1, keepdims=True))
           - jnp.exp(jnp.sum(lq2_ref[...] * lk2_ref[...], axis=-1, keepdims=True))
           + lambda_init)
    head_gain = sg_ref[...] * (1.0 - lambda_init)
    for hd in range(streams // 2):
        acc1, acc2 = acc_scr[2 * hd], acc_scr[2 * hd + 1]
        inv1 = 1.0 / acc1[vdim:vdim + 1]
        inv2 = lam / acc2[vdim:vdim + 1]
        out_t = acc1[:vdim] * inv1 - acc2[:vdim] * inv2
        ms = jnp.mean(out_t * out_t, axis=0, keepdims=True)
        out = (out_t * lax.rsqrt(ms + EPS)).T
        cols = slice(hd * vdim, (hd + 1) * vdim)
        o_ref[:, cols] = (out * head_gain * _silu(g_ref[:, cols].astype(F32))).astype(BF16)


def _diff_attention(y, k_tiles, vt_tiles, lq1, lk1, lq2, lk2, subln_g, *, batch, seq, width,
                    lambda_init):
    nq = seq // Q_TILE
    nkb = seq // KV_TILE
    cols = DIFF_STREAMS * HEAD_DIM
    g_col = width // cols
    vt_rows = DIFF_STREAMS // 2 * (2 * HEAD_DIM + SUM_ROWS)
    vec = lambda a: a.reshape(1, -1)
    small = lambda n: pl.BlockSpec((1, n), lambda b, h, i: (0, 0))
    return pl.pallas_call(
        functools.partial(_diff_kernel, tq=Q_TILE, lambda_init=lambda_init),
        grid=(batch, width // cols, nq),
        in_specs=[pl.BlockSpec((Q_TILE, cols), lambda b, h, i: (b * nq + i, h)),
                  pl.BlockSpec((1, nkb, cols // LANES, 2 * KV_TILE, LANES),
                               lambda b, h, i: (b, 0, h, 0, 0)),
                  pl.BlockSpec((1, nkb, vt_rows, KV_TILE), lambda b, h, i: (b, 0, h, 0)),
                  pl.BlockSpec((Q_TILE, cols), lambda b, h, i: (b * nq + i, g_col + h)),
                  small(HEAD_DIM), small(HEAD_DIM), small(HEAD_DIM), small(HEAD_DIM),
                  small(2 * HEAD_DIM)],
        out_specs=pl.BlockSpec((Q_TILE, cols), lambda b, h, i: (b * nq + i, h)),
        out_shape=jax.ShapeDtypeStruct((batch * seq, width), BF16),
        scratch_shapes=_attention_scratch(DIFF_STREAMS, Q_TILE, 2 * HEAD_DIM + SUM_ROWS),
        compiler_params=_params(3),
        name="differential_attention",
    )(y, k_tiles, vt_tiles, y, vec(lq1), vec(lk1), vec(lq2), vec(lk2), vec(subln_g))


def _post_kernel(*refs, final):
    if final:
        (h_ref, ma_ref, mb_ref, p_ref, woa_ref, wob_ref, gp_ref, wg_ref, wp_ref,
         fn_ref, o_ref) = refs
    else:
        h_ref, ma_ref, mb_ref, p_ref, woa_ref, wob_ref, gp_ref, wg_ref, wp_ref, o_ref = refs
    h = (h_ref[...]
         + jnp.dot(ma_ref[...], woa_ref[...], preferred_element_type=F32)
         + jnp.dot(mb_ref[...], wob_ref[...], preferred_element_type=F32))
    ms = jnp.mean(h * h, axis=-1, keepdims=True)
    hn = (h * lax.rsqrt(ms + EPS) * gp_ref[...]).astype(BF16)
    gate = _sigmoid(jnp.dot(hn, wg_ref[...], preferred_element_type=F32))
    ple = jnp.dot(p_ref[...].astype(BF16), wp_ref[...], preferred_element_type=F32)
    h = h + gate * ple
    if final:
        ms = jnp.mean(h * h, axis=-1, keepdims=True)
        h = h * lax.rsqrt(ms + EPS) * fn_ref[...]
    o_ref[...] = h


def _post(h, mix_a, mix_b, b_col, p, w_out, g_ple, w_gate, w_proj, final_norm=None, *, tm=1024):
    t, d = h.shape
    half = w_out.shape[0] // 2
    pd = p.shape[1]
    final = final_norm is not None
    const = lambda shape: pl.BlockSpec(shape, lambda i: (0, 0))
    in_specs = [pl.BlockSpec((tm, d), lambda i: (i, 0)),
                pl.BlockSpec((tm, half), lambda i: (i, 0)),
                pl.BlockSpec((tm, half), lambda i: (i, b_col)),
                pl.BlockSpec((tm, pd), lambda i: (i, 0)),
                pl.BlockSpec((half, d), lambda i: (0, 0)),
                pl.BlockSpec((half, d), lambda i: (1, 0)),
                const((1, d)), const((d, d)), const((pd, d))]
    args = [h, mix_a, mix_b, p, w_out, w_out, g_ple.reshape(1, d), w_gate, w_proj]
    if final:
        in_specs.append(const((1, d)))
        args.append(final_norm.reshape(1, d))
    return pl.pallas_call(
        functools.partial(_post_kernel, final=final),
        grid=(t // tm,),
        in_specs=in_specs,
        out_specs=pl.BlockSpec((tm, d), lambda i: (i, 0)),
        out_shape=jax.ShapeDtypeStruct((t, d), F32),
        compiler_params=_params(1),
        name="out_proj_ple_final" if final else "out_proj_ple",
    )(*args)


def _block_diag(w):
    nb, bs, _ = w.shape
    eye = jnp.eye(nb, dtype=w.dtype)
    return jnp.einsum('ncd,nm->ncmd', w, eye).reshape(nb * bs, nb * bs)


def kernel(x, p, positions, norm_mix, norm_ple, w_ple_gate, w_ple_proj, w_in_e, conv_w, conv_b,
           lru_wa, lru_ba, lru_wx, lru_bx, lru_lambda, w_out_e, w_in_o, lam_q1, lam_k1, lam_q2,
           lam_k2, subln_g, w_out_o, final_norm):
    batch, seq, d = x.shape
    depth = norm_mix.shape[0]
    t = batch * seq
    lru_w = conv_w.shape[-1]
    sb_w = (w_in_e.shape[-1] - 2 * lru_w) // 4
    diff_w = w_in_o.shape[-1] // 4

    half = HEAD_DIM // 2
    inv_freq = ROPE_THETA ** (-jnp.arange(0, HEAD_DIM, 2, dtype=F32) / HEAD_DIM)
    ang = positions.astype(F32).reshape(t, 1) * inv_freq
    cos = jnp.tile(jnp.cos(ang), (1, LANES // half))
    sin = jnp.tile(jnp.concatenate([-jnp.sin(ang), jnp.sin(ang)], axis=1), (1, LANES // HEAD_DIM))

    h = x.reshape(t, d)
    for i in range(depth):
        j = i // 2
        last = i == depth - 1
        if i % 2 == 0:
            w = w_in_e[j].astype(BF16)
            q0 = 2 * lru_w
            k0, v0, g0 = q0 + sb_w, q0 + 2 * sb_w, q0 + 3 * sb_w
            w_main = jnp.concatenate([w[:, q0:k0], w[:, g0:]], axis=1)
            w_gates = jnp.concatenate([_block_diag(lru_wa[j]), _block_diag(lru_wx[j])], axis=1)
            b_gates = jnp.concatenate([lru_ba[j], lru_bx[j]])
            lru = (w[:, :q0], conv_w[j], conv_b[j], w_gates.astype(BF16), b_gates, lru_lambda[j])
            y, k_tiles, vt_tiles, oa = _in_proj(
                h, norm_mix[i], w_main, w[:, k0:v0], w[:, v0:g0], lru=lru, batch=batch, seq=seq,
                q_range=(0, sb_w), scale=0.5 * HEAD_DIM ** -0.5, permute_keys=True)
            ob = _sb_attention(y, k_tiles, vt_tiles, batch=batch, seq=seq,
                               q0=0, g0=sb_w, width=sb_w)
            mix_a, mix_b, b_col, w_out = oa, ob, 0, w_out_e[j]
        else:
            lambda_init = 0.8 - 0.6 * math.exp(-0.3 * i)
            w = w_in_o[j].astype(BF16)
            w_main = jnp.concatenate([w[:, :diff_w], w[:, 3 * diff_w:]], axis=1)
            y, k_tiles, vt_tiles = _in_proj(
                h, norm_mix[i], w_main, w[:, diff_w:2 * diff_w], w[:, 2 * diff_w:3 * diff_w],
                cos, sin, batch=batch, seq=seq, q_range=(0, diff_w),
                scale=HEAD_DIM ** -0.5 * math.log2(math.e), permute_keys=False,
                value_head=2 * HEAD_DIM)
            o = _diff_attention(y, k_tiles, vt_tiles, lam_q1[j], lam_k1[j], lam_q2[j], lam_k2[j],
                                subln_g[j], batch=batch, seq=seq, width=diff_w,
                                lambda_init=lambda_init)
            mix_a, mix_b, b_col, w_out = o, o, 1, w_out_o[j]
        h = _post(h, mix_a, mix_b, b_col, p[i].reshape(t, -1), w_out.astype(BF16), norm_ple[i],
                  w_ple_gate[i].astype(BF16), w_ple_proj[i].astype(BF16),
                  final_norm if last else None)
    return h.reshape(batch, seq, d)
```

```python
import functools
import math

import jax
import jax.numpy as jnp
from jax import lax
from jax.experimental import pallas as pl
from jax.experimental.pallas import tpu as pltpu

F32 = jnp.float32
BF16 = jnp.bfloat16

EPS = 1e-6
HEAD_DIM = 64
LRU_C = 8.0
CONV_WIDTH = 4
ROPE_THETA = 10000.0

LANES = 128
SUBLANES = 8
KV_TILE = 256
Q_TILE = 256
Q_BLOCKS = 4
SEGMENT = KV_TILE // SUBLANES
SB_STREAMS = 8
DIFF_STREAMS = 8
SUM_ROWS = 16
VMEM_LIMIT = 48 * 1024 * 1024


def _params(n_axes):
    return pltpu.CompilerParams(
        dimension_semantics=("arbitrary",) * n_axes, vmem_limit_bytes=VMEM_LIMIT)


def _sigmoid(x):
    return 0.5 + 0.5 * jnp.tanh(0.5 * x)


def _silu(x):
    return x * _sigmoid(x)


def _rotary(y, cos, sin, first_half):
    width = y.shape[1]
    partner = jnp.where(first_half, pltpu.roll(y, width - HEAD_DIM // 2, 1),
                        pltpu.roll(y, HEAD_DIM // 2, 1))
    return y * cos + partner * sin


def _in_proj_kernel(*refs, chunk, rope, q_range, scale, permute_keys, lru_steps, value_head):
    x_ref, g_ref, wm_ref, wk_ref, wv_ref = refs[:5]
    if rope:
        cos_ref, sin_ref = refs[5:7]
    if lru_steps:
        wl_ref, cw_ref, cb_ref, wg_ref, bg_ref, lam_ref = refs[5:11]
        m_ref, k_ref, vt_ref, oa_ref, ext_ref, h_ref = refs[11:]
    else:
        m_ref, k_ref, vt_ref = refs[-3:]
    tm = x_ref.shape[0]
    x = x_ref[...]
    ms = jnp.mean(x * x, axis=-1, keepdims=True)
    xn = x * lax.rsqrt(ms + EPS) * g_ref[...]
    xn_bf = xn.astype(BF16)
    if rope:
        reps = chunk // LANES
        cos = jnp.concatenate([cos_ref[...]] * reps, axis=1)
        sin = jnp.concatenate([sin_ref[...]] * reps, axis=1)
        lane = lax.broadcasted_iota(jnp.int32, (tm, chunk), 1)
        first_half = (lane % HEAD_DIM) < (HEAD_DIM // 2)

    lru_tasks = []
    if lru_steps:
        width = oa_ref.shape[1]

        @pl.when(pl.program_id(0) % lru_steps == 0)
        def _():
            ext_ref[0:SUBLANES, :] = jnp.zeros((SUBLANES, width), F32)
            h_ref[...] = jnp.zeros((1, width), F32)

        xa = jnp.dot(xn_bf, wl_ref[:, :width], preferred_element_type=F32)
        ga = jnp.dot(xn_bf, wl_ref[:, width:], preferred_element_type=F32)
        xc = _causal_conv(xa, cw_ref, cb_ref, ext_ref)

    def main_chunk(c):
        lo = c * chunk
        y = jnp.dot(xn_bf, wm_ref[:, lo:lo + chunk], preferred_element_type=F32)
        if q_range[0] <= lo < q_range[1]:
            if rope:
                y = _rotary(y, cos, sin, first_half)
            y = y * scale
        m_ref[:, lo:lo + chunk] = y.astype(BF16)

    n_tiles = tm // KV_TILE
    if permute_keys:
        r = lax.broadcasted_iota(jnp.int32, (KV_TILE, KV_TILE), 0)
        c = lax.broadcasted_iota(jnp.int32, (KV_TILE, KV_TILE), 1)
        perm = jnp.where(c == (r % SUBLANES) * SEGMENT + r // SUBLANES, 1.0, 0.0).astype(BF16)
        xk = jnp.concatenate(
            [jnp.dot(perm, xn_bf[t * KV_TILE:(t + 1) * KV_TILE, :],
                     preferred_element_type=F32).astype(BF16) for t in range(n_tiles)], axis=0)
    else:
        xk = xn_bf

    low = lax.broadcasted_iota(jnp.int32, (KV_TILE, LANES), 1) < HEAD_DIM

    def key_chunk(c):
        kk = jnp.dot(xk, wk_ref[:, c * chunk:(c + 1) * chunk], preferred_element_type=F32)
        if rope:
            kk = _rotary(kk, cos, sin, first_half)
        for t in range(n_tiles):
            for b in range(chunk // LANES):
                blk = kk[t * KV_TILE:(t + 1) * KV_TILE, b * LANES:(b + 1) * LANES]
                nb = c * (chunk // LANES) + b
                k_ref[0, t, nb, 0:KV_TILE, :] = jnp.where(low, blk, 0.0).astype(BF16)
                k_ref[0, t, nb, KV_TILE:2 * KV_TILE, :] = jnp.where(low, 0.0, blk).astype(BF16)
    def value_chunk(c):
        vv = jnp.dot(xk, wv_ref[:, c * chunk:(c + 1) * chunk], preferred_element_type=F32)
        for t in range(n_tiles):
            vt = vv[t * KV_TILE:(t + 1) * KV_TILE, :].T.astype(BF16)
            if not value_head:
                vt_ref[0, t, c * chunk:(c + 1) * chunk, :] = vt
                continue
            for hh in range(chunk // value_head):
                r0 = (c * (chunk // value_head) + hh) * (value_head + SUM_ROWS)
                vt_ref[0, t, r0:r0 + value_head, :] = vt[hh * value_head:(hh + 1) * value_head]
                vt_ref[0, t, r0 + value_head:r0 + value_head + SUM_ROWS, :] = jnp.ones(
                    (SUM_ROWS, KV_TILE), BF16)

    for c in range(wm_ref.shape[1] // chunk):
        main_chunk(c)
    mxu_tasks = ([functools.partial(key_chunk, c) for c in range(wk_ref.shape[1] // chunk)]
                 + [functools.partial(value_chunk, c) for c in range(wv_ref.shape[1] // chunk)])
    if lru_steps:
        gates = jnp.dot(xc.astype(BF16), wg_ref[...], preferred_element_type=F32) + bg_ref[...]
        ts = tm // 4

        def lru_rows(r0):
            rows = slice(r0, r0 + ts)
            _rg_lru_rows(xc[rows], gates[rows], ga[rows], lam_ref, oa_ref, r0, h_ref)

        lru_tasks = [functools.partial(lru_rows, r0) for r0 in range(0, tm, ts)]
    for n in range(max(len(mxu_tasks), len(lru_tasks))):
        for tasks in (mxu_tasks, lru_tasks):
            if n < len(tasks):
                tasks[n]()


def _in_proj(h, g, w_main, w_k, w_v, cos=None, sin=None, lru=None, *, batch, seq, q_range,
             scale, permute_keys, value_head=0, tm=512, chunk=512):
    t, d = h.shape
    n_main, n_k, n_v = w_main.shape[1], w_k.shape[1], w_v.shape[1]
    vt_rows = n_v // value_head * (value_head + SUM_ROWS) if value_head else n_v
    rope = cos is not None
    assert not (rope and lru is not None)
    steps = seq // tm
    tiles = tm // KV_TILE
    nkb = seq // KV_TILE
    const = lambda shape: pl.BlockSpec(shape, lambda i: (0, 0))
    in_specs = [pl.BlockSpec((tm, d), lambda i: (i, 0)), const((1, d)),
                const((d, n_main)), const((d, n_k)), const((d, n_v))]
    args = [h, g.reshape(1, d), w_main, w_k, w_v]
    if rope:
        in_specs += [pl.BlockSpec((tm, LANES), lambda i: (i, 0))] * 2
        args += [cos, sin]
    out_specs = [pl.BlockSpec((tm, n_main), lambda i: (i, 0)),
                 pl.BlockSpec((1, tiles, n_k // LANES, 2 * KV_TILE, LANES),
                              lambda i: (i // steps, i % steps, 0, 0, 0)),
                 pl.BlockSpec((1, tiles, vt_rows, KV_TILE),
                              lambda i: (i // steps, i % steps, 0, 0))]
    out_shape = [jax.ShapeDtypeStruct((t, n_main), BF16),
                 jax.ShapeDtypeStruct((batch, nkb, n_k // LANES, 2 * KV_TILE, LANES), BF16),
                 jax.ShapeDtypeStruct((batch, nkb, vt_rows, KV_TILE), BF16)]
    scratch = []
    if lru is not None:
        w_lru, conv_w, conv_b, w_gates, b_gates, lam = lru
        width = conv_w.shape[1]
        in_specs += [const((d, 2 * width)), const((CONV_WIDTH, width)), const((1, width)),
                     const((width, 2 * width)), const((1, 2 * width)), const((1, width))]
        args += [w_lru, conv_w, conv_b.reshape(1, width), w_gates,
                 b_gates.reshape(1, 2 * width), lam.reshape(1, width)]
        out_specs.append(pl.BlockSpec((tm, width), lambda i: (i, 0)))
        out_shape.append(jax.ShapeDtypeStruct((t, width), BF16))
        scratch = [pltpu.VMEM((tm + SUBLANES, width), F32), pltpu.VMEM((1, width), F32)]
    return pl.pallas_call(
        functools.partial(_in_proj_kernel, chunk=chunk, rope=rope, q_range=q_range,
                          scale=scale, permute_keys=permute_keys, value_head=value_head,
                          lru_steps=steps if lru is not None else 0),
        grid=(t // tm,),
        in_specs=in_specs,
        out_specs=out_specs,
        out_shape=out_shape,
        scratch_shapes=scratch,
        compiler_params=_params(1),
        name="in_proj_rope" if rope else "in_proj_lru",
    )(*args)


def _causal_conv(xa, cw_ref, cb_ref, ext_ref):
    ts = xa.shape[0]
    ext_ref[SUBLANES:SUBLANES + ts, :] = xa
    xc = cb_ref[...] + cw_ref[CONV_WIDTH - 1:CONV_WIDTH, :] * xa
    for j in range(1, CONV_WIDTH):
        k = CONV_WIDTH - 1 - j
        xc = xc + cw_ref[k:k + 1, :] * ext_ref[SUBLANES - j:SUBLANES - j + ts, :]
    ext_ref[0:SUBLANES, :] = ext_ref[ts:ts + SUBLANES, :]
    return xc


def _rg_lru_rows(xc, gates, ga, lam_ref, o_ref, row0, h_ref):
    ts, width = xc.shape
    r = _sigmoid(gates[:, :width])
    i = _sigmoid(gates[:, width:])
    lam = lam_ref[...]
    log_sig_lam = jnp.minimum(lam, 0.0) - jnp.log(1.0 + jnp.exp(-jnp.abs(lam)))
    log_a = LRU_C * r * log_sig_lam
    a = jnp.exp(log_a)
    u = jnp.sqrt(1.0 - a * a) * (i * xc)

    groups = ts // SUBLANES
    a = a.reshape(groups, SUBLANES, width)
    u = u.reshape(groups, SUBLANES, width)
    row = lax.broadcasted_iota(jnp.int32, (groups, SUBLANES, width), 1)
    step = 1
    while step < SUBLANES:
        valid = row >= step
        a_prev = pltpu.roll(a, step, 1)
        u_prev = pltpu.roll(u, step, 1)
        u = jnp.where(valid, a * u_prev + u, u)
        a = jnp.where(valid, a * a_prev, a)
        step *= 2
    gate = _silu(ga)
    state = h_ref[...]
    for pair in range(ts // (2 * SUBLANES)):
        hs = []
        for grp in (2 * pair, 2 * pair + 1):
            h = u[grp] + a[grp] * state
            state = h[SUBLANES - 1:SUBLANES, :]
            hs.append(h)
        rows = slice(2 * pair * SUBLANES, 2 * (pair + 1) * SUBLANES)
        o_ref[row0 + rows.start:row0 + rows.stop, :] = (
            jnp.concatenate(hs, axis=0) * gate[rows]).astype(BF16)
    h_ref[...] = state


def _store_transposed_queries(q_ref, qt_scr):
    for blk in range(qt_scr.shape[0]):
        qt_scr[blk] = q_ref[:, blk * LANES:(blk + 1) * LANES].astype(F32).T.astype(BF16)


def _score_stage(k_ref, kb, qt_scr, s_scr, with_max=False):
    tile_max = []
    for blk in range(qt_scr.shape[0]):
        both = jnp.dot(k_ref[0, kb, blk], qt_scr[blk], preferred_element_type=F32)
        for half in range(2):
            s = both[half * KV_TILE:(half + 1) * KV_TILE]
            s_scr[2 * blk + half] = s
            if with_max:
                tile_max.append(jnp.max(s, axis=0, keepdims=True))
    return tuple(tile_max)


def _for_each_query_block(block_fn, q_ref, g_ref, o_ref, tq, *rest):
    def body(sub, carry):
        rows = pl.ds(pl.multiple_of(sub * tq, tq), tq)
        block_fn(pl.program_id(2) * Q_BLOCKS + sub, q_ref.at[rows, :], g_ref.at[rows, :],
                 o_ref.at[rows, :], tq, *rest)
        return carry

    lax.fori_loop(0, Q_BLOCKS, body, 0)


def _attention_scratch(streams, tq, value_rows):
    return [pltpu.VMEM((streams // 2, LANES, tq), BF16),
            pltpu.VMEM((streams, KV_TILE, tq), F32),
            pltpu.VMEM((streams, KV_TILE, tq), BF16),
            pltpu.VMEM((streams, value_rows, tq), F32)]


def _sb_kernel(q_ref, k_ref, vt_ref, g_ref, o_ref, *scratch, tq):
    _for_each_query_block(_sb_block, q_ref, g_ref, o_ref, tq, k_ref, vt_ref, *scratch)


def _sb_block(qi, q_ref, g_ref, o_ref, tq, k_ref, vt_ref, qt_scr, s_scr, w_scr, acc_scr):
    streams = s_scr.shape[0]
    _store_transposed_queries(q_ref, qt_scr)
    acc_scr[...] = jnp.zeros(acc_scr.shape, F32)

    r = lax.broadcasted_iota(jnp.int32, (KV_TILE, tq), 0)
    c = lax.broadcasted_iota(jnp.int32, (KV_TILE, tq), 1)
    causal = ((r % SUBLANES) * SEGMENT + r // SUBLANES) < c
    srow = lax.broadcasted_iota(jnp.int32, (SUBLANES, tq), 0)

    def weight_stage(later, masked):
        new_later = []
        for g in range(streams):
            run = jnp.ones((SUBLANES, tq), F32)
            for v in reversed(range(SEGMENT)):
                rows = slice(v * SUBLANES, (v + 1) * SUBLANES)
                half_tanh = 0.5 * jnp.tanh(s_scr[g, rows, :])
                beta = 0.5 + half_tanh
                rest = 0.5 - half_tanh
                if masked:
                    beta = jnp.where(causal[rows], beta, 0.0)
                    rest = jnp.where(causal[rows], rest, 1.0)
                s_scr[g, rows, :] = beta * run
                run = run * rest
            off = jnp.broadcast_to(later[g], (SUBLANES, tq))
            for s2 in range(1, SUBLANES):
                off = off * jnp.where(srow < s2, run[s2:s2 + 1, :], 1.0)
            off2 = jnp.concatenate([off, off], axis=0)
            for v in range(SEGMENT // 2):
                rows = slice(2 * v * SUBLANES, 2 * (v + 1) * SUBLANES)
                w_scr[g, rows, :] = (s_scr[g, rows, :] * off2).astype(BF16)
            new_later.append((off * run)[0:1, :])
        return tuple(new_later)

    def value_stage(t):
        vt_tile = vt_ref[0, qi - t]
        for g in range(streams):
            acc_scr[g] += jnp.dot(vt_tile[g * HEAD_DIM:(g + 1) * HEAD_DIM, :], w_scr[g],
                                  preferred_element_type=F32)

    _score_stage(k_ref, qi, qt_scr, s_scr)
    later = weight_stage(tuple(jnp.ones((1, tq), F32) for _ in range(streams)), True)
    _score_stage(k_ref, jnp.maximum(qi - 1, 0), qt_scr, s_scr)

    def step(j, later):
        value_stage(j - 1)
        later = weight_stage(later, False)
        _score_stage(k_ref, qi - (j + 1), qt_scr, s_scr)
        return later

    later = lax.fori_loop(1, qi, step, later)

    @pl.when(qi > 0)
    def _():
        value_stage(qi - 1)
        weight_stage(later, False)

    value_stage(qi)
    out_t = acc_scr[...].reshape(streams * HEAD_DIM, tq)
    o_ref[...] = (out_t.T * _silu(g_ref[...].astype(F32))).astype(BF16)


def _sb_attention(y, k_tiles, vt_tiles, *, batch, seq, q0, g0, width):
    nq = seq // (Q_BLOCKS * Q_TILE)
    nkb = seq // KV_TILE
    cols = SB_STREAMS * HEAD_DIM
    rows = Q_BLOCKS * Q_TILE
    return pl.pallas_call(
        functools.partial(_sb_kernel, tq=Q_TILE),
        grid=(batch, width // cols, nq),
        in_specs=[pl.BlockSpec((rows, cols), lambda b, p, i: (b * nq + i, q0 // cols + p)),
                  pl.BlockSpec((1, nkb, cols // LANES, 2 * KV_TILE, LANES),
                               lambda b, p, i: (b, 0, p, 0, 0)),
                  pl.BlockSpec((1, nkb, cols, KV_TILE), lambda b, p, i: (b, 0, p, 0)),
                  pl.BlockSpec((rows, cols), lambda b, p, i: (b * nq + i, g0 // cols + p))],
        out_specs=pl.BlockSpec((rows, cols), lambda b, p, i: (b * nq + i, p)),
        out_shape=jax.ShapeDtypeStruct((batch * seq, width), BF16),
        scratch_shapes=_attention_scratch(SB_STREAMS, Q_TILE, HEAD_DIM),
        compiler_params=_params(3),
        name="stick_breaking_attention",
    )(y, k_tiles, vt_tiles, y)


def _diff_kernel(q_ref, k_ref, vt_ref, g_ref, lq1_ref, lk1_ref, lq2_ref, lk2_ref, sg_ref,
                 o_ref, *scratch, tq, lambda_init):
    _for_each_query_block(
        functools.partial(_diff_block, lambda_init=lambda_init), q_ref, g_ref, o_ref, tq,
        k_ref, vt_ref, lq1_ref, lk1_ref, lq2_ref, lk2_ref, sg_ref, *scratch)


def _diff_block(qi, q_ref, g_ref, o_ref, tq, k_ref, vt_ref, lq1_ref, lk1_ref, lq2_ref, lk2_ref,
                sg_ref, qt_scr, s_scr, p_scr, acc_scr, *, lambda_init):
    streams = s_scr.shape[0]
    _store_transposed_queries(q_ref, qt_scr)
    acc_scr[...] = jnp.zeros(acc_scr.shape, F32)
    r = lax.broadcasted_iota(jnp.int32, (KV_TILE, tq), 0)
    c = lax.broadcasted_iota(jnp.int32, (KV_TILE, tq), 1)
    causal = r <= c
    vdim = 2 * HEAD_DIM
    vrows = vdim + SUM_ROWS

    def softmax_stage(stats, tile_max, masked):
        new = []
        for g in range(streams):
            m, _ = stats[g]
            s = s_scr[g]
            if masked:
                s = jnp.where(causal, s, -jnp.inf)
                m_new = jnp.maximum(m, jnp.max(s, axis=0, keepdims=True))
            else:
                m_new = jnp.maximum(m, tile_max[g])
            p = jnp.exp2(s - m_new)
            alpha = jnp.exp2(m - m_new)
            p_scr[g] = p.astype(BF16)
            new.append((m_new, alpha))
        return tuple(new)

    def value_stage(t, stats):
        vt_tile = vt_ref[0, qi - t]
        for g in range(streams):
            hd = g // 2
            acc_scr[g] = stats[g][1] * acc_scr[g] + jnp.dot(
                vt_tile[hd * vrows:(hd + 1) * vrows, :], p_scr[g], preferred_element_type=F32)

    init = tuple((jnp.full((1, tq), -jnp.inf, F32), jnp.zeros((1, tq), F32))
                 for _ in range(streams))
    _score_stage(k_ref, qi, qt_scr, s_scr)
    stats = softmax_stage(init, None, True)
    tile_max = _score_stage(k_ref, jnp.maximum(qi - 1, 0), qt_scr, s_scr, True)

    def step(j, carry):
        stats, tile_max = carry
        value_stage(j - 1, stats)
        stats = softmax_stage(stats, tile_max, False)
        return stats, _score_stage(k_ref, qi - (j + 1), qt_scr, s_scr, True)

    stats, tile_max = lax.fori_loop(1, qi, step, (stats, tile_max))

    def last_step(stats):
        value_stage(qi - 1, stats)
        return softmax_stage(stats, tile_max, False)

    stats = lax.cond(qi > 0, last_step, lambda st: st, stats)
    value_stage(qi, stats)

    lam = (jnp.exp(jnp.sum(lq1_ref[...] * lk1_ref[...], axis=-1, keepdims=True))
           - jnp.exp(jnp.sum(lq2_ref[...] * lk2_ref[...], axis=-1, keepdims=True))
           + lambda_init)
    head_gain = sg_ref[...] * (1.0 - lambda_init)
    for hd in range(streams // 2):
        acc1, acc2 = acc_scr[2 * hd], acc_scr[2 * hd + 1]
        inv1 = 1.0 / acc1[vdim:vdim + 1]
        inv2 = lam / acc2[vdim:vdim + 1]
        out_t = acc1[:vdim] * inv1 - acc2[:vdim] * inv2
        ms = jnp.mean(out_t * out_t, axis=0, keepdims=True)
        out = (out_t * lax.rsqrt(ms + EPS)).T
        cols = slice(hd * vdim, (hd + 1) * vdim)
        o_ref[:, cols] = (out * head_gain * _silu(g_ref[:, cols].astype(F32))).astype(BF16)


def _diff_attention(y, k_tiles, vt_tiles, lq1, lk1, lq2, lk2, subln_g, *, batch, seq, width,
                    lambda_init):
    nq = seq // (Q_BLOCKS * Q_TILE)
    nkb = seq // KV_TILE
    cols = DIFF_STREAMS * HEAD_DIM
    rows = Q_BLOCKS * Q_TILE
    g_col = width // cols
    vt_rows = DIFF_STREAMS // 2 * (2 * HEAD_DIM + SUM_ROWS)
    vec = lambda a: a.reshape(1, -1)
    small = lambda n: pl.BlockSpec((1, n), lambda b, h, i: (0, 0))
    return pl.pallas_call(
        functools.partial(_diff_kernel, tq=Q_TILE, lambda_init=lambda_init),
        grid=(batch, width // cols, nq),
        in_specs=[pl.BlockSpec((rows, cols), lambda b, h, i: (b * nq + i, h)),
                  pl.BlockSpec((1, nkb, cols // LANES, 2 * KV_TILE, LANES),
                               lambda b, h, i: (b, 0, h, 0, 0)),
                  pl.BlockSpec((1, nkb, vt_rows, KV_TILE), lambda b, h, i: (b, 0, h, 0)),
                  pl.BlockSpec((rows, cols), lambda b, h, i: (b * nq + i, g_col + h)),
                  small(HEAD_DIM), small(HEAD_DIM), small(HEAD_DIM), small(HEAD_DIM),
                  small(2 * HEAD_DIM)],
        out_specs=pl.BlockSpec((rows, cols), lambda b, h, i: (b * nq + i, h)),
        out_shape=jax.ShapeDtypeStruct((batch * seq, width), BF16),
        scratch_shapes=_attention_scratch(DIFF_STREAMS, Q_TILE, 2 * HEAD_DIM + SUM_ROWS),
        compiler_params=_params(3),
        name="differential_attention",
    )(y, k_tiles, vt_tiles, y, vec(lq1), vec(lk1), vec(lq2), vec(lk2), vec(subln_g))


def _post_kernel(*refs, final):
    if final:
        (h_ref, ma_ref, mb_ref, p_ref, woa_ref, wob_ref, gp_ref, wg_ref, wp_ref,
         fn_ref, o_ref) = refs
    else:
        h_ref, ma_ref, mb_ref, p_ref, woa_ref, wob_ref, gp_ref, wg_ref, wp_ref, o_ref = refs
    h = (h_ref[...]
         + jnp.dot(ma_ref[...], woa_ref[...], preferred_element_type=F32)
         + jnp.dot(mb_ref[...], wob_ref[...], preferred_element_type=F32))
    ms = jnp.mean(h * h, axis=-1, keepdims=True)
    hn = (h * lax.rsqrt(ms + EPS) * gp_ref[...]).astype(BF16)
    gate = _sigmoid(jnp.dot(hn, wg_ref[...], preferred_element_type=F32))
    ple = jnp.dot(p_ref[...].astype(BF16), wp_ref[...], preferred_element_type=F32)
    h = h + gate * ple
    if final:
        ms = jnp.mean(h * h, axis=-1, keepdims=True)
        h = h * lax.rsqrt(ms + EPS) * fn_ref[...]
    o_ref[...] = h


def _post(h, mix_a, mix_b, b_col, p, w_out, g_ple, w_gate, w_proj, final_norm=None, *, tm=1024):
    t, d = h.shape
    half = w_out.shape[0] // 2
    pd = p.shape[1]
    final = final_norm is not None
    const = lambda shape: pl.BlockSpec(shape, lambda i: (0, 0))
    in_specs = [pl.BlockSpec((tm, d), lambda i: (i, 0)),
                pl.BlockSpec((tm, half), lambda i: (i, 0)),
                pl.BlockSpec((tm, half), lambda i: (i, b_col)),
                pl.BlockSpec((tm, pd), lambda i: (i, 0)),
                pl.BlockSpec((half, d), lambda i: (0, 0)),
                pl.BlockSpec((half, d), lambda i: (1, 0)),
                const((1, d)), const((d, d)), const((pd, d))]
    args = [h, mix_a, mix_b, p, w_out, w_out, g_ple.reshape(1, d), w_gate, w_proj]
    if final:
        in_specs.append(const((1, d)))
        args.append(final_norm.reshape(1, d))
    return pl.pallas_call(
        functools.partial(_post_kernel, final=final),
        grid=(t // tm,),
        in_specs=in_specs,
        out_specs=pl.BlockSpec((tm, d), lambda i: (i, 0)),
        out_shape=jax.ShapeDtypeStruct((t, d), F32),
        compiler_params=_params(1),
        name="out_proj_ple_final" if final else "out_proj_ple",
    )(*args)


def _block_diag(w):
    nb, bs, _ = w.shape
    eye = jnp.eye(nb, dtype=w.dtype)
    return jnp.einsum('ncd,nm->ncmd', w, eye).reshape(nb * bs, nb * bs)


def kernel(x, p, positions, norm_mix, norm_ple, w_ple_gate, w_ple_proj, w_in_e, conv_w, conv_b,
           lru_wa, lru_ba, lru_wx, lru_bx, lru_lambda, w_out_e, w_in_o, lam_q1, lam_k1, lam_q2,
           lam_k2, subln_g, w_out_o, final_norm):
    batch, seq, d = x.shape
    depth = norm_mix.shape[0]
    t = batch * seq
    lru_w = conv_w.shape[-1]
    sb_w = (w_in_e.shape[-1] - 2 * lru_w) // 4
    diff_w = w_in_o.shape[-1] // 4

    half = HEAD_DIM // 2
    inv_freq = ROPE_THETA ** (-jnp.arange(0, HEAD_DIM, 2, dtype=F32) / HEAD_DIM)
    ang = positions.astype(F32).reshape(t, 1) * inv_freq
    cos = jnp.tile(jnp.cos(ang), (1, LANES // half))
    sin = jnp.tile(jnp.concatenate([-jnp.sin(ang), jnp.sin(ang)], axis=1), (1, LANES // HEAD_DIM))

    h = x.reshape(t, d)
    for i in range(depth):
        j = i // 2
        last = i == depth - 1
        if i % 2 == 0:
            w = w_in_e[j].astype(BF16)
            q0 = 2 * lru_w
            k0, v0, g0 = q0 + sb_w, q0 + 2 * sb_w, q0 + 3 * sb_w
            w_main = jnp.concatenate([w[:, q0:k0], w[:, g0:]], axis=1)
            w_gates = jnp.concatenate([_block_diag(lru_wa[j]), _block_diag(lru_wx[j])], axis=1)
            b_gates = jnp.concatenate([lru_ba[j], lru_bx[j]])
            lru = (w[:, :q0], conv_w[j], conv_b[j], w_gates.astype(BF16), b_gates, lru_lambda[j])
            y, k_tiles, vt_tiles, oa = _in_proj(
                h, norm_mix[i], w_main, w[:, k0:v0], w[:, v0:g0], lru=lru, batch=batch, seq=seq,
                q_range=(0, sb_w), scale=0.5 * HEAD_DIM ** -0.5, permute_keys=True)
            ob = _sb_attention(y, k_tiles, vt_tiles, batch=batch, seq=seq,
                               q0=0, g0=sb_w, width=sb_w)
            mix_a, mix_b, b_col, w_out = oa, ob, 0, w_out_e[j]
        else:
            lambda_init = 0.8 - 0.6 * math.exp(-0.3 * i)
            w = w_in_o[j].astype(BF16)
            w_main = jnp.concatenate([w[:, :diff_w], w[:, 3 * diff_w:]], axis=1)
            y, k_tiles, vt_tiles = _in_proj(
                h, norm_mix[i], w_main, w[:, diff_w:2 * diff_w], w[:, 2 * diff_w:3 * diff_w],
                cos, sin, batch=batch, seq=seq, q_range=(0, diff_w),
                scale=HEAD_DIM ** -0.5 * math.log2(math.e), permute_keys=False,
                value_head=2 * HEAD_DIM)
            o = _diff_attention(y, k_tiles, vt_tiles, lam_q1[j], lam_k1[j], lam_q2[j], lam_k2[j],
                                subln_g[j], batch=batch, seq=seq, width=diff_w,
                                lambda_init=lambda_init)
            mix_a, mix_b, b_col, w_out = o, o, 1, w_out_o[j]
        h = _post(h, mix_a, mix_b, b_col, p[i].reshape(t, -1), w_out.astype(BF16), norm_ple[i],
                  w_ple_gate[i].astype(BF16), w_ple_proj[i].astype(BF16),
                  final_norm if last else None)
    return h.reshape(batch, seq, d)
```

```python
import functools
import math

import jax
import jax.numpy as jnp
from jax import lax
from jax.experimental import pallas as pl
from jax.experimental.pallas import tpu as pltpu

F32 = jnp.float32
BF16 = jnp.bfloat16

EPS = 1e-6
HEAD_DIM = 64
LRU_C = 8.0
CONV_WIDTH = 4
ROPE_THETA = 10000.0

LANES = 128
SUBLANES = 8
KV_TILE = 256
Q_TILE = 256
SEGMENT = KV_TILE // SUBLANES
SB_STREAMS = 8
DIFF_STREAMS = 8
SUM_ROWS = 16
VMEM_LIMIT = 48 * 1024 * 1024


def _params(n_axes):
    return pltpu.CompilerParams(
        dimension_semantics=("arbitrary",) * n_axes, vmem_limit_bytes=VMEM_LIMIT)


def _sigmoid(x):
    return 0.5 + 0.5 * jnp.tanh(0.5 * x)


def _silu(x):
    return x * _sigmoid(x)


def _rotary(y, cos, sin, first_half):
    width = y.shape[1]
    partner = jnp.where(first_half, pltpu.roll(y, width - HEAD_DIM // 2, 1),
                        pltpu.roll(y, HEAD_DIM // 2, 1))
    return y * cos + partner * sin


def _in_proj_kernel(*refs, chunk, rope, q_range, scale, permute_keys, lru_steps, value_head):
    x_ref, g_ref, wm_ref, wk_ref, wv_ref = refs[:5]
    if rope:
        cos_ref, sin_ref = refs[5:7]
    if lru_steps:
        wl_ref, cw_ref, cb_ref, wg_ref, bg_ref, lam_ref = refs[5:11]
        m_ref, k_ref, vt_ref, oa_ref, ext_ref, h_ref = refs[11:]
    else:
        m_ref, k_ref, vt_ref = refs[-3:]
    tm = x_ref.shape[0]
    x = x_ref[...]
    ms = jnp.mean(x * x, axis=-1, keepdims=True)
    xn = x * lax.rsqrt(ms + EPS) * g_ref[...]
    xn_bf = xn.astype(BF16)
    if rope:
        reps = chunk // LANES
        cos = jnp.concatenate([cos_ref[...]] * reps, axis=1)
        sin = jnp.concatenate([sin_ref[...]] * reps, axis=1)
        lane = lax.broadcasted_iota(jnp.int32, (tm, chunk), 1)
        first_half = (lane % HEAD_DIM) < (HEAD_DIM // 2)

    lru_tasks = []
    if lru_steps:
        width = oa_ref.shape[1]

        @pl.when(pl.program_id(0) % lru_steps == 0)
        def _():
            ext_ref[0:SUBLANES, :] = jnp.zeros((SUBLANES, width), F32)
            h_ref[...] = jnp.zeros((1, width), F32)

        xa = jnp.dot(xn_bf, wl_ref[:, :width], preferred_element_type=F32)
        ga = jnp.dot(xn_bf, wl_ref[:, width:], preferred_element_type=F32)
        xc = _causal_conv(xa, cw_ref, cb_ref, ext_ref)

    def main_chunk(c):
        lo = c * chunk
        y = jnp.dot(xn_bf, wm_ref[:, lo:lo + chunk], preferred_element_type=F32)
        if q_range[0] <= lo < q_range[1]:
            if rope:
                y = _rotary(y, cos, sin, first_half)
            y = y * scale
        m_ref[:, lo:lo + chunk] = y.astype(BF16)

    n_tiles = tm // KV_TILE
    if permute_keys:
        r = lax.broadcasted_iota(jnp.int32, (KV_TILE, KV_TILE), 0)
        c = lax.broadcasted_iota(jnp.int32, (KV_TILE, KV_TILE), 1)
        perm = jnp.where(c == (r % SUBLANES) * SEGMENT + r // SUBLANES, 1.0, 0.0).astype(BF16)
        xk = jnp.concatenate(
            [jnp.dot(perm, xn_bf[t * KV_TILE:(t + 1) * KV_TILE, :],
                     preferred_element_type=F32).astype(BF16) for t in range(n_tiles)], axis=0)
    else:
        xk = xn_bf

    low = lax.broadcasted_iota(jnp.int32, (KV_TILE, LANES), 1) < HEAD_DIM

    def key_chunk(c):
        kk = jnp.dot(xk, wk_ref[:, c * chunk:(c + 1) * chunk], preferred_element_type=F32)
        if rope:
            kk = _rotary(kk, cos, sin, first_half)
        for t in range(n_tiles):
            for b in range(chunk // LANES):
                blk = kk[t * KV_TILE:(t + 1) * KV_TILE, b * LANES:(b + 1) * LANES]
                nb = c * (chunk // LANES) + b
                k_ref[0, t, nb, 0:KV_TILE, :] = jnp.where(low, blk, 0.0).astype(BF16)
                k_ref[0, t, nb, KV_TILE:2 * KV_TILE, :] = jnp.where(low, 0.0, blk).astype(BF16)
    def value_chunk(c):
        vv = jnp.dot(xk, wv_ref[:, c * chunk:(c + 1) * chunk], preferred_element_type=F32)
        for t in range(n_tiles):
            vt = vv[t * KV_TILE:(t + 1) * KV_TILE, :].T.astype(BF16)
            if not value_head:
                vt_ref[0, t, c * chunk:(c + 1) * chunk, :] = vt
                continue
            for hh in range(chunk // value_head):
                r0 = (c * (chunk // value_head) + hh) * (value_head + SUM_ROWS)
                vt_ref[0, t, r0:r0 + value_head, :] = vt[hh * value_head:(hh + 1) * value_head]
                vt_ref[0, t, r0 + value_head:r0 + value_head + SUM_ROWS, :] = jnp.ones(
                    (SUM_ROWS, KV_TILE), BF16)

    for c in range(wm_ref.shape[1] // chunk):
        main_chunk(c)
    mxu_tasks = ([functools.partial(key_chunk, c) for c in range(wk_ref.shape[1] // chunk)]
                 + [functools.partial(value_chunk, c) for c in range(wv_ref.shape[1] // chunk)])
    if lru_steps:
        gates = jnp.dot(xc.astype(BF16), wg_ref[...], preferred_element_type=F32) + bg_ref[...]
        ts = tm // 4

        def lru_rows(r0):
            rows = slice(r0, r0 + ts)
            _rg_lru_rows(xc[rows], gates[rows], ga[rows], lam_ref, oa_ref, r0, h_ref)

        lru_tasks = [functools.partial(lru_rows, r0) for r0 in range(0, tm, ts)]
    for n in range(max(len(mxu_tasks), len(lru_tasks))):
        for tasks in (mxu_tasks, lru_tasks):
            if n < len(tasks):
                tasks[n]()


def _in_proj(h, g, w_main, w_k, w_v, cos=None, sin=None, lru=None, *, batch, seq, q_range,
             scale, permute_keys, value_head=0, tm=512, chunk=512):
    t, d = h.shape
    n_main, n_k, n_v = w_main.shape[1], w_k.shape[1], w_v.shape[1]
    vt_rows = n_v // value_head * (value_head + SUM_ROWS) if value_head else n_v
    rope = cos is not None
    assert not (rope and lru is not None)
    steps = seq // tm
    tiles = tm // KV_TILE
    nkb = seq // KV_TILE
    const = lambda shape: pl.BlockSpec(shape, lambda i: (0, 0))
    in_specs = [pl.BlockSpec((tm, d), lambda i: (i, 0)), const((1, d)),
                const((d, n_main)), const((d, n_k)), const((d, n_v))]
    args = [h, g.reshape(1, d), w_main, w_k, w_v]
    if rope:
        in_specs += [pl.BlockSpec((tm, LANES), lambda i: (i, 0))] * 2
        args += [cos, sin]
    out_specs = [pl.BlockSpec((tm, n_main), lambda i: (i, 0)),
                 pl.BlockSpec((1, tiles, n_k // LANES, 2 * KV_TILE, LANES),
                              lambda i: (i // steps, i % steps, 0, 0, 0)),
                 pl.BlockSpec((1, tiles, vt_rows, KV_TILE),
                              lambda i: (i // steps, i % steps, 0, 0))]
    out_shape = [jax.ShapeDtypeStruct((t, n_main), BF16),
                 jax.ShapeDtypeStruct((batch, nkb, n_k // LANES, 2 * KV_TILE, LANES), BF16),
                 jax.ShapeDtypeStruct((batch, nkb, vt_rows, KV_TILE), BF16)]
    scratch = []
    if lru is not None:
        w_lru, conv_w, conv_b, w_gates, b_gates, lam = lru
        width = conv_w.shape[1]
        in_specs += [const((d, 2 * width)), const((CONV_WIDTH, width)), const((1, width)),
                     const((width, 2 * width)), const((1, 2 * width)), const((1, width))]
        args += [w_lru, conv_w, conv_b.reshape(1, width), w_gates,
                 b_gates.reshape(1, 2 * width), lam.reshape(1, width)]
        out_specs.append(pl.BlockSpec((tm, width), lambda i: (i, 0)))
        out_shape.append(jax.ShapeDtypeStruct((t, width), BF16))
        scratch = [pltpu.VMEM((tm + SUBLANES, width), F32), pltpu.VMEM((1, width), F32)]
    return pl.pallas_call(
        functools.partial(_in_proj_kernel, chunk=chunk, rope=rope, q_range=q_range,
                          scale=scale, permute_keys=permute_keys, value_head=value_head,
                          lru_steps=steps if lru is not None else 0),
        grid=(t // tm,),
        in_specs=in_specs,
        out_specs=out_specs,
        out_shape=out_shape,
        scratch_shapes=scratch,
        compiler_params=_params(1),
        name="in_proj_rope" if rope else "in_proj_lru",
    )(*args)


def _causal_conv(xa, cw_ref, cb_ref, ext_ref):
    ts = xa.shape[0]
    ext_ref[SUBLANES:SUBLANES + ts, :] = xa
    xc = cb_ref[...] + cw_ref[CONV_WIDTH - 1:CONV_WIDTH, :] * xa
    for j in range(1, CONV_WIDTH):
        k = CONV_WIDTH - 1 - j
        xc = xc + cw_ref[k:k + 1, :] * ext_ref[SUBLANES - j:SUBLANES - j + ts, :]
    ext_ref[0:SUBLANES, :] = ext_ref[ts:ts + SUBLANES, :]
    return xc


def _rg_lru_rows(xc, gates, ga, lam_ref, o_ref, row0, h_ref):
    ts, width = xc.shape
    r = _sigmoid(gates[:, :width])
    i = _sigmoid(gates[:, width:])
    lam = lam_ref[...]
    log_sig_lam = jnp.minimum(lam, 0.0) - jnp.log(1.0 + jnp.exp(-jnp.abs(lam)))
    log_a = LRU_C * r * log_sig_lam
    a = jnp.exp(log_a)
    u = jnp.sqrt(1.0 - a * a) * (i * xc)

    groups = ts // SUBLANES
    a = a.reshape(groups, SUBLANES, width)
    u = u.reshape(groups, SUBLANES, width)
    row = lax.broadcasted_iota(jnp.int32, (groups, SUBLANES, width), 1)
    step = 1
    while step < SUBLANES:
        valid = row >= step
        a_prev = pltpu.roll(a, step, 1)
        u_prev = pltpu.roll(u, step, 1)
        u = jnp.where(valid, a * u_prev + u, u)
        a = jnp.where(valid, a * a_prev, a)
        step *= 2
    gate = _silu(ga)
    state = h_ref[...]
    for pair in range(ts // (2 * SUBLANES)):
        hs = []
        for grp in (2 * pair, 2 * pair + 1):
            h = u[grp] + a[grp] * state
            state = h[SUBLANES - 1:SUBLANES, :]
            hs.append(h)
        rows = slice(2 * pair * SUBLANES, 2 * (pair + 1) * SUBLANES)
        o_ref[row0 + rows.start:row0 + rows.stop, :] = (
            jnp.concatenate(hs, axis=0) * gate[rows]).astype(BF16)
    h_ref[...] = state


def _store_transposed_queries(q_ref, qt_scr):
    for blk in range(qt_scr.shape[0]):
        qt_scr[blk] = q_ref[:, blk * LANES:(blk + 1) * LANES].astype(F32).T.astype(BF16)


def _score_stage(k_ref, kb, qt_scr, s_scr, with_max=False):
    tile_max = []
    for blk in range(qt_scr.shape[0]):
        both = jnp.dot(k_ref[0, kb, blk], qt_scr[blk], preferred_element_type=F32)
        for half in range(2):
            s = both[half * KV_TILE:(half + 1) * KV_TILE]
            s_scr[2 * blk + half] = s
            if with_max:
                tile_max.append(jnp.max(s, axis=0, keepdims=True))
    return tuple(tile_max)


def _attention_scratch(streams, tq, value_rows):
    return [pltpu.VMEM((streams // 2, LANES, tq), BF16),
            pltpu.VMEM((streams, KV_TILE, tq), F32),
            pltpu.VMEM((streams, KV_TILE, tq), BF16),
            pltpu.VMEM((streams, value_rows, tq), F32)]


def _sb_kernel(q_ref, k_ref, vt_ref, g_ref, o_ref, qt_scr, s_scr, w_scr, acc_scr, *, tq):
    qi = pl.program_id(2)
    streams = s_scr.shape[0]
    _store_transposed_queries(q_ref, qt_scr)
    acc_scr[...] = jnp.zeros(acc_scr.shape, F32)

    r = lax.broadcasted_iota(jnp.int32, (KV_TILE, tq), 0)
    c = lax.broadcasted_iota(jnp.int32, (KV_TILE, tq), 1)
    causal = ((r % SUBLANES) * SEGMENT + r // SUBLANES) < c
    srow = lax.broadcasted_iota(jnp.int32, (SUBLANES, tq), 0)

    def weight_stage(later, masked):
        new_later = []
        for g in range(streams):
            run = jnp.ones((SUBLANES, tq), F32)
            for v in reversed(range(SEGMENT)):
                rows = slice(v * SUBLANES, (v + 1) * SUBLANES)
                beta = 0.5 + 0.5 * jnp.tanh(s_scr[g, rows, :])
                if masked:
                    beta = jnp.where(causal[rows], beta, 0.0)
                part = beta * run
                s_scr[g, rows, :] = part
                run = run - part
            off = jnp.broadcast_to(later[g], (SUBLANES, tq))
            for s2 in range(1, SUBLANES):
                off = off * jnp.where(srow < s2, run[s2:s2 + 1, :], 1.0)
            off2 = jnp.concatenate([off, off], axis=0)
            for v in range(SEGMENT // 2):
                rows = slice(2 * v * SUBLANES, 2 * (v + 1) * SUBLANES)
                w_scr[g, rows, :] = (s_scr[g, rows, :] * off2).astype(BF16)
            new_later.append((off * run)[0:1, :])
        return tuple(new_later)

    def value_stage(t):
        vt_tile = vt_ref[0, qi - t]
        for g in range(streams):
            acc_scr[g] += jnp.dot(vt_tile[g * HEAD_DIM:(g + 1) * HEAD_DIM, :], w_scr[g],
                                  preferred_element_type=F32)

    _score_stage(k_ref, qi, qt_scr, s_scr)
    later = weight_stage(tuple(jnp.ones((1, tq), F32) for _ in range(streams)), True)
    _score_stage(k_ref, jnp.maximum(qi - 1, 0), qt_scr, s_scr)

    def step(j, later):
        value_stage(j - 1)
        later = weight_stage(later, False)
        _score_stage(k_ref, qi - (j + 1), qt_scr, s_scr)
        return later

    later = lax.fori_loop(1, qi, step, later)

    @pl.when(qi > 0)
    def _():
        value_stage(qi - 1)
        weight_stage(later, False)

    value_stage(qi)
    out_t = acc_scr[...].reshape(streams * HEAD_DIM, tq)
    o_ref[...] = (out_t.T * _silu(g_ref[...].astype(F32))).astype(BF16)


def _sb_attention(y, k_tiles, vt_tiles, *, batch, seq, q0, g0, width):
    nq = seq // Q_TILE
    nkb = seq // KV_TILE
    cols = SB_STREAMS * HEAD_DIM
    return pl.pallas_call(
        functools.partial(_sb_kernel, tq=Q_TILE),
        grid=(batch, width // cols, nq),
        in_specs=[pl.BlockSpec((Q_TILE, cols), lambda b, p, i: (b * nq + i, q0 // cols + p)),
                  pl.BlockSpec((1, nkb, cols // LANES, 2 * KV_TILE, LANES),
                               lambda b, p, i: (b, 0, p, 0, 0)),
                  pl.BlockSpec((1, nkb, cols, KV_TILE), lambda b, p, i: (b, 0, p, 0)),
                  pl.BlockSpec((Q_TILE, cols), lambda b, p, i: (b * nq + i, g0 // cols + p))],
        out_specs=pl.BlockSpec((Q_TILE, cols), lambda b, p, i: (b * nq + i, p)),
        out_shape=jax.ShapeDtypeStruct((batch * seq, width), BF16),
        scratch_shapes=_attention_scratch(SB_STREAMS, Q_TILE, HEAD_DIM),
        compiler_params=_params(3),
        name="stick_breaking_attention",
    )(y, k_tiles, vt_tiles, y)


def _diff_kernel(q_ref, k_ref, vt_ref, g_ref, lq1_ref, lk1_ref, lq2_ref, lk2_ref, sg_ref,
                 o_ref, qt_scr, s_scr, p_scr, acc_scr, *, tq, lambda_init):
    qi = pl.program_id(2)
    streams = s_scr.shape[0]
    _store_transposed_queries(q_ref, qt_scr)
    acc_scr[...] = jnp.zeros(acc_scr.shape, F32)
    r = lax.broadcasted_iota(jnp.int32, (KV_TILE, tq), 0)
    c = lax.broadcasted_iota(jnp.int32, (KV_TILE, tq), 1)
    causal = r <= c
    vdim = 2 * HEAD_DIM
    vrows = vdim + SUM_ROWS

    def softmax_stage(stats, tile_max, masked):
        new = []
        for g in range(streams):
            m, _ = stats[g]
            s = s_scr[g]
            if masked:
                s = jnp.where(causal, s, -jnp.inf)
                m_new = jnp.maximum(m, jnp.max(s, axis=0, keepdims=True))
            else:
                m_new = jnp.maximum(m, tile_max[g])
            p = jnp.exp2(s - m_new)
            alpha = jnp.exp2(m - m_new)
            p_scr[g] = p.astype(BF16)
            new.append((m_new, alpha))
        return tuple(new)

    def value_stage(t, stats):
        vt_tile = vt_ref[0, qi - t]
        for g in range(streams):
            hd = g // 2
            acc_scr[g] = stats[g][1] * acc_scr[g] + jnp.dot(
                vt_tile[hd * vrows:(hd + 1) * vrows, :], p_scr[g], preferred_element_type=F32)

    init = tuple((jnp.full((1, tq), -jnp.inf, F32), jnp.zeros((1, tq), F32))
                 for _ in range(streams))
    _score_stage(k_ref, qi, qt_scr, s_scr)
    stats = softmax_stage(init, None, True)
    tile_max = _score_stage(k_ref, jnp.maximum(qi - 1, 0), qt_scr, s_scr, True)

    def step(j, carry):
        stats, tile_max = carry
        value_stage(j - 1, stats)
        stats = softmax_stage(stats, tile_max, False)
        return stats, _score_stage(k_ref, qi - (j + 1), qt_scr, s_scr, True)

    stats, tile_max = lax.fori_loop(1, qi, step, (stats, tile_max))

    def last_step(stats):
        value_stage(qi - 1, stats)
        return softmax_stage(stats, tile_max, False)

    stats = lax.cond(qi > 0, last_step, lambda st: st, stats)
    value_stage(qi, stats)

    lam = (jnp.exp(jnp.sum(lq1_ref[...] * lk1_ref[...], axis=-1, keepdims=True))
           - jnp.exp(jnp.sum(lq2_ref[...] * lk2_ref[...], axis=-1, keepdims=True))
           + lambda_init)
    head_gain = sg_ref[...] * (1.0 - lambda_init)
    for hd in range(streams // 2):
        acc1, acc2 = acc_scr[2 * hd], acc_scr[2 * hd + 1]
        inv1 = 1.0 / acc1[vdim:vdim + 1]
        inv2 = lam / acc2[vdim:vdim + 1]
        out_t = acc1[:vdim] * inv1 - acc2[:vdim] * inv2
        ms = jnp.mean(out_t * out_t, axis=0, keepdims=True)
        out = (out_t * lax.rsqrt(ms + EPS)).T
        cols = slice(hd * vdim, (hd + 1) * vdim)
        o_ref[:, cols] = (out * head_gain * _silu(g_ref[:, cols].astype(F32))).astype(BF16)


def _diff_attention(y, k_tiles, vt_tiles, lq1, lk1, lq2, lk2, subln_g, *, batch, seq, width,
                    lambda_init):
    nq = seq // Q_TILE
    nkb = seq // KV_TILE
    cols = DIFF_STREAMS * HEAD_DIM
    g_col = width // cols
    vt_rows = DIFF_STREAMS // 2 * (2 * HEAD_DIM + SUM_ROWS)
    vec = lambda a: a.reshape(1, -1)
    small = lambda n: pl.BlockSpec((1, n), lambda b, h, i: (0, 0))
    return pl.pallas_call(
        functools.partial(_diff_kernel, tq=Q_TILE, lambda_init=lambda_init),
        grid=(batch, width // cols, nq),
        in_specs=[pl.BlockSpec((Q_TILE, cols), lambda b, h, i: (b * nq + i, h)),
                  pl.BlockSpec((1, nkb, cols // LANES, 2 * KV_TILE, LANES),
                               lambda b, h, i: (b, 0, h, 0, 0)),
                  pl.BlockSpec((1, nkb, vt_rows, KV_TILE), lambda b, h, i: (b, 0, h, 0)),
                  pl.BlockSpec((Q_TILE, cols), lambda b, h, i: (b * nq + i, g_col + h)),
                  small(HEAD_DIM), small(HEAD_DIM), small(HEAD_DIM), small(HEAD_DIM),
                  small(2 * HEAD_DIM)],
        out_specs=pl.BlockSpec((Q_TILE, cols), lambda b, h, i: (b * nq + i, h)),
        out_shape=jax.ShapeDtypeStruct((batch * seq, width), BF16),
        scratch_shapes=_attention_scratch(DIFF_STREAMS, Q_TILE, 2 * HEAD_DIM + SUM_ROWS),
        compiler_params=_params(3),
        name="differential_attention",
    )(y, k_tiles, vt_tiles, y, vec(lq1), vec(lk1), vec(lq2), vec(lk2), vec(subln_g))


def _post_kernel(*refs, final):
    if final:
        (h_ref, ma_ref, mb_ref, p_ref, woa_ref, wob_ref, gp_ref, wg_ref, wp_ref,
         fn_ref, o_ref) = refs
    else:
        h_ref, ma_ref, mb_ref, p_ref, woa_ref, wob_ref, gp_ref, wg_ref, wp_ref, o_ref = refs
    h = (h_ref[...]
         + jnp.dot(ma_ref[...], woa_ref[...], preferred_element_type=F32)
         + jnp.dot(mb_ref[...], wob_ref[...], preferred_element_type=F32))
    ms = jnp.mean(h * h, axis=-1, keepdims=True)
    hn = (h * lax.rsqrt(ms + EPS) * gp_ref[...]).astype(BF16)
    gate = _sigmoid(jnp.dot(hn, wg_ref[...], preferred_element_type=F32))
    ple = jnp.dot(p_ref[...].astype(BF16), wp_ref[...], preferred_element_type=F32)
    h = h + gate * ple
    if final:
        ms = jnp.mean(h * h, axis=-1, keepdims=True)
        h = h * lax.rsqrt(ms + EPS) * fn_ref[...]
    o_ref[...] = h


def _post(h, mix_a, mix_b, b_col, p, w_out, g_ple, w_gate, w_proj, final_norm=None, *, tm=1024):
    t, d = h.shape
    half = w_out.shape[0] // 2
    pd = p.shape[1]
    final = final_norm is not None
    const = lambda shape: pl.BlockSpec(shape, lambda i: (0, 0))
    in_specs = [pl.BlockSpec((tm, d), lambda i: (i, 0)),
                pl.BlockSpec((tm, half), lambda i: (i, 0)),
                pl.BlockSpec((tm, half), lambda i: (i, b_col)),
                pl.BlockSpec((tm, pd), lambda i: (i, 0)),
                pl.BlockSpec((half, d), lambda i: (0, 0)),
                pl.BlockSpec((half, d), lambda i: (1, 0)),
                const((1, d)), const((d, d)), const((pd, d))]
    args = [h, mix_a, mix_b, p, w_out, w_out, g_ple.reshape(1, d), w_gate, w_proj]
    if final:
        in_specs.append(const((1, d)))
        args.append(final_norm.reshape(1, d))
    return pl.pallas_call(
        functools.partial(_post_kernel, final=final),
        grid=(t // tm,),
        in_specs=in_specs,
        out_specs=pl.BlockSpec((tm, d), lambda i: (i, 0)),
        out_shape=jax.ShapeDtypeStruct((t, d), F32),
        compiler_params=_params(1),
        name="out_proj_ple_final" if final else "out_proj_ple",
    )(*args)


def _block_diag(w):
    nb, bs, _ = w.shape
    eye = jnp.eye(nb, dtype=w.dtype)
    return jnp.einsum('ncd,nm->ncmd', w, eye).reshape(nb * bs, nb * bs)


def kernel(x, p, positions, norm_mix, norm_ple, w_ple_gate, w_ple_proj, w_in_e, conv_w, conv_b,
           lru_wa, lru_ba, lru_wx, lru_bx, lru_lambda, w_out_e, w_in_o, lam_q1, lam_k1, lam_q2,
           lam_k2, subln_g, w_out_o, final_norm):
    batch, seq, d = x.shape
    depth = norm_mix.shape[0]
    t = batch * seq
    lru_w = conv_w.shape[-1]
    sb_w = (w_in_e.shape[-1] - 2 * lru_w) // 4
    diff_w = w_in_o.shape[-1] // 4

    half = HEAD_DIM // 2
    inv_freq = ROPE_THETA ** (-jnp.arange(0, HEAD_DIM, 2, dtype=F32) / HEAD_DIM)
    ang = positions.astype(F32).reshape(t, 1) * inv_freq
    cos = jnp.tile(jnp.cos(ang), (1, LANES // half))
    sin = jnp.tile(jnp.concatenate([-jnp.sin(ang), jnp.sin(ang)], axis=1), (1, LANES // HEAD_DIM))

    h = x.reshape(t, d)
    for i in range(depth):
        j = i // 2
        last = i == depth - 1
        if i % 2 == 0:
            w = w_in_e[j].astype(BF16)
            q0 = 2 * lru_w
            k0, v0, g0 = q0 + sb_w, q0 + 2 * sb_w, q0 + 3 * sb_w
            w_main = jnp.concatenate([w[:, q0:k0], w[:, g0:]], axis=1)
            w_gates = jnp.concatenate([_block_diag(lru_wa[j]), _block_diag(lru_wx[j])], axis=1)
            b_gates = jnp.concatenate([lru_ba[j], lru_bx[j]])
            lru = (w[:, :q0], conv_w[j], conv_b[j], w_gates.astype(BF16), b_gates, lru_lambda[j])
            y, k_tiles, vt_tiles, oa = _in_proj(
                h, norm_mix[i], w_main, w[:, k0:v0], w[:, v0:g0], lru=lru, batch=batch, seq=seq,
                q_range=(0, sb_w), scale=0.5 * HEAD_DIM ** -0.5, permute_keys=True)
            ob = _sb_attention(y, k_tiles, vt_tiles, batch=batch, seq=seq,
                               q0=0, g0=sb_w, width=sb_w)
            mix_a, mix_b, b_col, w_out = oa, ob, 0, w_out_e[j]
        else:
            lambda_init = 0.8 - 0.6 * math.exp(-0.3 * i)
            w = w_in_o[j].astype(BF16)
            w_main = jnp.concatenate([w[:, :diff_w], w[:, 3 * diff_w:]], axis=1)
            y, k_tiles, vt_tiles = _in_proj(
                h, norm_mix[i], w_main, w[:, diff_w:2 * diff_w], w[:, 2 * diff_w:3 * diff_w],
                cos, sin, batch=batch, seq=seq, q_range=(0, diff_w),
                scale=HEAD_DIM ** -0.5 * math.log2(math.e), permute_keys=False,
                value_head=2 * HEAD_DIM)
            o = _diff_attention(y, k_tiles, vt_tiles, lam_q1[j], lam_k1[j], lam_q2[j], lam_k2[j],
                                subln_g[j], batch=batch, seq=seq, width=diff_w,
                                lambda_init=lambda_init)
            mix_a, mix_b, b_col, w_out = o, o, 1, w_out_o[j]
        h = _post(h, mix_a, mix_b, b_col, p[i].reshape(t, -1), w_out.astype(BF16), norm_ple[i],
                  w_ple_gate[i].astype(BF16), w_ple_proj[i].astype(BF16),
                  final_norm if last else None)
    return h.reshape(batch, seq, d)
```

```python
import functools
import math

import jax
import jax.numpy as jnp
from jax import lax
from jax.experimental import pallas as pl
from jax.experimental.pallas import tpu as pltpu

F32 = jnp.float32
BF16 = jnp.bfloat16

EPS = 1e-6
HEAD_DIM = 64
LRU_C = 8.0
CONV_WIDTH = 4
ROPE_THETA = 10000.0

LANES = 128
SUBLANES = 8
KV_TILE = 256
Q_TILE = 256
SEGMENT = KV_TILE // SUBLANES
SB_STREAMS = 8
DIFF_STREAMS = 8
SUM_ROWS = 16
VMEM_LIMIT = 56 * 1024 * 1024


def _params(n_axes):
    return pltpu.CompilerParams(
        dimension_semantics=("arbitrary",) * n_axes, vmem_limit_bytes=VMEM_LIMIT)


def _sigmoid(x):
    return 0.5 + 0.5 * jnp.tanh(0.5 * x)


def _silu(x):
    return x * _sigmoid(x)


def _rotary(y, cos, sin, first_half):
    width = y.shape[1]
    partner = jnp.where(first_half, pltpu.roll(y, width - HEAD_DIM // 2, 1),
                        pltpu.roll(y, HEAD_DIM // 2, 1))
    return y * cos + partner * sin


def _in_proj_kernel(*refs, chunk, rope, q_range, scale, permute_keys, lru_steps, value_head):
    x_ref, g_ref, wm_ref, wk_ref, wv_ref = refs[:5]
    if rope:
        cos_ref, sin_ref = refs[5:7]
    if lru_steps:
        wl_ref, cw_ref, cb_ref, wg_ref, bg_ref, lam_ref = refs[5:11]
        m_ref, k_ref, vt_ref, oa_ref, ext_ref, h_ref = refs[11:]
    else:
        m_ref, k_ref, vt_ref = refs[-3:]
    tm = x_ref.shape[0]
    x = x_ref[...]
    ms = jnp.mean(x * x, axis=-1, keepdims=True)
    xn = x * lax.rsqrt(ms + EPS) * g_ref[...]
    xn_bf = xn.astype(BF16)
    if rope:
        reps = chunk // LANES
        cos = jnp.concatenate([cos_ref[...]] * reps, axis=1)
        sin = jnp.concatenate([sin_ref[...]] * reps, axis=1)
        lane = lax.broadcasted_iota(jnp.int32, (tm, chunk), 1)
        first_half = (lane % HEAD_DIM) < (HEAD_DIM // 2)

    lru_tasks = []
    if lru_steps:
        width = oa_ref.shape[1]

        @pl.when(pl.program_id(0) % lru_steps == 0)
        def _():
            ext_ref[0:SUBLANES, :] = jnp.zeros((SUBLANES, width), F32)
            h_ref[...] = jnp.zeros((1, width), F32)

        xa = jnp.dot(xn_bf, wl_ref[:, :width], preferred_element_type=F32)
        ga = jnp.dot(xn_bf, wl_ref[:, width:], preferred_element_type=F32)
        xc = _causal_conv(xa, cw_ref, cb_ref, ext_ref)

    def main_chunk(c):
        lo = c * chunk
        y = jnp.dot(xn_bf, wm_ref[:, lo:lo + chunk], preferred_element_type=F32)
        if q_range[0] <= lo < q_range[1]:
            if rope:
                y = _rotary(y, cos, sin, first_half)
            y = y * scale
        m_ref[:, lo:lo + chunk] = y.astype(BF16)

    n_tiles = tm // KV_TILE
    if permute_keys:
        r = lax.broadcasted_iota(jnp.int32, (KV_TILE, KV_TILE), 0)
        c = lax.broadcasted_iota(jnp.int32, (KV_TILE, KV_TILE), 1)
        perm = jnp.where(c == (r % SUBLANES) * SEGMENT + r // SUBLANES, 1.0, 0.0).astype(BF16)
        xk = jnp.concatenate(
            [jnp.dot(perm, xn_bf[t * KV_TILE:(t + 1) * KV_TILE, :],
                     preferred_element_type=F32).astype(BF16) for t in range(n_tiles)], axis=0)
    else:
        xk = xn_bf

    low = lax.broadcasted_iota(jnp.int32, (KV_TILE, LANES), 1) < HEAD_DIM

    def key_chunk(c):
        kk = jnp.dot(xk, wk_ref[:, c * chunk:(c + 1) * chunk], preferred_element_type=F32)
        if rope:
            kk = _rotary(kk, cos, sin, first_half)
        for t in range(n_tiles):
            for b in range(chunk // LANES):
                blk = kk[t * KV_TILE:(t + 1) * KV_TILE, b * LANES:(b + 1) * LANES]
                nb = c * (chunk // LANES) + b
                k_ref[0, t, nb, 0:KV_TILE, :] = jnp.where(low, blk, 0.0).astype(BF16)
                k_ref[0, t, nb, KV_TILE:2 * KV_TILE, :] = jnp.where(low, 0.0, blk).astype(BF16)
    def value_chunk(c):
        vv = jnp.dot(xk, wv_ref[:, c * chunk:(c + 1) * chunk], preferred_element_type=F32)
        for t in range(n_tiles):
            vt = vv[t * KV_TILE:(t + 1) * KV_TILE, :].T.astype(BF16)
            if not value_head:
                vt_ref[0, t, c * chunk:(c + 1) * chunk, :] = vt
                continue
            for hh in range(chunk // value_head):
                r0 = (c * (chunk // value_head) + hh) * (value_head + SUM_ROWS)
                vt_ref[0, t, r0:r0 + value_head, :] = vt[hh * value_head:(hh + 1) * value_head]
                vt_ref[0, t, r0 + value_head:r0 + value_head + SUM_ROWS, :] = jnp.ones(
                    (SUM_ROWS, KV_TILE), BF16)

    for c in range(wm_ref.shape[1] // chunk):
        main_chunk(c)
    mxu_tasks = ([functools.partial(key_chunk, c) for c in range(wk_ref.shape[1] // chunk)]
                 + [functools.partial(value_chunk, c) for c in range(wv_ref.shape[1] // chunk)])
    if lru_steps:
        gates = jnp.dot(xc.astype(BF16), wg_ref[...], preferred_element_type=F32) + bg_ref[...]
        ts = tm // 4

        def lru_rows(r0):
            rows = slice(r0, r0 + ts)
            _rg_lru_rows(xc[rows], gates[rows], ga[rows], lam_ref, oa_ref, r0, h_ref)

        lru_tasks = [functools.partial(lru_rows, r0) for r0 in range(0, tm, ts)]
    for n in range(max(len(mxu_tasks), len(lru_tasks))):
        for tasks in (mxu_tasks, lru_tasks):
            if n < len(tasks):
                tasks[n]()


def _in_proj(h, g, w_main, w_k, w_v, cos=None, sin=None, lru=None, *, batch, seq, q_range,
             scale, permute_keys, value_head=0, tm=1024, chunk=512):
    t, d = h.shape
    n_main, n_k, n_v = w_main.shape[1], w_k.shape[1], w_v.shape[1]
    vt_rows = n_v // value_head * (value_head + SUM_ROWS) if value_head else n_v
    rope = cos is not None
    assert not (rope and lru is not None)
    steps = seq // tm
    tiles = tm // KV_TILE
    nkb = seq // KV_TILE
    const = lambda shape: pl.BlockSpec(shape, lambda i: (0, 0))
    in_specs = [pl.BlockSpec((tm, d), lambda i: (i, 0)), const((1, d)),
                const((d, n_main)), const((d, n_k)), const((d, n_v))]
    args = [h, g.reshape(1, d), w_main, w_k, w_v]
    if rope:
        in_specs += [pl.BlockSpec((tm, LANES), lambda i: (i, 0))] * 2
        args += [cos, sin]
    out_specs = [pl.BlockSpec((tm, n_main), lambda i: (i, 0)),
                 pl.BlockSpec((1, tiles, n_k // LANES, 2 * KV_TILE, LANES),
                              lambda i: (i // steps, i % steps, 0, 0, 0)),
                 pl.BlockSpec((1, tiles, vt_rows, KV_TILE),
                              lambda i: (i // steps, i % steps, 0, 0))]
    out_shape = [jax.ShapeDtypeStruct((t, n_main), BF16),
                 jax.ShapeDtypeStruct((batch, nkb, n_k // LANES, 2 * KV_TILE, LANES), BF16),
                 jax.ShapeDtypeStruct((batch, nkb, vt_rows, KV_TILE), BF16)]
    scratch = []
    if lru is not None:
        w_lru, conv_w, conv_b, w_gates, b_gates, lam = lru
        width = conv_w.shape[1]
        in_specs += [const((d, 2 * width)), const((CONV_WIDTH, width)), const((1, width)),
                     const((width, 2 * width)), const((1, 2 * width)), const((1, width))]
        args += [w_lru, conv_w, conv_b.reshape(1, width), w_gates,
                 b_gates.reshape(1, 2 * width), lam.reshape(1, width)]
        out_specs.append(pl.BlockSpec((tm, width), lambda i: (i, 0)))
        out_shape.append(jax.ShapeDtypeStruct((t, width), BF16))
        scratch = [pltpu.VMEM((tm + SUBLANES, width), F32), pltpu.VMEM((1, width), F32)]
    return pl.pallas_call(
        functools.partial(_in_proj_kernel, chunk=chunk, rope=rope, q_range=q_range,
                          scale=scale, permute_keys=permute_keys, value_head=value_head,
                          lru_steps=steps if lru is not None else 0),
        grid=(t // tm,),
        in_specs=in_specs,
        out_specs=out_specs,
        out_shape=out_shape,
        scratch_shapes=scratch,
        compiler_params=_params(1),
        name="in_proj_rope" if rope else "in_proj_lru",
    )(*args)


def _causal_conv(xa, cw_ref, cb_ref, ext_ref):
    ts = xa.shape[0]
    ext_ref[SUBLANES:SUBLANES + ts, :] = xa
    xc = cb_ref[...] + cw_ref[CONV_WIDTH - 1:CONV_WIDTH, :] * xa
    for j in range(1, CONV_WIDTH):
        k = CONV_WIDTH - 1 - j
        xc = xc + cw_ref[k:k + 1, :] * ext_ref[SUBLANES - j:SUBLANES - j + ts, :]
    ext_ref[0:SUBLANES, :] = ext_ref[ts:ts + SUBLANES, :]
    return xc


def _rg_lru_rows(xc, gates, ga, lam_ref, o_ref, row0, h_ref):
    ts, width = xc.shape
    r = _sigmoid(gates[:, :width])
    i = _sigmoid(gates[:, width:])
    lam = lam_ref[...]
    log_sig_lam = jnp.minimum(lam, 0.0) - jnp.log(1.0 + jnp.exp(-jnp.abs(lam)))
    log_a = LRU_C * r * log_sig_lam
    a = jnp.exp(log_a)
    u = jnp.sqrt(1.0 - a * a) * (i * xc)

    groups = ts // SUBLANES
    a = a.reshape(groups, SUBLANES, width)
    u = u.reshape(groups, SUBLANES, width)
    row = lax.broadcasted_iota(jnp.int32, (groups, SUBLANES, width), 1)
    step = 1
    while step < SUBLANES:
        valid = row >= step
        a_prev = pltpu.roll(a, step, 1)
        u_prev = pltpu.roll(u, step, 1)
        u = jnp.where(valid, a * u_prev + u, u)
        a = jnp.where(valid, a * a_prev, a)
        step *= 2
    gate = _silu(ga)
    state = h_ref[...]
    for pair in range(ts // (2 * SUBLANES)):
        hs = []
        for grp in (2 * pair, 2 * pair + 1):
            h = u[grp] + a[grp] * state
            state = h[SUBLANES - 1:SUBLANES, :]
            hs.append(h)
        rows = slice(2 * pair * SUBLANES, 2 * (pair + 1) * SUBLANES)
        o_ref[row0 + rows.start:row0 + rows.stop, :] = (
            jnp.concatenate(hs, axis=0) * gate[rows]).astype(BF16)
    h_ref[...] = state


def _store_transposed_queries(q_ref, qt_scr):
    for blk in range(qt_scr.shape[0]):
        qt_scr[blk] = q_ref[:, blk * LANES:(blk + 1) * LANES].astype(F32).T.astype(BF16)


def _score_stage(k_ref, kb, qt_scr, s_scr, with_max=False):
    tile_max = []
    for blk in range(qt_scr.shape[0]):
        both = jnp.dot(k_ref[0, kb, blk], qt_scr[blk], preferred_element_type=F32)
        for half in range(2):
            s = both[half * KV_TILE:(half + 1) * KV_TILE]
            s_scr[2 * blk + half] = s
            if with_max:
                tile_max.append(jnp.max(s, axis=0, keepdims=True))
    return tuple(tile_max)


def _attention_scratch(streams, tq, value_rows):
    return [pltpu.VMEM((streams // 2, LANES, tq), BF16),
            pltpu.VMEM((streams, KV_TILE, tq), F32),
            pltpu.VMEM((streams, KV_TILE, tq), BF16),
            pltpu.VMEM((streams, value_rows, tq), F32)]


def _sb_kernel(q_ref, k_ref, vt_ref, g_ref, o_ref, qt_scr, s_scr, w_scr, acc_scr, *, tq):
    qi = pl.program_id(2)
    streams = s_scr.shape[0]
    _store_transposed_queries(q_ref, qt_scr)
    acc_scr[...] = jnp.zeros(acc_scr.shape, F32)

    r = lax.broadcasted_iota(jnp.int32, (KV_TILE, tq), 0)
    c = lax.broadcasted_iota(jnp.int32, (KV_TILE, tq), 1)
    causal = ((r % SUBLANES) * SEGMENT + r // SUBLANES) < c
    srow = lax.broadcasted_iota(jnp.int32, (SUBLANES, tq), 0)

    def weight_stage(later, masked):
        new_later = []
        for g in range(streams):
            run = jnp.ones((SUBLANES, tq), F32)
            for v in reversed(range(SEGMENT)):
                rows = slice(v * SUBLANES, (v + 1) * SUBLANES)
                half_tanh = 0.5 * jnp.tanh(s_scr[g, rows, :])
                beta = 0.5 + half_tanh
                rest = 0.5 - half_tanh
                if masked:
                    beta = jnp.where(causal[rows], beta, 0.0)
                    rest = jnp.where(causal[rows], rest, 1.0)
                s_scr[g, rows, :] = beta * run
                run = run * rest
            off = jnp.broadcast_to(later[g], (SUBLANES, tq))
            for s2 in range(1, SUBLANES):
                off = off * jnp.where(srow < s2, run[s2:s2 + 1, :], 1.0)
            off2 = jnp.concatenate([off, off], axis=0)
            for v in range(SEGMENT // 2):
                rows = slice(2 * v * SUBLANES, 2 * (v + 1) * SUBLANES)
                w_scr[g, rows, :] = (s_scr[g, rows, :] * off2).astype(BF16)
            new_later.append((off * run)[0:1, :])
        return tuple(new_later)

    def value_stage(t):
        vt_tile = vt_ref[0, qi - t]
        for g in range(streams):
            acc_scr[g] += jnp.dot(vt_tile[g * HEAD_DIM:(g + 1) * HEAD_DIM, :], w_scr[g],
                                  preferred_element_type=F32)

    _score_stage(k_ref, qi, qt_scr, s_scr)
    later = weight_stage(tuple(jnp.ones((1, tq), F32) for _ in range(streams)), True)
    _score_stage(k_ref, jnp.maximum(qi - 1, 0), qt_scr, s_scr)

    def step(j, later):
        value_stage(j - 1)
        later = weight_stage(later, False)
        _score_stage(k_ref, qi - (j + 1), qt_scr, s_scr)
        return later

    later = lax.fori_loop(1, qi, step, later)

    @pl.when(qi > 0)
    def _():
        value_stage(qi - 1)
        weight_stage(later, False)

    value_stage(qi)
    out_t = acc_scr[...].reshape(streams * HEAD_DIM, tq)
    o_ref[...] = (out_t.T * _silu(g_ref[...].astype(F32))).astype(BF16)


def _sb_attention(y, k_tiles, vt_tiles, *, batch, seq, q0, g0, width):
    nq = seq // Q_TILE
    nkb = seq // KV_TILE
    cols = SB_STREAMS * HEAD_DIM
    return pl.pallas_call(
        functools.partial(_sb_kernel, tq=Q_TILE),
        grid=(batch, width // cols, nq),
        in_specs=[pl.BlockSpec((Q_TILE, cols), lambda b, p, i: (b * nq + i, q0 // cols + p)),
                  pl.BlockSpec((1, nkb, cols // LANES, 2 * KV_TILE, LANES),
                               lambda b, p, i: (b, 0, p, 0, 0)),
                  pl.BlockSpec((1, nkb, cols, KV_TILE), lambda b, p, i: (b, 0, p, 0)),
                  pl.BlockSpec((Q_TILE, cols), lambda b, p, i: (b * nq + i, g0 // cols + p))],
        out_specs=pl.BlockSpec((Q_TILE, cols), lambda b, p, i: (b * nq + i, p)),
        out_shape=jax.ShapeDtypeStruct((batch * seq, width), BF16),
        scratch_shapes=_attention_scratch(SB_STREAMS, Q_TILE, HEAD_DIM),
        compiler_params=_params(3),
        name="stick_breaking_attention",
    )(y, k_tiles, vt_tiles, y)


def _diff_kernel(q_ref, k_ref, vt_ref, g_ref, lq1_ref, lk1_ref, lq2_ref, lk2_ref, sg_ref,
                 o_ref, qt_scr, s_scr, p_scr, acc_scr, *, tq, lambda_init):
    qi = pl.program_id(2)
    streams = s_scr.shape[0]
    _store_transposed_queries(q_ref, qt_scr)
    acc_scr[...] = jnp.zeros(acc_scr.shape, F32)
    r = lax.broadcasted_iota(jnp.int32, (KV_TILE, tq), 0)
    c = lax.broadcasted_iota(jnp.int32, (KV_TILE, tq), 1)
    causal = r <= c
    vdim = 2 * HEAD_DIM
    vrows = vdim + SUM_ROWS

    def softmax_stage(stats, tile_max, masked):
        new = []
        for g in range(streams):
            m, _ = stats[g]
            s = s_scr[g]
            if masked:
                s = jnp.where(causal, s, -jnp.inf)
                m_new = jnp.maximum(m, jnp.max(s, axis=0, keepdims=True))
            else:
                m_new = jnp.maximum(m, tile_max[g])
            p = jnp.exp2(s - m_new)
            alpha = jnp.exp2(m - m_new)
            p_scr[g] = p.astype(BF16)
            new.append((m_new, alpha))
        return tuple(new)

    def value_stage(t, stats):
        vt_tile = vt_ref[0, qi - t]
        for g in range(streams):
            hd = g // 2
            acc_scr[g] = stats[g][1] * acc_scr[g] + jnp.dot(
                vt_tile[hd * vrows:(hd + 1) * vrows, :], p_scr[g], preferred_element_type=F32)

    init = tuple((jnp.full((1, tq), -jnp.inf, F32), jnp.zeros((1, tq), F32))
                 for _ in range(streams))
    _score_stage(k_ref, qi, qt_scr, s_scr)
    stats = softmax_stage(init, None, True)
    tile_max = _score_stage(k_ref, jnp.maximum(qi - 1, 0), qt_scr, s_scr, True)

    def step(j, carry):
        stats, tile_max = carry
        value_stage(j - 1, stats)
        stats = softmax_stage(stats, tile_max, False)
        return stats, _score_stage(k_ref, qi - (j + 1), qt_scr, s_scr, True)

    stats, tile_max = lax.fori_loop(1, qi, step, (stats, tile_max))

    def last_step(stats):
        value_stage(qi - 1, stats)
        return softmax_stage(stats, tile_max, False)

    stats = lax.cond(qi > 0, last_step, lambda st: st, stats)
    value_stage(qi, stats)

    lam = (jnp.exp(jnp.sum(lq1_ref[...] * lk1_ref[...], axis=-1, keepdims=True))
           - jnp.exp(jnp.sum(lq2_ref[...] * lk2_ref[...], axis=-1, keepdims=True))
           + lambda_init)
    head_gain = sg_ref[...] * (1.0 - lambda_init)
    for hd in range(streams // 2):
        acc1, acc2 = acc_scr[2 * hd], acc_scr[2 * hd + 1]
        inv1 = 1.0 / acc1[vdim:vdim + 1]
        inv2 = lam / acc2[vdim:vdim + 1]
        out_t = acc1[:vdim] * inv1 - acc2[:vdim] * inv2
        ms = jnp.mean(out_t * out_t, axis=0, keepdims=True)
        out = (out_t * lax.rsqrt(ms + EPS)).T
        cols = slice(hd * vdim, (hd + 1) * vdim)
        o_ref[:, cols] = (out * head_gain * _silu(g_ref[:, cols].astype(F32))).astype(BF16)


def _diff_attention(y, k_tiles, vt_tiles, lq1, lk1, lq2, lk2, subln_g, *, batch, seq, width,
                    lambda_init):
    nq = seq // Q_TILE
    nkb = seq // KV_TILE
    cols = DIFF_STREAMS * HEAD_DIM
    g_col = width // cols
    vt_rows = DIFF_STREAMS // 2 * (2 * HEAD_DIM + SUM_ROWS)
    vec = lambda a: a.reshape(1, -1)
    small = lambda n: pl.BlockSpec((1, n), lambda b, h, i: (0, 0))
    return pl.pallas_call(
        functools.partial(_diff_kernel, tq=Q_TILE, lambda_init=lambda_init),
        grid=(batch, width // cols, nq),
        in_specs=[pl.BlockSpec((Q_TILE, cols), lambda b, h, i: (b * nq + i, h)),
                  pl.BlockSpec((1, nkb, cols // LANES, 2 * KV_TILE, LANES),
                               lambda b, h, i: (b, 0, h, 0, 0)),
                  pl.BlockSpec((1, nkb, vt_rows, KV_TILE), lambda b, h, i: (b, 0, h, 0)),
                  pl.BlockSpec((Q_TILE, cols), lambda b, h, i: (b * nq + i, g_col + h)),
                  small(HEAD_DIM), small(HEAD_DIM), small(HEAD_DIM), small(HEAD_DIM),
                  small(2 * HEAD_DIM)],
        out_specs=pl.BlockSpec((Q_TILE, cols), lambda b, h, i: (b * nq + i, h)),
        out_shape=jax.ShapeDtypeStruct((batch * seq, width), BF16),
        scratch_shapes=_attention_scratch(DIFF_STREAMS, Q_TILE, 2 * HEAD_DIM + SUM_ROWS),
        compiler_params=_params(3),
        name="differential_attention",
    )(y, k_tiles, vt_tiles, y, vec(lq1), vec(lk1), vec(lq2), vec(lk2), vec(subln_g))


def _post_kernel(*refs, final):
    if final:
        (h_ref, ma_ref, mb_ref, p_ref, woa_ref, wob_ref, gp_ref, wg_ref, wp_ref,
         fn_ref, o_ref) = refs
    else:
        h_ref, ma_ref, mb_ref, p_ref, woa_ref, wob_ref, gp_ref, wg_ref, wp_ref, o_ref = refs
    h = (h_ref[...]
         + jnp.dot(ma_ref[...], woa_ref[...], preferred_element_type=F32)
         + jnp.dot(mb_ref[...], wob_ref[...], preferred_element_type=F32))
    ms = jnp.mean(h * h, axis=-1, keepdims=True)
    hn = (h * lax.rsqrt(ms + EPS) * gp_ref[...]).astype(BF16)
    gate = _sigmoid(jnp.dot(hn, wg_ref[...], preferred_element_type=F32))
    ple = jnp.dot(p_ref[...].astype(BF16), wp_ref[...], preferred_element_type=F32)
    h = h + gate * ple
    if final:
        ms = jnp.mean(h * h, axis=-1, keepdims=True)
        h = h * lax.rsqrt(ms + EPS) * fn_ref[...]
    o_ref[...] = h


def _post(h, mix_a, mix_b, b_col, p, w_out, g_ple, w_gate, w_proj, final_norm=None, *, tm=1024):
    t, d = h.shape
    half = w_out.shape[0] // 2
    pd = p.shape[1]
    final = final_norm is not None
    const = lambda shape: pl.BlockSpec(shape, lambda i: (0, 0))
    in_specs = [pl.BlockSpec((tm, d), lambda i: (i, 0)),
                pl.BlockSpec((tm, half), lambda i: (i, 0)),
                pl.BlockSpec((tm, half), lambda i: (i, b_col)),
                pl.BlockSpec((tm, pd), lambda i: (i, 0)),
                pl.BlockSpec((half, d), lambda i: (0, 0)),
                pl.BlockSpec((half, d), lambda i: (1, 0)),
                const((1, d)), const((d, d)), const((pd, d))]
    args = [h, mix_a, mix_b, p, w_out, w_out, g_ple.reshape(1, d), w_gate, w_proj]
    if final:
        in_specs.append(const((1, d)))
        args.append(final_norm.reshape(1, d))
    return pl.pallas_call(
        functools.partial(_post_kernel, final=final),
        grid=(t // tm,),
        in_specs=in_specs,
        out_specs=pl.BlockSpec((tm, d), lambda i: (i, 0)),
        out_shape=jax.ShapeDtypeStruct((t, d), F32),
        compiler_params=_params(1),
        name="out_proj_ple_final" if final else "out_proj_ple",
    )(*args)


def _block_diag(w):
    nb, bs, _ = w.shape
    eye = jnp.eye(nb, dtype=w.dtype)
    return jnp.einsum('ncd,nm->ncmd', w, eye).reshape(nb * bs, nb * bs)


def kernel(x, p, positions, norm_mix, norm_ple, w_ple_gate, w_ple_proj, w_in_e, conv_w, conv_b,
           lru_wa, lru_ba, lru_wx, lru_bx, lru_lambda, w_out_e, w_in_o, lam_q1, lam_k1, lam_q2,
           lam_k2, subln_g, w_out_o, final_norm):
    batch, seq, d = x.shape
    depth = norm_mix.shape[0]
    t = batch * seq
    lru_w = conv_w.shape[-1]
    sb_w = (w_in_e.shape[-1] - 2 * lru_w) // 4
    diff_w = w_in_o.shape[-1] // 4

    half = HEAD_DIM // 2
    inv_freq = ROPE_THETA ** (-jnp.arange(0, HEAD_DIM, 2, dtype=F32) / HEAD_DIM)
    ang = positions.astype(F32).reshape(t, 1) * inv_freq
    cos = jnp.tile(jnp.cos(ang), (1, LANES // half))
    sin = jnp.tile(jnp.concatenate([-jnp.sin(ang), jnp.sin(ang)], axis=1), (1, LANES // HEAD_DIM))

    h = x.reshape(t, d)
    for i in range(depth):
        j = i // 2
        last = i == depth - 1
        if i % 2 == 0:
            w = w_in_e[j].astype(BF16)
            q0 = 2 * lru_w
            k0, v0, g0 = q0 + sb_w, q0 + 2 * sb_w, q0 + 3 * sb_w
            w_main = jnp.concatenate([w[:, q0:k0], w[:, g0:]], axis=1)
            w_gates = jnp.concatenate([_block_diag(lru_wa[j]), _block_diag(lru_wx[j])], axis=1)
            b_gates = jnp.concatenate([lru_ba[j], lru_bx[j]])
            lru = (w[:, :q0], conv_w[j], conv_b[j], w_gates.astype(BF16), b_gates, lru_lambda[j])
            y, k_tiles, vt_tiles, oa = _in_proj(
                h, norm_mix[i], w_main, w[:, k0:v0], w[:, v0:g0], lru=lru, batch=batch, seq=seq,
                q_range=(0, sb_w), scale=0.5 * HEAD_DIM ** -0.5, permute_keys=True)
            ob = _sb_attention(y, k_tiles, vt_tiles, batch=batch, seq=seq,
                               q0=0, g0=sb_w, width=sb_w)
            mix_a, mix_b, b_col, w_out = oa, ob, 0, w_out_e[j]
        else:
            lambda_init = 0.8 - 0.6 * math.exp(-0.3 * i)
            w = w_in_o[j].astype(BF16)
            w_main = jnp.concatenate([w[:, :diff_w], w[:, 3 * diff_w:]], axis=1)
            y, k_tiles, vt_tiles = _in_proj(
                h, norm_mix[i], w_main, w[:, diff_w:2 * diff_w], w[:, 2 * diff_w:3 * diff_w],
                cos, sin, batch=batch, seq=seq, q_range=(0, diff_w),
                scale=HEAD_DIM ** -0.5 * math.log2(math.e), permute_keys=False,
                value_head=2 * HEAD_DIM)
            o = _diff_attention(y, k_tiles, vt_tiles, lam_q1[j], lam_k1[j], lam_q2[j], lam_k2[j],
                                subln_g[j], batch=batch, seq=seq, width=diff_w,
                                lambda_init=lambda_init)
            mix_a, mix_b, b_col, w_out = o, o, 1, w_out_o[j]
        h = _post(h, mix_a, mix_b, b_col, p[i].reshape(t, -1), w_out.astype(BF16), norm_ple[i],
                  w_ple_gate[i].astype(BF16), w_ple_proj[i].astype(BF16),
                  final_norm if last else None)
    return h.reshape(batch, seq, d)
```

```python
import functools
import math

import jax
import jax.numpy as jnp
from jax import lax
from jax.experimental import pallas as pl
from jax.experimental.pallas import tpu as pltpu

F32 = jnp.float32
BF16 = jnp.bfloat16

EPS = 1e-6
HEAD_DIM = 64
LRU_C = 8.0
CONV_WIDTH = 4
ROPE_THETA = 10000.0

LANES = 128
SUBLANES = 8
KV_TILE = 256
Q_TILE = 256
SEGMENT = KV_TILE // SUBLANES
SB_STREAMS = 8
DIFF_STREAMS = 8
SUM_ROWS = 16
VMEM_LIMIT = 48 * 1024 * 1024
IN_PROJ_VMEM_LIMIT = 56 * 1024 * 1024


def _params(n_axes, vmem_limit=VMEM_LIMIT):
    return pltpu.CompilerParams(
        dimension_semantics=("arbitrary",) * n_axes, vmem_limit_bytes=vmem_limit)


def _sigmoid(x):
    return 0.5 + 0.5 * jnp.tanh(0.5 * x)


def _silu(x):
    return x * _sigmoid(x)


def _rotary(y, cos, sin, first_half):
    width = y.shape[1]
    partner = jnp.where(first_half, pltpu.roll(y, width - HEAD_DIM // 2, 1),
                        pltpu.roll(y, HEAD_DIM // 2, 1))
    return y * cos + partner * sin


def _in_proj_kernel(*refs, chunk, rope, q_range, scale, permute_keys, lru_steps, value_head):
    x_ref, g_ref, wm_ref, wk_ref, wv_ref = refs[:5]
    if rope:
        cos_ref, sin_ref = refs[5:7]
    if lru_steps:
        wl_ref, cw_ref, cb_ref, wg_ref, bg_ref, lam_ref = refs[5:11]
        m_ref, k_ref, vt_ref, oa_ref, ext_ref, h_ref = refs[11:]
    else:
        m_ref, k_ref, vt_ref = refs[-3:]
    tm = x_ref.shape[0]
    x = x_ref[...]
    ms = jnp.mean(x * x, axis=-1, keepdims=True)
    xn = x * lax.rsqrt(ms + EPS) * g_ref[...]
    xn_bf = xn.astype(BF16)
    if rope:
        reps = chunk // LANES
        cos = jnp.concatenate([cos_ref[...]] * reps, axis=1)
        sin = jnp.concatenate([sin_ref[...]] * reps, axis=1)
        lane = lax.broadcasted_iota(jnp.int32, (tm, chunk), 1)
        first_half = (lane % HEAD_DIM) < (HEAD_DIM // 2)

    lru_tasks = []
    if lru_steps:
        width = oa_ref.shape[1]

        @pl.when(pl.program_id(0) % lru_steps == 0)
        def _():
            ext_ref[0:SUBLANES, :] = jnp.zeros((SUBLANES, width), F32)
            h_ref[...] = jnp.zeros((1, width), F32)

        xa = jnp.dot(xn_bf, wl_ref[:, :width], preferred_element_type=F32)
        ga = jnp.dot(xn_bf, wl_ref[:, width:], preferred_element_type=F32)
        xc = _causal_conv(xa, cw_ref, cb_ref, ext_ref)

    def main_chunk(c):
        lo = c * chunk
        y = jnp.dot(xn_bf, wm_ref[:, lo:lo + chunk], preferred_element_type=F32)
        if q_range[0] <= lo < q_range[1]:
            if rope:
                y = _rotary(y, cos, sin, first_half)
            y = y * scale
        m_ref[:, lo:lo + chunk] = y.astype(BF16)

    n_tiles = tm // KV_TILE
    if permute_keys:
        r = lax.broadcasted_iota(jnp.int32, (KV_TILE, KV_TILE), 0)
        c = lax.broadcasted_iota(jnp.int32, (KV_TILE, KV_TILE), 1)
        perm = jnp.where(c == (r % SUBLANES) * SEGMENT + r // SUBLANES, 1.0, 0.0).astype(BF16)
        xk = jnp.concatenate(
            [jnp.dot(perm, xn_bf[t * KV_TILE:(t + 1) * KV_TILE, :],
                     preferred_element_type=F32).astype(BF16) for t in range(n_tiles)], axis=0)
    else:
        xk = xn_bf

    low = lax.broadcasted_iota(jnp.int32, (KV_TILE, LANES), 1) < HEAD_DIM

    def key_chunk(c):
        kk = jnp.dot(xk, wk_ref[:, c * chunk:(c + 1) * chunk], preferred_element_type=F32)
        if rope:
            kk = _rotary(kk, cos, sin, first_half)
        for t in range(n_tiles):
            for b in range(chunk // LANES):
                blk = kk[t * KV_TILE:(t + 1) * KV_TILE, b * LANES:(b + 1) * LANES]
                nb = c * (chunk // LANES) + b
                k_ref[0, t, nb, 0:KV_TILE, :] = jnp.where(low, blk, 0.0).astype(BF16)
                k_ref[0, t, nb, KV_TILE:2 * KV_TILE, :] = jnp.where(low, 0.0, blk).astype(BF16)
    def value_chunk(c):
        vv = jnp.dot(xk, wv_ref[:, c * chunk:(c + 1) * chunk], preferred_element_type=F32)
        for t in range(n_tiles):
            vt = vv[t * KV_TILE:(t + 1) * KV_TILE, :].T.astype(BF16)
            if not value_head:
                vt_ref[0, t, c * chunk:(c + 1) * chunk, :] = vt
                continue
            for hh in range(chunk // value_head):
                r0 = (c * (chunk // value_head) + hh) * (value_head + SUM_ROWS)
                vt_ref[0, t, r0:r0 + value_head, :] = vt[hh * value_head:(hh + 1) * value_head]
                vt_ref[0, t, r0 + value_head:r0 + value_head + SUM_ROWS, :] = jnp.ones(
                    (SUM_ROWS, KV_TILE), BF16)

    for c in range(wm_ref.shape[1] // chunk):
        main_chunk(c)
    mxu_tasks = ([functools.partial(key_chunk, c) for c in range(wk_ref.shape[1] // chunk)]
                 + [functools.partial(value_chunk, c) for c in range(wv_ref.shape[1] // chunk)])
    if lru_steps:
        gates = jnp.dot(xc.astype(BF16), wg_ref[...], preferred_element_type=F32) + bg_ref[...]
        ts = tm // 4

        def lru_rows(r0):
            rows = slice(r0, r0 + ts)
            _rg_lru_rows(xc[rows], gates[rows], ga[rows], lam_ref, oa_ref, r0, h_ref)

        lru_tasks = [functools.partial(lru_rows, r0) for r0 in range(0, tm, ts)]
    for n in range(max(len(mxu_tasks), len(lru_tasks))):
        for tasks in (mxu_tasks, lru_tasks):
            if n < len(tasks):
                tasks[n]()


def _in_proj(h, g, w_main, w_k, w_v, cos=None, sin=None, lru=None, *, batch, seq, q_range,
             scale, permute_keys, value_head=0, tm=1024, chunk=512):
    t, d = h.shape
    n_main, n_k, n_v = w_main.shape[1], w_k.shape[1], w_v.shape[1]
    vt_rows = n_v // value_head * (value_head + SUM_ROWS) if value_head else n_v
    rope = cos is not None
    assert not (rope and lru is not None)
    steps = seq // tm
    tiles = tm // KV_TILE
    nkb = seq // KV_TILE
    const = lambda shape: pl.BlockSpec(shape, lambda i: (0, 0))
    in_specs = [pl.BlockSpec((tm, d), lambda i: (i, 0)), const((1, d)),
                const((d, n_main)), const((d, n_k)), const((d, n_v))]
    args = [h, g.reshape(1, d), w_main, w_k, w_v]
    if rope:
        in_specs += [pl.BlockSpec((tm, LANES), lambda i: (i, 0))] * 2
        args += [cos, sin]
    out_specs = [pl.BlockSpec((tm, n_main), lambda i: (i, 0)),
                 pl.BlockSpec((1, tiles, n_k // LANES, 2 * KV_TILE, LANES),
                              lambda i: (i // steps, i % steps, 0, 0, 0)),
                 pl.BlockSpec((1, tiles, vt_rows, KV_TILE),
                              lambda i: (i // steps, i % steps, 0, 0))]
    out_shape = [jax.ShapeDtypeStruct((t, n_main), BF16),
                 jax.ShapeDtypeStruct((batch, nkb, n_k // LANES, 2 * KV_TILE, LANES), BF16),
                 jax.ShapeDtypeStruct((batch, nkb, vt_rows, KV_TILE), BF16)]
    scratch = []
    if lru is not None:
        w_lru, conv_w, conv_b, w_gates, b_gates, lam = lru
        width = conv_w.shape[1]
        in_specs += [const((d, 2 * width)), const((CONV_WIDTH, width)), const((1, width)),
                     const((width, 2 * width)), const((1, 2 * width)), const((1, width))]
        args += [w_lru, conv_w, conv_b.reshape(1, width), w_gates,
                 b_gates.reshape(1, 2 * width), lam.reshape(1, width)]
        out_specs.append(pl.BlockSpec((tm, width), lambda i: (i, 0)))
        out_shape.append(jax.ShapeDtypeStruct((t, width), BF16))
        scratch = [pltpu.VMEM((tm + SUBLANES, width), F32), pltpu.VMEM((1, width), F32)]
    return pl.pallas_call(
        functools.partial(_in_proj_kernel, chunk=chunk, rope=rope, q_range=q_range,
                          scale=scale, permute_keys=permute_keys, value_head=value_head,
                          lru_steps=steps if lru is not None else 0),
        grid=(t // tm,),
        in_specs=in_specs,
        out_specs=out_specs,
        out_shape=out_shape,
        scratch_shapes=scratch,
        compiler_params=_params(1, IN_PROJ_VMEM_LIMIT),
        name="in_proj_rope" if rope else "in_proj_lru",
    )(*args)


def _causal_conv(xa, cw_ref, cb_ref, ext_ref):
    ts = xa.shape[0]
    ext_ref[SUBLANES:SUBLANES + ts, :] = xa
    xc = cb_ref[...] + cw_ref[CONV_WIDTH - 1:CONV_WIDTH, :] * xa
    for j in range(1, CONV_WIDTH):
        k = CONV_WIDTH - 1 - j
        xc = xc + cw_ref[k:k + 1, :] * ext_ref[SUBLANES - j:SUBLANES - j + ts, :]
    ext_ref[0:SUBLANES, :] = ext_ref[ts:ts + SUBLANES, :]
    return xc


def _rg_lru_rows(xc, gates, ga, lam_ref, o_ref, row0, h_ref):
    ts, width = xc.shape
    r = _sigmoid(gates[:, :width])
    i = _sigmoid(gates[:, width:])
    lam = lam_ref[...]
    log_sig_lam = jnp.minimum(lam, 0.0) - jnp.log(1.0 + jnp.exp(-jnp.abs(lam)))
    log_a = LRU_C * r * log_sig_lam
    a = jnp.exp(log_a)
    u = jnp.sqrt(1.0 - a * a) * (i * xc)

    groups = ts // SUBLANES
    a = a.reshape(groups, SUBLANES, width)
    u = u.reshape(groups, SUBLANES, width)
    row = lax.broadcasted_iota(jnp.int32, (groups, SUBLANES, width), 1)
    step = 1
    while step < SUBLANES:
        valid = row >= step
        a_prev = pltpu.roll(a, step, 1)
        u_prev = pltpu.roll(u, step, 1)
        u = jnp.where(valid, a * u_prev + u, u)
        a = jnp.where(valid, a * a_prev, a)
        step *= 2
    gate = _silu(ga)
    state = h_ref[...]
    for pair in range(ts // (2 * SUBLANES)):
        hs = []
        for grp in (2 * pair, 2 * pair + 1):
            h = u[grp] + a[grp] * state
            state = h[SUBLANES - 1:SUBLANES, :]
            hs.append(h)
        rows = slice(2 * pair * SUBLANES, 2 * (pair + 1) * SUBLANES)
        o_ref[row0 + rows.start:row0 + rows.stop, :] = (
            jnp.concatenate(hs, axis=0) * gate[rows]).astype(BF16)
    h_ref[...] = state


def _store_transposed_queries(q_ref, qt_scr):
    for blk in range(qt_scr.shape[0]):
        qt_scr[blk] = q_ref[:, blk * LANES:(blk + 1) * LANES].astype(F32).T.astype(BF16)


def _score_stage(k_ref, kb, qt_scr, s_scr, with_max=False):
    tile_max = []
    for blk in range(qt_scr.shape[0]):
        both = jnp.dot(k_ref[0, kb, blk], qt_scr[blk], preferred_element_type=F32)
        for half in range(2):
            s = both[half * KV_TILE:(half + 1) * KV_TILE]
            s_scr[2 * blk + half] = s
            if with_max:
                tile_max.append(jnp.max(s, axis=0, keepdims=True))
    return tuple(tile_max)


def _attention_scratch(streams, tq, value_rows):
    return [pltpu.VMEM((streams // 2, LANES, tq), BF16),
            pltpu.VMEM((streams, KV_TILE, tq), F32),
            pltpu.VMEM((streams, KV_TILE, tq), BF16),
            pltpu.VMEM((streams, value_rows, tq), F32)]


def _sb_kernel(q_ref, k_ref, vt_ref, g_ref, o_ref, qt_scr, s_scr, w_scr, acc_scr, *, tq):
    qi = pl.program_id(2)
    streams = s_scr.shape[0]
    _store_transposed_queries(q_ref, qt_scr)
    acc_scr[...] = jnp.zeros(acc_scr.shape, F32)

    r = lax.broadcasted_iota(jnp.int32, (KV_TILE, tq), 0)
    c = lax.broadcasted_iota(jnp.int32, (KV_TILE, tq), 1)
    causal = ((r % SUBLANES) * SEGMENT + r // SUBLANES) < c
    srow = lax.broadcasted_iota(jnp.int32, (SUBLANES, tq), 0)

    def weight_stage(later, masked):
        new_later = []
        for g in range(streams):
            run = jnp.ones((SUBLANES, tq), F32)
            for v in reversed(range(SEGMENT)):
                rows = slice(v * SUBLANES, (v + 1) * SUBLANES)
                half_tanh = 0.5 * jnp.tanh(s_scr[g, rows, :])
                beta = 0.5 + half_tanh
                rest = 0.5 - half_tanh
                if masked:
                    beta = jnp.where(causal[rows], beta, 0.0)
                    rest = jnp.where(causal[rows], rest, 1.0)
                s_scr[g, rows, :] = beta * run
                run = run * rest
            off = jnp.broadcast_to(later[g], (SUBLANES, tq))
            for s2 in range(1, SUBLANES):
                off = off * jnp.where(srow < s2, run[s2:s2 + 1, :], 1.0)
            off2 = jnp.concatenate([off, off], axis=0)
            for v in range(SEGMENT // 2):
                rows = slice(2 * v * SUBLANES, 2 * (v + 1) * SUBLANES)
                w_scr[g, rows, :] = (s_scr[g, rows, :] * off2).astype(BF16)
            new_later.append((off * run)[0:1, :])
        return tuple(new_later)

    def value_stage(t):
        vt_tile = vt_ref[0, qi - t]
        for g in range(streams):
            acc_scr[g] += jnp.dot(vt_tile[g * HEAD_DIM:(g + 1) * HEAD_DIM, :], w_scr[g],
                                  preferred_element_type=F32)

    _score_stage(k_ref, qi, qt_scr, s_scr)
    later = weight_stage(tuple(jnp.ones((1, tq), F32) for _ in range(streams)), True)
    _score_stage(k_ref, jnp.maximum(qi - 1, 0), qt_scr, s_scr)

    def step(j, later):
        value_stage(j - 1)
        later = weight_stage(later, False)
        _score_stage(k_ref, qi - (j + 1), qt_scr, s_scr)
        return later

    later = lax.fori_loop(1, qi, step, later)

    @pl.when(qi > 0)
    def _():
        value_stage(qi - 1)
        weight_stage(later, False)

    value_stage(qi)
    out_t = acc_scr[...].reshape(streams * HEAD_DIM, tq)
    o_ref[...] = (out_t.T * _silu(g_ref[...].astype(F32))).astype(BF16)


def _sb_attention(y, k_tiles, vt_tiles, *, batch, seq, q0, g0, width):
    nq = seq // Q_TILE
    nkb = seq // KV_TILE
    cols = SB_STREAMS * HEAD_DIM
    return pl.pallas_call(
        functools.partial(_sb_kernel, tq=Q_TILE),
        grid=(batch, width // cols, nq),
        in_specs=[pl.BlockSpec((Q_TILE, cols), lambda b, p, i: (b * nq + i, q0 // cols + p)),
                  pl.BlockSpec((1, nkb, cols // LANES, 2 * KV_TILE, LANES),
                               lambda b, p, i: (b, 0, p, 0, 0)),
                  pl.BlockSpec((1, nkb, cols, KV_TILE), lambda b, p, i: (b, 0, p, 0)),
                  pl.BlockSpec((Q_TILE, cols), lambda b, p, i: (b * nq + i, g0 // cols + p))],
        out_specs=pl.BlockSpec((Q_TILE, cols), lambda b, p, i: (b * nq + i, p)),
        out_shape=jax.ShapeDtypeStruct((batch * seq, width), BF16),
        scratch_shapes=_attention_scratch(SB_STREAMS, Q_TILE, HEAD_DIM),
        compiler_params=_params(3),
        name="stick_breaking_attention",
    )(y, k_tiles, vt_tiles, y)


def _diff_kernel(q_ref, k_ref, vt_ref, g_ref, lq1_ref, lk1_ref, lq2_ref, lk2_ref, sg_ref,
                 o_ref, qt_scr, s_scr, p_scr, acc_scr, *, tq, lambda_init):
    qi = pl.program_id(2)
    streams = s_scr.shape[0]
    _store_transposed_queries(q_ref, qt_scr)
    acc_scr[...] = jnp.zeros(acc_scr.shape, F32)
    r = lax.broadcasted_iota(jnp.int32, (KV_TILE, tq), 0)
    c = lax.broadcasted_iota(jnp.int32, (KV_TILE, tq), 1)
    causal = r <= c
    vdim = 2 * HEAD_DIM
    vrows = vdim + SUM_ROWS

    def softmax_stage(stats, tile_max, masked):
        new = []
        for g in range(streams):
            m, _ = stats[g]
            s = s_scr[g]
            if masked:
                s = jnp.where(causal, s, -jnp.inf)
                m_new = jnp.maximum(m, jnp.max(s, axis=0, keepdims=True))
            else:
                m_new = jnp.maximum(m, tile_max[g])
            p = jnp.exp2(s - m_new)
            alpha = jnp.exp2(m - m_new)
            p_scr[g] = p.astype(BF16)
            new.append((m_new, alpha))
        return tuple(new)

    def value_stage(t, stats):
        vt_tile = vt_ref[0, qi - t]
        for g in range(streams):
            hd = g // 2
            acc_scr[g] = stats[g][1] * acc_scr[g] + jnp.dot(
                vt_tile[hd * vrows:(hd + 1) * vrows, :], p_scr[g], preferred_element_type=F32)

    init = tuple((jnp.full((1, tq), -jnp.inf, F32), jnp.zeros((1, tq), F32))
                 for _ in range(streams))
    _score_stage(k_ref, qi, qt_scr, s_scr)
    stats = softmax_stage(init, None, True)
    tile_max = _score_stage(k_ref, jnp.maximum(qi - 1, 0), qt_scr, s_scr, True)

    def step(j, carry):
        stats, tile_max = carry
        value_stage(j - 1, stats)
        stats = softmax_stage(stats, tile_max, False)
        return stats, _score_stage(k_ref, qi - (j + 1), qt_scr, s_scr, True)

    stats, tile_max = lax.fori_loop(1, qi, step, (stats, tile_max))

    def last_step(stats):
        value_stage(qi - 1, stats)
        return softmax_stage(stats, tile_max, False)

    stats = lax.cond(qi > 0, last_step, lambda st: st, stats)
    value_stage(qi, stats)

    lam = (jnp.exp(jnp.sum(lq1_ref[...] * lk1_ref[...], axis=-1, keepdims=True))
           - jnp.exp(jnp.sum(lq2_ref[...] * lk2_ref[...], axis=-1, keepdims=True))
           + lambda_init)
    head_gain = sg_ref[...] * (1.0 - lambda_init)
    for hd in range(streams // 2):
        acc1, acc2 = acc_scr[2 * hd], acc_scr[2 * hd + 1]
        inv1 = 1.0 / acc1[vdim:vdim + 1]
        inv2 = lam / acc2[vdim:vdim + 1]
        out_t = acc1[:vdim] * inv1 - acc2[:vdim] * inv2
        ms = jnp.mean(out_t * out_t, axis=0, keepdims=True)
        out = (out_t * lax.rsqrt(ms + EPS)).T
        cols = slice(hd * vdim, (hd + 1) * vdim)
        o_ref[:, cols] = (out * head_gain * _silu(g_ref[:, cols].astype(F32))).astype(BF16)


def _diff_attention(y, k_tiles, vt_tiles, lq1, lk1, lq2, lk2, subln_g, *, batch, seq, width,
                    lambda_init):
    nq = seq // Q_TILE
    nkb = seq // KV_TILE
    cols = DIFF_STREAMS * HEAD_DIM
    g_col = width // cols
    vt_rows = DIFF_STREAMS // 2 * (2 * HEAD_DIM + SUM_ROWS)
    vec = lambda a: a.reshape(1, -1)
    small = lambda n: pl.BlockSpec((1, n), lambda b, h, i: (0, 0))
    return pl.pallas_call(
        functools.partial(_diff_kernel, tq=Q_TILE, lambda_init=lambda_init),
        grid=(batch, width // cols, nq),
        in_specs=[pl.BlockSpec((Q_TILE, cols), lambda b, h, i: (b * nq + i, h)),
                  pl.BlockSpec((1, nkb, cols // LANES, 2 * KV_TILE, LANES),
                               lambda b, h, i: (b, 0, h, 0, 0)),
                  pl.BlockSpec((1, nkb, vt_rows, KV_TILE), lambda b, h, i: (b, 0, h, 0)),
                  pl.BlockSpec((Q_TILE, cols), lambda b, h, i: (b * nq + i, g_col + h)),
                  small(HEAD_DIM), small(HEAD_DIM), small(HEAD_DIM), small(HEAD_DIM),
                  small(2 * HEAD_DIM)],
        out_specs=pl.BlockSpec((Q_TILE, cols), lambda b, h, i: (b * nq + i, h)),
        out_shape=jax.ShapeDtypeStruct((batch * seq, width), BF16),
        scratch_shapes=_attention_scratch(DIFF_STREAMS, Q_TILE, 2 * HEAD_DIM + SUM_ROWS),
        compiler_params=_params(3),
        name="differential_attention",
    )(y, k_tiles, vt_tiles, y, vec(lq1), vec(lk1), vec(lq2), vec(lk2), vec(subln_g))


def _post_kernel(*refs, final):
    if final:
        (h_ref, ma_ref, mb_ref, p_ref, woa_ref, wob_ref, gp_ref, wg_ref, wp_ref,
         fn_ref, o_ref) = refs
    else:
        h_ref, ma_ref, mb_ref, p_ref, woa_ref, wob_ref, gp_ref, wg_ref, wp_ref, o_ref = refs
    h = (h_ref[...]
         + jnp.dot(ma_ref[...], woa_ref[...], preferred_element_type=F32)
         + jnp.dot(mb_ref[...], wob_ref[...], preferred_element_type=F32))
    ms = jnp.mean(h * h, axis=-1, keepdims=True)
    hn = (h * lax.rsqrt(ms + EPS) * gp_ref[...]).astype(BF16)
    gate = _sigmoid(jnp.dot(hn, wg_ref[...], preferred_element_type=F32))
    ple = jnp.dot(p_ref[...].astype(BF16), wp_ref[...], preferred_element_type=F32)
    h = h + gate * ple
    if final:
        ms = jnp.mean(h * h, axis=-1, keepdims=True)
        h = h * lax.rsqrt(ms + EPS) * fn_ref[...]
    o_ref[...] = h


def _post(h, mix_a, mix_b, b_col, p, w_out, g_ple, w_gate, w_proj, final_norm=None, *, tm=1024):
    t, d = h.shape
    half = w_out.shape[0] // 2
    pd = p.shape[1]
    final = final_norm is not None
    const = lambda shape: pl.BlockSpec(shape, lambda i: (0, 0))
    in_specs = [pl.BlockSpec((tm, d), lambda i: (i, 0)),
                pl.BlockSpec((tm, half), lambda i: (i, 0)),
                pl.BlockSpec((tm, half), lambda i: (i, b_col)),
                pl.BlockSpec((tm, pd), lambda i: (i, 0)),
                pl.BlockSpec((half, d), lambda i: (0, 0)),
                pl.BlockSpec((half, d), lambda i: (1, 0)),
                const((1, d)), const((d, d)), const((pd, d))]
    args = [h, mix_a, mix_b, p, w_out, w_out, g_ple.reshape(1, d), w_gate, w_proj]
    if final:
        in_specs.append(const((1, d)))
        args.append(final_norm.reshape(1, d))
    return pl.pallas_call(
        functools.partial(_post_kernel, final=final),
        grid=(t // tm,),
        in_specs=in_specs,
        out_specs=pl.BlockSpec((tm, d), lambda i: (i, 0)),
        out_shape=jax.ShapeDtypeStruct((t, d), F32),
        compiler_params=_params(1),
        name="out_proj_ple_final" if final else "out_proj_ple",
    )(*args)


def _block_diag(w):
    nb, bs, _ = w.shape
    eye = jnp.eye(nb, dtype=w.dtype)
    return jnp.einsum('ncd,nm->ncmd', w, eye).reshape(nb * bs, nb * bs)


def kernel(x, p, positions, norm_mix, norm_ple, w_ple_gate, w_ple_proj, w_in_e, conv_w, conv_b,
           lru_wa, lru_ba, lru_wx, lru_bx, lru_lambda, w_out_e, w_in_o, lam_q1, lam_k1, lam_q2,
           lam_k2, subln_g, w_out_o, final_norm):
    batch, seq, d = x.shape
    depth = norm_mix.shape[0]
    t = batch * seq
    lru_w = conv_w.shape[-1]
    sb_w = (w_in_e.shape[-1] - 2 * lru_w) // 4
    diff_w = w_in_o.shape[-1] // 4

    half = HEAD_DIM // 2
    inv_freq = ROPE_THETA ** (-jnp.arange(0, HEAD_DIM, 2, dtype=F32) / HEAD_DIM)
    ang = positions.astype(F32).reshape(t, 1) * inv_freq
    cos = jnp.tile(jnp.cos(ang), (1, LANES // half))
    sin = jnp.tile(jnp.concatenate([-jnp.sin(ang), jnp.sin(ang)], axis=1), (1, LANES // HEAD_DIM))

    h = x.reshape(t, d)
    for i in range(depth):
        j = i // 2
        last = i == depth - 1
        if i % 2 == 0:
            w = w_in_e[j].astype(BF16)
            q0 = 2 * lru_w
            k0, v0, g0 = q0 + sb_w, q0 + 2 * sb_w, q0 + 3 * sb_w
            w_main = jnp.concatenate([w[:, q0:k0], w[:, g0:]], axis=1)
            w_gates = jnp.concatenate([_block_diag(lru_wa[j]), _block_diag(lru_wx[j])], axis=1)
            b_gates = jnp.concatenate([lru_ba[j], lru_bx[j]])
            lru = (w[:, :q0], conv_w[j], conv_b[j], w_gates.astype(BF16), b_gates, lru_lambda[j])
            y, k_tiles, vt_tiles, oa = _in_proj(
                h, norm_mix[i], w_main, w[:, k0:v0], w[:, v0:g0], lru=lru, batch=batch, seq=seq,
                q_range=(0, sb_w), scale=0.5 * HEAD_DIM ** -0.5, permute_keys=True)
            ob = _sb_attention(y, k_tiles, vt_tiles, batch=batch, seq=seq,
                               q0=0, g0=sb_w, width=sb_w)
            mix_a, mix_b, b_col, w_out = oa, ob, 0, w_out_e[j]
        else:
            lambda_init = 0.8 - 0.6 * math.exp(-0.3 * i)
            w = w_in_o[j].astype(BF16)
            w_main = jnp.concatenate([w[:, :diff_w], w[:, 3 * diff_w:]], axis=1)
            y, k_tiles, vt_tiles = _in_proj(
                h, norm_mix[i], w_main, w[:, diff_w:2 * diff_w], w[:, 2 * diff_w:3 * diff_w],
                cos, sin, batch=batch, seq=seq, q_range=(0, diff_w),
                scale=HEAD_DIM ** -0.5 * math.log2(math.e), permute_keys=False,
                value_head=2 * HEAD_DIM)
            o = _diff_attention(y, k_tiles, vt_tiles, lam_q1[j], lam_k1[j], lam_q2[j], lam_k2[j],
                                subln_g[j], batch=batch, seq=seq, width=diff_w,
                                lambda_init=lambda_init)
            mix_a, mix_b, b_col, w_out = o, o, 1, w_out_o[j]
        h = _post(h, mix_a, mix_b, b_col, p[i].reshape(t, -1), w_out.astype(BF16), norm_ple[i],
                  w_ple_gate[i].astype(BF16), w_ple_proj[i].astype(BF16),
                  final_norm if last else None)
    return h.reshape(batch, seq, d)
```

```python
import functools
import math

import jax
import jax.numpy as jnp
from jax import lax
from jax.experimental import pallas as pl
from jax.experimental.pallas import tpu as pltpu

F32 = jnp.float32
BF16 = jnp.bfloat16

EPS = 1e-6
HEAD_DIM = 64
LRU_C = 8.0
CONV_WIDTH = 4
ROPE_THETA = 10000.0

LANES = 128
SUBLANES = 8
KV_TILE = 256
Q_TILE = 256
SEGMENT = KV_TILE // SUBLANES
SB_STREAMS = 8
DIFF_STREAMS = 8
SUM_ROWS = 16
VMEM_LIMIT = 48 * 1024 * 1024
IN_PROJ_VMEM_LIMIT = 56 * 1024 * 1024


def _params(n_axes, vmem_limit=VMEM_LIMIT):
    return pltpu.CompilerParams(
        dimension_semantics=("arbitrary",) * n_axes, vmem_limit_bytes=vmem_limit)


def _sigmoid(x):
    return 0.5 + 0.5 * jnp.tanh(0.5 * x)


def _silu(x):
    return x * _sigmoid(x)


def _rotary(y, cos, sin, first_half):
    width = y.shape[1]
    partner = jnp.where(first_half, pltpu.roll(y, width - HEAD_DIM // 2, 1),
                        pltpu.roll(y, HEAD_DIM // 2, 1))
    return y * cos + partner * sin


def _in_proj_kernel(*refs, chunk, rope, q_range, scale, permute_keys, lru_steps, value_head):
    x_ref, g_ref, wm_ref, wk_ref, wv_ref = refs[:5]
    if rope:
        cos_ref, sin_ref = refs[5:7]
    if lru_steps:
        wl_ref, cw_ref, cb_ref, wg_ref, bg_ref, lam_ref = refs[5:11]
        m_ref, k_ref, vt_ref, oa_ref, ext_ref, h_ref = refs[11:]
    else:
        m_ref, k_ref, vt_ref = refs[-3:]
    tm = x_ref.shape[0]
    x = x_ref[...]
    ms = jnp.mean(x * x, axis=-1, keepdims=True)
    xn = x * lax.rsqrt(ms + EPS) * g_ref[...]
    xn_bf = xn.astype(BF16)
    if rope:
        reps = chunk // LANES
        cos = jnp.concatenate([cos_ref[...]] * reps, axis=1)
        sin = jnp.concatenate([sin_ref[...]] * reps, axis=1)
        lane = lax.broadcasted_iota(jnp.int32, (tm, chunk), 1)
        first_half = (lane % HEAD_DIM) < (HEAD_DIM // 2)

    lru_tasks = []
    if lru_steps:
        width = oa_ref.shape[1]

        @pl.when(pl.program_id(0) % lru_steps == 0)
        def _():
            ext_ref[0:SUBLANES, :] = jnp.zeros((SUBLANES, width), F32)
            h_ref[...] = jnp.zeros((1, width), F32)

        xa = jnp.dot(xn_bf, wl_ref[:, :width], preferred_element_type=F32)
        ga = jnp.dot(xn_bf, wl_ref[:, width:], preferred_element_type=F32)
        xc = _causal_conv(xa, cw_ref, cb_ref, ext_ref)

    def main_chunk(c):
        lo = c * chunk
        y = jnp.dot(xn_bf, wm_ref[:, lo:lo + chunk], preferred_element_type=F32)
        if q_range[0] <= lo < q_range[1]:
            if rope:
                y = _rotary(y, cos, sin, first_half)
            y = y * scale
        m_ref[:, lo:lo + chunk] = y.astype(BF16)

    n_tiles = tm // KV_TILE
    if permute_keys:
        r = lax.broadcasted_iota(jnp.int32, (KV_TILE, KV_TILE), 0)
        c = lax.broadcasted_iota(jnp.int32, (KV_TILE, KV_TILE), 1)
        perm = jnp.where(c == (r % SUBLANES) * SEGMENT + r // SUBLANES, 1.0, 0.0).astype(BF16)
        xk = jnp.concatenate(
            [jnp.dot(perm, xn_bf[t * KV_TILE:(t + 1) * KV_TILE, :],
                     preferred_element_type=F32).astype(BF16) for t in range(n_tiles)], axis=0)
    else:
        xk = xn_bf

    low = lax.broadcasted_iota(jnp.int32, (KV_TILE, LANES), 1) < HEAD_DIM

    def key_chunk(c):
        kk = jnp.dot(xk, wk_ref[:, c * chunk:(c + 1) * chunk], preferred_element_type=F32)
        if rope:
            kk = _rotary(kk, cos, sin, first_half)
        for t in range(n_tiles):
            for b in range(chunk // LANES):
                blk = kk[t * KV_TILE:(t + 1) * KV_TILE, b * LANES:(b + 1) * LANES]
                nb = c * (chunk // LANES) + b
                k_ref[0, t, nb, 0:KV_TILE, :] = jnp.where(low, blk, 0.0).astype(BF16)
                k_ref[0, t, nb, KV_TILE:2 * KV_TILE, :] = jnp.where(low, 0.0, blk).astype(BF16)
    def value_chunk(c):
        vv = jnp.dot(xk, wv_ref[:, c * chunk:(c + 1) * chunk], preferred_element_type=F32)
        for t in range(n_tiles):
            vt = vv[t * KV_TILE:(t + 1) * KV_TILE, :].T.astype(BF16)
            if not value_head:
                vt_ref[0, t, c * chunk:(c + 1) * chunk, :] = vt
                continue
            for hh in range(chunk // value_head):
                r0 = (c * (chunk // value_head) + hh) * (value_head + SUM_ROWS)
                vt_ref[0, t, r0:r0 + value_head, :] = vt[hh * value_head:(hh + 1) * value_head]
                vt_ref[0, t, r0 + value_head:r0 + value_head + SUM_ROWS, :] = jnp.ones(
                    (SUM_ROWS, KV_TILE), BF16)

    for c in range(wm_ref.shape[1] // chunk):
        main_chunk(c)
    mxu_tasks = ([functools.partial(key_chunk, c) for c in range(wk_ref.shape[1] // chunk)]
                 + [functools.partial(value_chunk, c) for c in range(wv_ref.shape[1] // chunk)])
    if lru_steps:
        gates = jnp.dot(xc.astype(BF16), wg_ref[...], preferred_element_type=F32) + bg_ref[...]
        ts = tm // 4

        def lru_rows(r0):
            rows = slice(r0, r0 + ts)
            _rg_lru_rows(xc[rows], gates[rows], ga[rows], lam_ref, oa_ref, r0, h_ref)

        lru_tasks = [functools.partial(lru_rows, r0) for r0 in range(0, tm, ts)]
    for n in range(max(len(mxu_tasks), len(lru_tasks))):
        for tasks in (mxu_tasks, lru_tasks):
            if n < len(tasks):
                tasks[n]()


def _in_proj(h, g, w_main, w_k, w_v, cos=None, sin=None, lru=None, *, batch, seq, q_range,
             scale, permute_keys, value_head=0, tm=1024, chunk=512):
    t, d = h.shape
    n_main, n_k, n_v = w_main.shape[1], w_k.shape[1], w_v.shape[1]
    vt_rows = n_v // value_head * (value_head + SUM_ROWS) if value_head else n_v
    rope = cos is not None
    assert not (rope and lru is not None)
    assert seq % tm == 0 and tm % KV_TILE == 0
    assert n_main % chunk == 0 and n_k % chunk == 0 and n_v % chunk == 0 and chunk % LANES == 0
    steps = seq // tm
    tiles = tm // KV_TILE
    nkb = seq // KV_TILE
    const = lambda shape: pl.BlockSpec(shape, lambda i: (0, 0))
    in_specs = [pl.BlockSpec((tm, d), lambda i: (i, 0)), const((1, d)),
                const((d, n_main)), const((d, n_k)), const((d, n_v))]
    args = [h, g.reshape(1, d), w_main, w_k, w_v]
    if rope:
        in_specs += [pl.BlockSpec((tm, LANES), lambda i: (i, 0))] * 2
        args += [cos, sin]
    out_specs = [pl.BlockSpec((tm, n_main), lambda i: (i, 0)),
                 pl.BlockSpec((1, tiles, n_k // LANES, 2 * KV_TILE, LANES),
                              lambda i: (i // steps, i % steps, 0, 0, 0)),
                 pl.BlockSpec((1, tiles, vt_rows, KV_TILE),
                              lambda i: (i // steps, i % steps, 0, 0))]
    out_shape = [jax.ShapeDtypeStruct((t, n_main), BF16),
                 jax.ShapeDtypeStruct((batch, nkb, n_k // LANES, 2 * KV_TILE, LANES), BF16),
                 jax.ShapeDtypeStruct((batch, nkb, vt_rows, KV_TILE), BF16)]
    scratch = []
    if lru is not None:
        w_lru, conv_w, conv_b, w_gates, b_gates, lam = lru
        width = conv_w.shape[1]
        in_specs += [const((d, 2 * width)), const((CONV_WIDTH, width)), const((1, width)),
                     const((width, 2 * width)), const((1, 2 * width)), const((1, width))]
        args += [w_lru, conv_w, conv_b.reshape(1, width), w_gates,
                 b_gates.reshape(1, 2 * width), lam.reshape(1, width)]
        out_specs.append(pl.BlockSpec((tm, width), lambda i: (i, 0)))
        out_shape.append(jax.ShapeDtypeStruct((t, width), BF16))
        scratch = [pltpu.VMEM((tm + SUBLANES, width), F32), pltpu.VMEM((1, width), F32)]
    return pl.pallas_call(
        functools.partial(_in_proj_kernel, chunk=chunk, rope=rope, q_range=q_range,
                          scale=scale, permute_keys=permute_keys, value_head=value_head,
                          lru_steps=steps if lru is not None else 0),
        grid=(t // tm,),
        in_specs=in_specs,
        out_specs=out_specs,
        out_shape=out_shape,
        scratch_shapes=scratch,
        compiler_params=_params(1, IN_PROJ_VMEM_LIMIT),
        name="in_proj_rope" if rope else "in_proj_lru",
    )(*args)


def _causal_conv(xa, cw_ref, cb_ref, ext_ref):
    ts = xa.shape[0]
    ext_ref[SUBLANES:SUBLANES + ts, :] = xa
    xc = cb_ref[...] + cw_ref[CONV_WIDTH - 1:CONV_WIDTH, :] * xa
    for j in range(1, CONV_WIDTH):
        k = CONV_WIDTH - 1 - j
        xc = xc + cw_ref[k:k + 1, :] * ext_ref[SUBLANES - j:SUBLANES - j + ts, :]
    ext_ref[0:SUBLANES, :] = ext_ref[ts:ts + SUBLANES, :]
    return xc


def _rg_lru_rows(xc, gates, ga, lam_ref, o_ref, row0, h_ref):
    ts, width = xc.shape
    r = _sigmoid(gates[:, :width])
    i = _sigmoid(gates[:, width:])
    lam = lam_ref[...]
    log_sig_lam = jnp.minimum(lam, 0.0) - jnp.log(1.0 + jnp.exp(-jnp.abs(lam)))
    log_a = LRU_C * r * log_sig_lam
    a = jnp.exp(log_a)
    u = jnp.sqrt(1.0 - a * a) * (i * xc)

    groups = ts // SUBLANES
    a = a.reshape(groups, SUBLANES, width)
    u = u.reshape(groups, SUBLANES, width)
    row = lax.broadcasted_iota(jnp.int32, (groups, SUBLANES, width), 1)
    step = 1
    while step < SUBLANES:
        valid = row >= step
        a_prev = pltpu.roll(a, step, 1)
        u_prev = pltpu.roll(u, step, 1)
        u = jnp.where(valid, a * u_prev + u, u)
        a = jnp.where(valid, a * a_prev, a)
        step *= 2
    gate = _silu(ga)
    state = h_ref[...]
    for pair in range(ts // (2 * SUBLANES)):
        hs = []
        for grp in (2 * pair, 2 * pair + 1):
            h = u[grp] + a[grp] * state
            state = h[SUBLANES - 1:SUBLANES, :]
            hs.append(h)
        rows = slice(2 * pair * SUBLANES, 2 * (pair + 1) * SUBLANES)
        o_ref[row0 + rows.start:row0 + rows.stop, :] = (
            jnp.concatenate(hs, axis=0) * gate[rows]).astype(BF16)
    h_ref[...] = state


def _store_transposed_queries(q_ref, qt_scr):
    for blk in range(qt_scr.shape[0]):
        qt_scr[blk] = q_ref[:, blk * LANES:(blk + 1) * LANES].astype(F32).T.astype(BF16)


def _score_stage(k_ref, kb, qt_scr, s_scr, with_max=False):
    tile_max = []
    for blk in range(qt_scr.shape[0]):
        both = jnp.dot(k_ref[0, kb, blk], qt_scr[blk], preferred_element_type=F32)
        for half in range(2):
            s = both[half * KV_TILE:(half + 1) * KV_TILE]
            s_scr[2 * blk + half] = s
            if with_max:
                tile_max.append(jnp.max(s, axis=0, keepdims=True))
    return tuple(tile_max)


def _attention_scratch(streams, tq, value_rows):
    return [pltpu.VMEM((streams // 2, LANES, tq), BF16),
            pltpu.VMEM((streams, KV_TILE, tq), F32),
            pltpu.VMEM((streams, KV_TILE, tq), BF16),
            pltpu.VMEM((streams, value_rows, tq), F32)]


def _sb_kernel(q_ref, k_ref, vt_ref, g_ref, o_ref, qt_scr, s_scr, w_scr, acc_scr, *, tq):
    qi = pl.program_id(2)
    streams = s_scr.shape[0]
    _store_transposed_queries(q_ref, qt_scr)
    acc_scr[...] = jnp.zeros(acc_scr.shape, F32)

    r = lax.broadcasted_iota(jnp.int32, (KV_TILE, tq), 0)
    c = lax.broadcasted_iota(jnp.int32, (KV_TILE, tq), 1)
    causal = ((r % SUBLANES) * SEGMENT + r // SUBLANES) < c
    srow = lax.broadcasted_iota(jnp.int32, (SUBLANES, tq), 0)

    def weight_stage(later, masked):
        new_later = []
        for g in range(streams):
            run = jnp.ones((SUBLANES, tq), F32)
            for v in reversed(range(SEGMENT)):
                rows = slice(v * SUBLANES, (v + 1) * SUBLANES)
                half_tanh = 0.5 * jnp.tanh(s_scr[g, rows, :])
                beta = 0.5 + half_tanh
                rest = 0.5 - half_tanh
                if masked:
                    beta = jnp.where(causal[rows], beta, 0.0)
                    rest = jnp.where(causal[rows], rest, 1.0)
                s_scr[g, rows, :] = beta * run
                run = run * rest
            off = jnp.broadcast_to(later[g], (SUBLANES, tq))
            for s2 in range(1, SUBLANES):
                off = off * jnp.where(srow < s2, run[s2:s2 + 1, :], 1.0)
            off2 = jnp.concatenate([off, off], axis=0)
            for v in range(SEGMENT // 2):
                rows = slice(2 * v * SUBLANES, 2 * (v + 1) * SUBLANES)
                w_scr[g, rows, :] = (s_scr[g, rows, :] * off2).astype(BF16)
            new_later.append((off * run)[0:1, :])
        return tuple(new_later)

    def value_stage(t):
        vt_tile = vt_ref[0, qi - t]
        for g in range(streams):
            acc_scr[g] += jnp.dot(vt_tile[g * HEAD_DIM:(g + 1) * HEAD_DIM, :], w_scr[g],
                                  preferred_element_type=F32)

    _score_stage(k_ref, qi, qt_scr, s_scr)
    later = weight_stage(tuple(jnp.ones((1, tq), F32) for _ in range(streams)), True)
    _score_stage(k_ref, jnp.maximum(qi - 1, 0), qt_scr, s_scr)

    def step(j, later):
        value_stage(j - 1)
        later = weight_stage(later, False)
        _score_stage(k_ref, qi - (j + 1), qt_scr, s_scr)
        return later

    later = lax.fori_loop(1, qi, step, later)

    @pl.when(qi > 0)
    def _():
        value_stage(qi - 1)
        weight_stage(later, False)

    value_stage(qi)
    out_t = acc_scr[...].reshape(streams * HEAD_DIM, tq)
    o_ref[...] = (out_t.T * _silu(g_ref[...].astype(F32))).astype(BF16)


def _sb_attention(y, k_tiles, vt_tiles, *, batch, seq, q0, g0, width):
    nq = seq // Q_TILE
    nkb = seq // KV_TILE
    cols = SB_STREAMS * HEAD_DIM
    assert seq % Q_TILE == 0 and Q_TILE == KV_TILE and width % cols == 0
    return pl.pallas_call(
        functools.partial(_sb_kernel, tq=Q_TILE),
        grid=(batch, width // cols, nq),
        in_specs=[pl.BlockSpec((Q_TILE, cols), lambda b, p, i: (b * nq + i, q0 // cols + p)),
                  pl.BlockSpec((1, nkb, cols // LANES, 2 * KV_TILE, LANES),
                               lambda b, p, i: (b, 0, p, 0, 0)),
                  pl.BlockSpec((1, nkb, cols, KV_TILE), lambda b, p, i: (b, 0, p, 0)),
                  pl.BlockSpec((Q_TILE, cols), lambda b, p, i: (b * nq + i, g0 // cols + p))],
        out_specs=pl.BlockSpec((Q_TILE, cols), lambda b, p, i: (b * nq + i, p)),
        out_shape=jax.ShapeDtypeStruct((batch * seq, width), BF16),
        scratch_shapes=_attention_scratch(SB_STREAMS, Q_TILE, HEAD_DIM),
        compiler_params=_params(3),
        name="stick_breaking_attention",
    )(y, k_tiles, vt_tiles, y)


def _diff_kernel(q_ref, k_ref, vt_ref, g_ref, lq1_ref, lk1_ref, lq2_ref, lk2_ref, sg_ref,
                 o_ref, qt_scr, s_scr, p_scr, acc_scr, *, tq, lambda_init):
    qi = pl.program_id(2)
    streams = s_scr.shape[0]
    _store_transposed_queries(q_ref, qt_scr)
    acc_scr[...] = jnp.zeros(acc_scr.shape, F32)
    r = lax.broadcasted_iota(jnp.int32, (KV_TILE, tq), 0)
    c = lax.broadcasted_iota(jnp.int32, (KV_TILE, tq), 1)
    causal = r <= c
    vdim = 2 * HEAD_DIM
    vrows = vdim + SUM_ROWS

    def softmax_stage(stats, tile_max, masked):
        new = []
        for g in range(streams):
            m, _ = stats[g]
            s = s_scr[g]
            if masked:
                s = jnp.where(causal, s, -jnp.inf)
                m_new = jnp.maximum(m, jnp.max(s, axis=0, keepdims=True))
            else:
                m_new = jnp.maximum(m, tile_max[g])
            p = jnp.exp2(s - m_new)
            alpha = jnp.exp2(m - m_new)
            p_scr[g] = p.astype(BF16)
            new.append((m_new, alpha))
        return tuple(new)

    def value_stage(t, stats):
        vt_tile = vt_ref[0, qi - t]
        for g in range(streams):
            hd = g // 2
            acc_scr[g] = stats[g][1] * acc_scr[g] + jnp.dot(
                vt_tile[hd * vrows:(hd + 1) * vrows, :], p_scr[g], preferred_element_type=F32)

    init = tuple((jnp.full((1, tq), -jnp.inf, F32), jnp.zeros((1, tq), F32))
                 for _ in range(streams))
    _score_stage(k_ref, qi, qt_scr, s_scr)
    stats = softmax_stage(init, None, True)
    tile_max = _score_stage(k_ref, jnp.maximum(qi - 1, 0), qt_scr, s_scr, True)

    def step(j, carry):
        stats, tile_max = carry
        value_stage(j - 1, stats)
        stats = softmax_stage(stats, tile_max, False)
        return stats, _score_stage(k_ref, qi - (j + 1), qt_scr, s_scr, True)

    stats, tile_max = lax.fori_loop(1, qi, step, (stats, tile_max))

    def last_step(stats):
        value_stage(qi - 1, stats)
        return softmax_stage(stats, tile_max, False)

    stats = lax.cond(qi > 0, last_step, lambda st: st, stats)
    value_stage(qi, stats)

    lam = (jnp.exp(jnp.sum(lq1_ref[...] * lk1_ref[...], axis=-1, keepdims=True))
           - jnp.exp(jnp.sum(lq2_ref[...] * lk2_ref[...], axis=-1, keepdims=True))
           + lambda_init)
    head_gain = sg_ref[...] * (1.0 - lambda_init)
    for hd in range(streams // 2):
        acc1, acc2 = acc_scr[2 * hd], acc_scr[2 * hd + 1]
        inv1 = 1.0 / acc1[vdim:vdim + 1]
        inv2 = lam / acc2[vdim:vdim + 1]
        out_t = acc1[:vdim] * inv1 - acc2[:vdim] * inv2
        ms = jnp.mean(out_t * out_t, axis=0, keepdims=True)
        out = (out_t * lax.rsqrt(ms + EPS)).T
        cols = slice(hd * vdim, (hd + 1) * vdim)
        o_ref[:, cols] = (out * head_gain * _silu(g_ref[:, cols].astype(F32))).astype(BF16)


def _diff_attention(y, k_tiles, vt_tiles, lq1, lk1, lq2, lk2, subln_g, *, batch, seq, width,
                    lambda_init):
    nq = seq // Q_TILE
    nkb = seq // KV_TILE
    cols = DIFF_STREAMS * HEAD_DIM
    assert seq % Q_TILE == 0 and Q_TILE == KV_TILE and width % cols == 0
    g_col = width // cols
    vt_rows = DIFF_STREAMS // 2 * (2 * HEAD_DIM + SUM_ROWS)
    vec = lambda a: a.reshape(1, -1)
    small = lambda n: pl.BlockSpec((1, n), lambda b, h, i: (0, 0))
    return pl.pallas_call(
        functools.partial(_diff_kernel, tq=Q_TILE, lambda_init=lambda_init),
        grid=(batch, width // cols, nq),
        in_specs=[pl.BlockSpec((Q_TILE, cols), lambda b, h, i: (b * nq + i, h)),
                  pl.BlockSpec((1, nkb, cols // LANES, 2 * KV_TILE, LANES),
                               lambda b, h, i: (b, 0, h, 0, 0)),
                  pl.BlockSpec((1, nkb, vt_rows, KV_TILE), lambda b, h, i: (b, 0, h, 0)),
                  pl.BlockSpec((Q_TILE, cols), lambda b, h, i: (b * nq + i, g_col + h)),
                  small(HEAD_DIM), small(HEAD_DIM), small(HEAD_DIM), small(HEAD_DIM),
                  small(2 * HEAD_DIM)],
        out_specs=pl.BlockSpec((Q_TILE, cols), lambda b, h, i: (b * nq + i, h)),
        out_shape=jax.ShapeDtypeStruct((batch * seq, width), BF16),
        scratch_shapes=_attention_scratch(DIFF_STREAMS, Q_TILE, 2 * HEAD_DIM + SUM_ROWS),
        compiler_params=_params(3),
        name="differential_attention",
    )(y, k_tiles, vt_tiles, y, vec(lq1), vec(lk1), vec(lq2), vec(lk2), vec(subln_g))


def _post_kernel(*refs, final):
    if final:
        (h_ref, ma_ref, mb_ref, p_ref, woa_ref, wob_ref, gp_ref, wg_ref, wp_ref,
         fn_ref, o_ref) = refs
    else:
        h_ref, ma_ref, mb_ref, p_ref, woa_ref, wob_ref, gp_ref, wg_ref, wp_ref, o_ref = refs
    h = (h_ref[...]
         + jnp.dot(ma_ref[...], woa_ref[...], preferred_element_type=F32)
         + jnp.dot(mb_ref[...], wob_ref[...], preferred_element_type=F32))
    ms = jnp.mean(h * h, axis=-1, keepdims=True)
    hn = (h * lax.rsqrt(ms + EPS) * gp_ref[...]).astype(BF16)
    gate = _sigmoid(jnp.dot(hn, wg_ref[...], preferred_element_type=F32))
    ple = jnp.dot(p_ref[...].astype(BF16), wp_ref[...], preferred_element_type=F32)
    h = h + gate * ple
    if final:
        ms = jnp.mean(h * h, axis=-1, keepdims=True)
        h = h * lax.rsqrt(ms + EPS) * fn_ref[...]
    o_ref[...] = h


def _post(h, mix_a, mix_b, b_col, p, w_out, g_ple, w_gate, w_proj, final_norm=None, *, tm=1024):
    t, d = h.shape
    assert t % tm == 0 and w_out.shape[0] % 2 == 0
    half = w_out.shape[0] // 2
    pd = p.shape[1]
    final = final_norm is not None
    const = lambda shape: pl.BlockSpec(shape, lambda i: (0, 0))
    in_specs = [pl.BlockSpec((tm, d), lambda i: (i, 0)),
                pl.BlockSpec((tm, half), lambda i: (i, 0)),
                pl.BlockSpec((tm, half), lambda i: (i, b_col)),
                pl.BlockSpec((tm, pd), lambda i: (i, 0)),
                pl.BlockSpec((half, d), lambda i: (0, 0)),
                pl.BlockSpec((half, d), lambda i: (1, 0)),
                const((1, d)), const((d, d)), const((pd, d))]
    args = [h, mix_a, mix_b, p, w_out, w_out, g_ple.reshape(1, d), w_gate, w_proj]
    if final:
        in_specs.append(const((1, d)))
        args.append(final_norm.reshape(1, d))
    return pl.pallas_call(
        functools.partial(_post_kernel, final=final),
        grid=(t // tm,),
        in_specs=in_specs,
        out_specs=pl.BlockSpec((tm, d), lambda i: (i, 0)),
        out_shape=jax.ShapeDtypeStruct((t, d), F32),
        compiler_params=_params(1),
        name="out_proj_ple_final" if final else "out_proj_ple",
    )(*args)


def _block_diag(w):
    nb, bs, _ = w.shape
    eye = jnp.eye(nb, dtype=w.dtype)
    return jnp.einsum('ncd,nm->ncmd', w, eye).reshape(nb * bs, nb * bs)


def kernel(x, p, positions, norm_mix, norm_ple, w_ple_gate, w_ple_proj, w_in_e, conv_w, conv_b,
           lru_wa, lru_ba, lru_wx, lru_bx, lru_lambda, w_out_e, w_in_o, lam_q1, lam_k1, lam_q2,
           lam_k2, subln_g, w_out_o, final_norm):
    batch, seq, d = x.shape
    depth = norm_mix.shape[0]
    t = batch * seq
    lru_w = conv_w.shape[-1]
    sb_w = (w_in_e.shape[-1] - 2 * lru_w) // 4
    diff_w = w_in_o.shape[-1] // 4

    half = HEAD_DIM // 2
    inv_freq = ROPE_THETA ** (-jnp.arange(0, HEAD_DIM, 2, dtype=F32) / HEAD_DIM)
    ang = positions.astype(F32).reshape(t, 1) * inv_freq
    cos = jnp.tile(jnp.cos(ang), (1, LANES // half))
    sin = jnp.tile(jnp.concatenate([-jnp.sin(ang), jnp.sin(ang)], axis=1), (1, LANES // HEAD_DIM))

    h = x.reshape(t, d)
    for i in range(depth):
        j = i // 2
        last = i == depth - 1
        if i % 2 == 0:
            w = w_in_e[j].astype(BF16)
            q0 = 2 * lru_w
            k0, v0, g0 = q0 + sb_w, q0 + 2 * sb_w, q0 + 3 * sb_w
            w_main = jnp.concatenate([w[:, q0:k0], w[:, g0:]], axis=1)
            w_gates = jnp.concatenate([_block_diag(lru_wa[j]), _block_diag(lru_wx[j])], axis=1)
            b_gates = jnp.concatenate([lru_ba[j], lru_bx[j]])
            lru = (w[:, :q0], conv_w[j], conv_b[j], w_gates.astype(BF16), b_gates, lru_lambda[j])
            y, k_tiles, vt_tiles, oa = _in_proj(
                h, norm_mix[i], w_main, w[:, k0:v0], w[:, v0:g0], lru=lru, batch=batch, seq=seq,
                q_range=(0, sb_w), scale=0.5 * HEAD_DIM ** -0.5, permute_keys=True)
            ob = _sb_attention(y, k_tiles, vt_tiles, batch=batch, seq=seq,
                               q0=0, g0=sb_w, width=sb_w)
            mix_a, mix_b, b_col, w_out = oa, ob, 0, w_out_e[j]
        else:
            lambda_init = 0.8 - 0.6 * math.exp(-0.3 * i)
            w = w_in_o[j].astype(BF16)
            w_main = jnp.concatenate([w[:, :diff_w], w[:, 3 * diff_w:]], axis=1)
            y, k_tiles, vt_tiles = _in_proj(
                h, norm_mix[i], w_main, w[:, diff_w:2 * diff_w], w[:, 2 * diff_w:3 * diff_w],
                cos, sin, batch=batch, seq=seq, q_range=(0, diff_w),
                scale=HEAD_DIM ** -0.5 * math.log2(math.e), permute_keys=False,
                value_head=2 * HEAD_DIM)
            o = _diff_attention(y, k_tiles, vt_tiles, lam_q1[j], lam_k1[j], lam_q2[j], lam_k2[j],
                                subln_g[j], batch=batch, seq=seq, width=diff_w,
                                lambda_init=lambda_init)
            mix_a, mix_b, b_col, w_out = o, o, 1, w_out_o[j]
        h = _post(h, mix_a, mix_b, b_col, p[i].reshape(t, -1), w_out.astype(BF16), norm_ple[i],
                  w_ple_gate[i].astype(BF16), w_ple_proj[i].astype(BF16),
                  final_norm if last else None)
    return h.reshape(batch, seq, d)
```
